```python
import math
import jax, jax.numpy as jnp
from jax import lax
import numpy as np

D_MODEL = 1024
BATCH = 16
SEQ = 2048
DEPTH = 2
DEC_BATCH = 128
DEC_SEQ = 1
PAST_LEN = 8192
PAGE_SIZE = 128

ML_HEADS = 4
ML_HEAD_DIM = 128
ML_WIDTH = ML_HEADS * ML_HEAD_DIM
ML_CHUNK = 64
POOL_WINDOWS = (2, 4, 8, 16)
POOL_GROUP = 128
POOL_WIDTH = len(POOL_WINDOWS) * POOL_GROUP
POOL_BUF = max(POOL_WINDOWS) - 1
MLA_HEADS = 8
MLA_NOPE = 128
MLA_ROPE = 64
MLA_V = 128
MLA_Q_RANK = 512
MLA_KV_RANK = 256
MLA_SCALE = (MLA_NOPE + MLA_ROPE) ** -0.5
ROPE_THETA = 10000.0
Q_BLOCK = 128
D_FF = 4 * D_MODEL
EPS = 1e-6
POOL_NUM = 5
POOL_DEN = 4

IN0_WIDTH = 4 * ML_WIDTH + POOL_WIDTH + 2 * ML_HEADS
IN1_WIDTH = MLA_Q_RANK + MLA_KV_RANK + MLA_ROPE

kernel_name = 'hybrid_mlstm_pool_mla_decode_step'

F32 = jnp.float32


def rmsnorm(x, g):
    xf = x.astype(F32)
    y = xf * lax.rsqrt(jnp.mean(xf * xf, axis=-1, keepdims=True) + EPS)
    return (y * g.astype(F32)).astype(x.dtype)


def head_norm(h, g):
    b, t, nh, dh = h.shape
    hc = h - jnp.mean(h, axis=-1, keepdims=True)
    y = hc * lax.rsqrt(jnp.mean(hc * hc, axis=-1, keepdims=True) + EPS)
    return y.reshape(b, t, nh * dh) * g.astype(F32)


def sqrelu_mlp(h, w_up, w_down):
    a = jax.nn.relu(h @ w_up)
    return (a * a) @ w_down


def rope(x, pos):
    half = x.shape[-1] // 2
    inv = ROPE_THETA ** (-jnp.arange(half, dtype=F32) * 2.0 / x.shape[-1])
    ang = pos.astype(F32)[:, None] * inv[None, :]
    cos = jnp.cos(ang)[None, :, None, :]
    sin = jnp.sin(ang)[None, :, None, :]
    xf = x.astype(F32)
    x1, x2 = xf[..., :half], xf[..., half:]
    return jnp.concatenate([x1 * cos - x2 * sin, x2 * cos + x1 * sin], axis=-1)


def _to_chunks(a, n_chunks, chunk):
    a = a.reshape((a.shape[0], n_chunks, chunk) + a.shape[2:])
    return jnp.moveaxis(jnp.moveaxis(a, 1, 0), 3, 2)


def mlstm_scan(q, k, v, ig, lf, c0, n0, m0):
    b, t, nh, dh = q.shape
    chunk = math.gcd(t, ML_CHUNK)
    nc = t // chunk
    causal = jnp.tril(jnp.ones((chunk, chunk), dtype=bool))

    def step(carry, inp):
        c, n, m = carry
        qc, kc, vc, ic, fc = inp
        bcum = jnp.cumsum(fc, axis=-1)
        log_intra = jnp.where(causal, bcum[..., :, None] - bcum[..., None, :] + ic[..., None, :], -jnp.inf)
        log_inter = bcum + m[..., None]
        m_t = jnp.maximum(log_inter, jnp.max(log_intra, axis=-1))
        w_intra = jnp.exp(log_intra - m_t[..., None])
        w_inter = jnp.exp(log_inter - m_t)
        s = jnp.einsum('bhtk,bhsk->bhts', qc, kc) * w_intra
        num = jnp.einsum('bhts,bhsv->bhtv', s, vc) + w_inter[..., None] * jnp.einsum('bhvk,bhtk->bhtv', c, qc)
        den = jnp.sum(s, axis=-1) + w_inter * jnp.einsum('bhk,bhtk->bht', n, qc)
        h = num / jnp.maximum(jnp.abs(den), jnp.exp(-m_t))[..., None]
        m_new = m_t[..., -1]
        w_tok = jnp.exp(bcum[..., -1:] - bcum + ic - m_new[..., None])
        w_old = jnp.exp(bcum[..., -1] + m - m_new)
        c_new = w_old[..., None, None] * c + jnp.einsum('bhs,bhsv,bhsk->bhvk', w_tok, vc, kc)
        n_new = w_old[..., None] * n + jnp.einsum('bhs,bhsk->bhk', w_tok, kc)
        return (c_new, n_new, m_new), h

    xs = (_to_chunks(q, nc, chunk), _to_chunks(k, nc, chunk), _to_chunks(v, nc, chunk),
          _to_chunks(ig, nc, chunk), _to_chunks(lf, nc, chunk))
    (c, n, m), hs = lax.scan(step, (c0, n0, m0), xs)
    h = jnp.transpose(hs, (1, 0, 3, 2, 4)).reshape(b, t, nh, dh)
    return h, c, n, m


def pool_mix(u, buf, pos0, pool_w, pool_scale):
    b, t, _ = u.shape
    full = jnp.concatenate([buf.astype(u.dtype), u], axis=1)
    cs = jnp.cumsum(full.astype(F32), axis=1)
    cs = jnp.concatenate([jnp.zeros_like(cs[:, :1]), cs], axis=1)
    pos = pos0 + jnp.arange(t)
    uf = u.astype(F32)
    outs = []
    for g, win in enumerate(POOL_WINDOWS):
        sl = slice(g * POOL_GROUP, (g + 1) * POOL_GROUP)
        wsum = cs[:, POOL_BUF + 1:POOL_BUF + t + 1, sl] - cs[:, POOL_BUF + 1 - win:POOL_BUF + t + 1 - win, sl]
        cnt = jnp.minimum(win, pos + 1).astype(F32)
        pooled = wsum / cnt[None, :, None] - uf[..., sl]
        outs.append(jnp.einsum('btc,cd->btd', pooled, pool_w[g].astype(F32)))
    y = jnp.concatenate(outs, axis=-1) * pool_scale.astype(F32)
    return y.astype(u.dtype), full[:, -POOL_BUF:]


def mlstm_pool_mixer(h, ml_state, pool_buf, pos0, p):
    b, t, _ = h.shape
    W, PW, H = ML_WIDTH, POOL_WIDTH, ML_HEADS
    z = h @ p['w_in0']
    hd = (b, t, H, ML_HEAD_DIM)
    q = z[..., :W].astype(F32).reshape(hd)
    k = z[..., W:2 * W].astype(F32).reshape(hd) * ML_HEAD_DIM ** -0.5
    v = z[..., 2 * W:3 * W].astype(F32).reshape(hd)
    o_gate = jax.nn.sigmoid(z[..., 3 * W:4 * W].astype(F32))
    u = z[..., 4 * W:4 * W + PW]
    ig = z[..., 4 * W + PW:4 * W + PW + H].astype(F32) + p['mlstm_b_i'].astype(F32)
    lf = jax.nn.log_sigmoid(z[..., 4 * W + PW + H:].astype(F32) + p['mlstm_b_f'].astype(F32))
    c0, n0, m0 = ml_state
    hm, c, n, m = mlstm_scan(q, k, v, ig, lf, c0.astype(F32), n0.astype(F32), m0.astype(F32))
    hm = head_norm(hm, p['mlstm_norm']) * o_gate
    hp, new_buf = pool_mix(u, pool_buf, pos0, p['pool_w'], p['pool_scale'])
    y = jnp.concatenate([hm.astype(h.dtype), hp], axis=-1) @ p['w_out0']
    return y, (c, n, m), new_buf


def mla_attend(q_lat, q_pe, keys_c, keys_pe, q_pos, k_pos):
    s = (jnp.einsum('bthr,bsr->bhts', q_lat, keys_c) + jnp.einsum('bthd,bsd->bhts', q_pe, keys_pe)) * MLA_SCALE
    s = jnp.where(k_pos[None, :] <= q_pos[:, None], s, -jnp.inf)
    pr = jax.nn.softmax(s, axis=-1)
    return jnp.einsum('bhts,bsr->bthr', pr, keys_c)


def mla_mixer(h, past, pos0, p):
    b, t, _ = h.shape
    pos = pos0 + jnp.arange(t)
    z = h @ p['w_in1']
    cq = rmsnorm(z[..., :MLA_Q_RANK], p['mla_q_norm'])
    ckv = rmsnorm(z[..., MLA_Q_RANK:MLA_Q_RANK + MLA_KV_RANK], p['mla_kv_norm'])
    kpe = rope(z[..., MLA_Q_RANK + MLA_KV_RANK:][:, :, None, :], pos)[:, :, 0, :]
    qf = (cq @ p['w_q_up']).reshape(b, t, MLA_HEADS, MLA_NOPE + MLA_ROPE)
    q_pe = rope(qf[..., MLA_NOPE:], pos)
    q_lat = jnp.einsum('bthd,rhd->bthr', qf[..., :MLA_NOPE].astype(F32), p['w_uk'].astype(F32))
    ckv_f = ckv.astype(F32)
    if past is None:
        blk = math.gcd(t, Q_BLOCK)
        nb = t // blk
        qlb = jnp.swapaxes(q_lat.reshape(b, nb, blk, MLA_HEADS, MLA_KV_RANK), 0, 1)
        qpb = jnp.swapaxes(q_pe.reshape(b, nb, blk, MLA_HEADS, MLA_ROPE), 0, 1)
        posb = pos.reshape(nb, blk)

        def one_block(args):
            ql, qp, qpos = args
            return mla_attend(ql, qp, ckv_f, kpe, qpos, pos)

        o = lax.map(one_block, (qlb, qpb, posb))
        o = jnp.swapaxes(o, 0, 1).reshape(b, t, MLA_HEADS, MLA_KV_RANK)
    else:
        cache_c, cache_pe, page_table = past
        n_pages = page_table.shape[1]
        past_len = n_pages * cache_c.shape[1]
        past_c = cache_c[page_table].reshape(b, past_len, MLA_KV_RANK).astype(F32)
        past_pe = cache_pe[page_table].reshape(b, past_len, MLA_ROPE).astype(F32)
        keys_c = jnp.concatenate([past_c, ckv_f], axis=1)
        keys_pe = jnp.concatenate([past_pe, kpe], axis=1)
        o = mla_attend(q_lat, q_pe, keys_c, keys_pe, pos, jnp.arange(past_len + t))
    heads = jnp.einsum('bthr,rhd->bthd', o, p['w_uv'].astype(F32)).reshape(b, t, MLA_HEADS * MLA_V)
    y = heads.astype(h.dtype) @ p['w_out1']
    return y, ckv, kpe.astype(h.dtype)


def trunk(x, ml_state, pool_buf, past, pos0, p):
    for layer in range(DEPTH):
        h = rmsnorm(x, p['norm_mix'][layer])
        if layer % 2 == 0:
            y, ml_new, buf_new = mlstm_pool_mixer(h, ml_state, pool_buf, pos0, p)
        else:
            y, ckv_new, kpe_new = mla_mixer(h, past, pos0, p)
        x = x + y
        x = x + sqrelu_mlp(rmsnorm(x, p['norm_ffn'][layer]), p['w_up'][layer], p['w_down'][layer])
    return rmsnorm(x, p['norm_final']), ml_new, buf_new, ckv_new, kpe_new


def setup_inputs(seed: int = 0) -> dict:
    key = jax.random.key(seed)
    ks = jax.random.split(key, 32)
    n_pages = PAST_LEN // PAGE_SIZE
    n_phys = (DEC_BATCH * n_pages * POOL_NUM) // POOL_DEN

    def nrm(k, shape, scale):
        return jax.random.normal(k, shape, F32) * scale

    def gain(k, shape):
        return 1.0 + 0.02 * jax.random.normal(k, shape, F32)

    page_table = jax.random.permutation(ks[8], n_phys)[:DEC_BATCH * n_pages].reshape(DEC_BATCH, n_pages).astype(jnp.int32)
    return {
        'x_prompt': nrm(ks[0], (BATCH, SEQ, D_MODEL), 1.0),
        'x_sample': nrm(ks[1], (DEC_BATCH, DEC_SEQ, D_MODEL), 1.0),
        'state_mlstm_C': nrm(ks[2], (DEC_BATCH, ML_HEADS, ML_HEAD_DIM, ML_HEAD_DIM), 0.3),
        'state_mlstm_n': nrm(ks[3], (DEC_BATCH, ML_HEADS, ML_HEAD_DIM), 0.3),
        'state_mlstm_m': nrm(ks[4], (DEC_BATCH, ML_HEADS), 0.5),
        'state_pool': nrm(ks[5], (DEC_BATCH, POOL_BUF, POOL_WIDTH), 1.0),
        'cache_latent': nrm(ks[6], (n_phys, PAGE_SIZE, MLA_KV_RANK), 1.0),
        'cache_rope_k': nrm(ks[7], (n_phys, PAGE_SIZE, MLA_ROPE), 1.0),
        'page_table': page_table,
        'norm_mix': gain(ks[9], (DEPTH, D_MODEL)),
        'norm_ffn': gain(ks[10], (DEPTH, D_MODEL)),
        'norm_final': gain(ks[11], (D_MODEL,)),
        'w_in0': nrm(ks[12], (D_MODEL, IN0_WIDTH), D_MODEL ** -0.5),
        'mlstm_b_i': nrm(ks[13], (ML_HEADS,), 0.1),
        'mlstm_b_f': 3.0 + nrm(ks[14], (ML_HEADS,), 0.5),
        'mlstm_norm': gain(ks[15], (ML_WIDTH,)),
        'pool_w': nrm(ks[16], (len(POOL_WINDOWS), POOL_GROUP, POOL_GROUP), POOL_GROUP ** -0.5),
        'pool_scale': gain(ks[17], (POOL_WIDTH,)),
        'w_out0': nrm(ks[18], (ML_WIDTH + POOL_WIDTH, D_MODEL), (ML_WIDTH + POOL_WIDTH) ** -0.5),
        'w_in1': nrm(ks[19], (D_MODEL, IN1_WIDTH), D_MODEL ** -0.5),
        'mla_q_norm': gain(ks[20], (MLA_Q_RANK,)),
        'mla_kv_norm': gain(ks[21], (MLA_KV_RANK,)),
        'w_q_up': nrm(ks[22], (MLA_Q_RANK, MLA_HEADS * (MLA_NOPE + MLA_ROPE)), MLA_Q_RANK ** -0.5),
        'w_uk': nrm(ks[23], (MLA_KV_RANK, MLA_HEADS, MLA_NOPE), MLA_KV_RANK ** -0.5),
        'w_uv': nrm(ks[24], (MLA_KV_RANK, MLA_HEADS, MLA_V), MLA_KV_RANK ** -0.5),
        'w_out1': nrm(ks[25], (MLA_HEADS * MLA_V, D_MODEL), (MLA_HEADS * MLA_V) ** -0.5),
        'w_up': nrm(ks[26], (DEPTH, D_MODEL, D_FF), D_MODEL ** -0.5),
        'w_down': nrm(ks[27], (DEPTH, D_FF, D_MODEL), D_FF ** -0.5),
    }


def reference(x_prompt, x_sample, state_mlstm_C, state_mlstm_n, state_mlstm_m, state_pool,
              cache_latent, cache_rope_k, page_table, norm_mix, norm_ffn, norm_final,
              w_in0, mlstm_b_i, mlstm_b_f, mlstm_norm, pool_w, pool_scale, w_out0,
              w_in1, mla_q_norm, mla_kv_norm, w_q_up, w_uk, w_uv, w_out1, w_up, w_down):
    p = {'norm_mix': norm_mix, 'norm_ffn': norm_ffn, 'norm_final': norm_final,
         'w_in0': w_in0, 'mlstm_b_i': mlstm_b_i, 'mlstm_b_f': mlstm_b_f, 'mlstm_norm': mlstm_norm,
         'pool_w': pool_w, 'pool_scale': pool_scale, 'w_out0': w_out0,
         'w_in1': w_in1, 'mla_q_norm': mla_q_norm, 'mla_kv_norm': mla_kv_norm, 'w_q_up': w_q_up,
         'w_uk': w_uk, 'w_uv': w_uv, 'w_out1': w_out1, 'w_up': w_up, 'w_down': w_down}
    bp = x_prompt.shape[0]
    ml_init = (jnp.zeros((bp, ML_HEADS, ML_HEAD_DIM, ML_HEAD_DIM), F32),
               jnp.zeros((bp, ML_HEADS, ML_HEAD_DIM), F32),
               jnp.zeros((bp, ML_HEADS), F32))
    buf_init = jnp.zeros((bp, POOL_BUF, POOL_WIDTH), x_prompt.dtype)
    y_prompt, (c_p, n_p, m_p), buf_p, ckv_p, kpe_p = trunk(x_prompt, ml_init, buf_init, None, 0, p)
    past_len = page_table.shape[1] * cache_latent.shape[1]
    y_sample, (c_s, n_s, m_s), buf_s, ckv_s, kpe_s = trunk(
        x_sample, (state_mlstm_C, state_mlstm_n, state_mlstm_m), state_pool,
        (cache_latent, cache_rope_k, page_table), past_len, p)
    return (y_prompt, y_sample, c_p, n_p, m_p, buf_p, ckv_p, kpe_p, c_s, n_s, m_s, buf_s, ckv_s, kpe_s)
```

```python
import functools
import math

import jax
import jax.numpy as jnp
from jax import lax
from jax.experimental import pallas as pl
from jax.experimental.pallas import tpu as pltpu

F32 = jnp.float32
BF16 = jnp.bfloat16

EPS = 1e-6
ML_HEADS = 4
ML_HEAD_DIM = 128
ML_WIDTH = ML_HEADS * ML_HEAD_DIM
POOL_WINDOWS = (2, 4, 8, 16)
POOL_GROUP = 128
POOL_WIDTH = len(POOL_WINDOWS) * POOL_GROUP
POOL_BUF = max(POOL_WINDOWS) - 1
POOL_HIST = POOL_BUF + 1
MLA_HEADS = 8
MLA_NOPE = 128
MLA_ROPE = 64
MLA_V = 128
MLA_Q_RANK = 512
MLA_KV_RANK = 256
MLA_SCALE = (MLA_NOPE + MLA_ROPE) ** -0.5
ROPE_THETA = 10000.0
LANES = 128
MLA_KEY_WIDTH = MLA_KV_RANK + LANES
Z_MAIN = 4 * ML_WIDTH + POOL_WIDTH

VMEM_LIMIT_BYTES = 48 * 1024 * 1024

_NT = (((1,), (1,)), ((), ()))
_TN = (((0,), (0,)), ((), ()))


def _dot(a, b):
    return jnp.dot(a, b, preferred_element_type=F32)


def _dot_nt(a, b):
    return lax.dot_general(a, b, _NT, preferred_element_type=F32)


def _dot_tn(a, b):
    return lax.dot_general(a, b, _TN, preferred_element_type=F32)


def _rms(x, g):
    return x * lax.rsqrt(jnp.mean(x * x, axis=-1, keepdims=True) + EPS) * g


def _log_sigmoid(x):
    return jnp.minimum(x, 0.0) - jnp.log1p(jnp.exp(-jnp.abs(x)))


def _head_norm(h, g):
    hc = h - jnp.mean(h, axis=-1, keepdims=True)
    return hc * lax.rsqrt(jnp.mean(hc * hc, axis=-1, keepdims=True) + EPS) * g


def _cumsum_lanes(x):
    n = x.shape[-1]
    lane = lax.broadcasted_iota(jnp.int32, x.shape, x.ndim - 1)
    s = 1
    while s < n:
        x = x + jnp.where(lane >= s, pltpu.roll(x, s, axis=x.ndim - 1), 0.0)
        s *= 2
    return x


def _params(*semantics):
    return pltpu.CompilerParams(dimension_semantics=semantics, vmem_limit_bytes=VMEM_LIMIT_BYTES)


def _full(shape):
    return pl.BlockSpec(shape, lambda *_: (0,) * len(shape))


def _mixer0_kernel(x_ref, g_ref, win_ref, wgi_ref, wgf_ref, bi_ref, bf_ref, gn_ref, pw_ref, ps_ref,
                   wout_ref, y_ref, c_out, n_out, m_out, buf_out,
                   z_s, asm_s, ext_s, c_s, n_s, m_s, *, tt, chunk, pos0):
    t = pl.program_id(1)

    @pl.when(t == 0)
    def _():
        c_s[...] = jnp.zeros_like(c_s)
        n_s[...] = jnp.zeros_like(n_s)
        m_s[...] = jnp.zeros_like(m_s)
        ext_s[0:POOL_HIST, :] = jnp.zeros((POOL_HIST, POOL_WIDTH), F32)

    x = x_ref[0]
    hn = _rms(x, g_ref[...]).astype(BF16)
    z_s[...] = _dot(hn, win_ref[...])
    ig_all = _dot_nt(wgi_ref[...], hn) + bi_ref[...]
    lf_all = _log_sigmoid(_dot_nt(wgf_ref[...], hn) + bf_ref[...])

    dh = ML_HEAD_DIM
    row = lax.broadcasted_iota(jnp.int32, (chunk, chunk), 0)
    col = lax.broadcasted_iota(jnp.int32, (chunk, chunk), 1)
    causal = row >= col
    diag = row == col
    neg_inf = jnp.float32(-jnp.inf)

    for c in range(tt // chunk):
        rows = slice(c * chunk, (c + 1) * chunk)
        ig = ig_all[:, rows]
        bcum = _cumsum_lanes(lf_all[:, rows])
        a = ig - bcum
        m_prev8 = m_s[:, 0:1]
        m_last8 = jnp.maximum(m_prev8, jnp.max(a, axis=1, keepdims=True))
        w_old8 = jnp.exp(m_prev8 - m_last8)
        m_new8 = bcum[:, chunk - 1:chunk] + m_last8
        for h in range(ML_HEADS):
            a_row = a[h:h + 1, :]
            b_row = bcum[h:h + 1, :]
            m_prev = m_prev8[h:h + 1, :]
            a_mat = jnp.where(causal, a_row, neg_inf)
            m_col = jnp.maximum(jnp.max(a_mat, axis=1, keepdims=True), m_prev)
            w_intra = jnp.exp(a_mat - m_col)
            w_inter = jnp.exp(m_prev - m_col)
            b_col = jnp.sum(jnp.where(diag, b_row, 0.0), axis=1, keepdims=True)
            a_col = jnp.sum(jnp.where(diag, a_row, 0.0), axis=1, keepdims=True)
            m_tok = b_col + m_col

            q = z_s[rows, h * dh:(h + 1) * dh]
            k = z_s[rows, ML_WIDTH + h * dh:ML_WIDTH + (h + 1) * dh] * (dh ** -0.5)
            v = z_s[rows, 2 * ML_WIDTH + h * dh:2 * ML_WIDTH + (h + 1) * dh]
            qb = q.astype(BF16)
            vb = v.astype(BF16)
            c_mat = c_s[h]
            n_row = n_s[h:h + 1, :]
            s = _dot_nt(qb, k.astype(BF16)) * w_intra
            num = _dot(s.astype(BF16), vb) + w_inter * _dot_nt(qb, c_mat.astype(BF16))
            den = jnp.sum(s, axis=1, keepdims=True) + w_inter * jnp.sum(q * n_row, axis=1, keepdims=True)
            hh = num / jnp.maximum(jnp.abs(den), jnp.exp(-m_tok))

            w_tok = jnp.exp(a_col - m_last8[h:h + 1, :])
            w_old = w_old8[h:h + 1, :]
            kw = k * w_tok
            c_s[h] = w_old * c_mat + _dot_tn(vb, kw.astype(BF16))
            n_s[h:h + 1, :] = w_old * n_row + jnp.sum(kw, axis=0, keepdims=True)

            o_gate = jax.nn.sigmoid(z_s[rows, 3 * ML_WIDTH + h * dh:3 * ML_WIDTH + (h + 1) * dh])
            hm = _head_norm(hh, gn_ref[:, h * dh:(h + 1) * dh]) * o_gate
            asm_s[rows, h * dh:(h + 1) * dh] = hm.astype(BF16)
        m_s[...] = jnp.broadcast_to(m_new8, m_s.shape)

    ext_s[POOL_HIST:POOL_HIST + tt, :] = z_s[:, 4 * ML_WIDTH:Z_MAIN]
    pos = pos0 + t * tt + lax.broadcasted_iota(jnp.int32, (tt, 1), 0)
    for g, win in enumerate(POOL_WINDOWS):
        lanes = slice(g * POOL_GROUP, (g + 1) * POOL_GROUP)
        u = ext_s[POOL_HIST:POOL_HIST + tt, lanes]
        wsum = u
        for j in range(1, win):
            wsum = wsum + ext_s[POOL_HIST - j:POOL_HIST - j + tt, lanes]
        cnt = jnp.minimum(win, pos + 1).astype(F32)
        pooled = wsum / cnt - u
        og = _dot(pooled.astype(BF16), pw_ref[g]) * ps_ref[:, lanes]
        asm_s[:, ML_WIDTH + g * POOL_GROUP:ML_WIDTH + (g + 1) * POOL_GROUP] = og.astype(BF16)
    hist = ext_s[tt:tt + POOL_HIST, :]
    ext_s[0:POOL_HIST, :] = hist

    y_ref[0] = x + _dot(asm_s[...], wout_ref[...])

    @pl.when(t == pl.num_programs(1) - 1)
    def _():
        c_out[0] = c_s[...]
        n_out[0] = n_s[0:ML_HEADS, :]
        m_out[0] = m_s[...]
        buf_out[0] = hist


def _mixer0_prompt(x, w, pos0):
    b, t, d = x.shape
    tt = min(256, t)
    chunk = min(128, tt)
    assert t % tt == 0 and tt % chunk == 0 and tt >= POOL_HIST
    kern = functools.partial(_mixer0_kernel, tt=tt, chunk=chunk, pos0=pos0)
    out_shape = (
        jax.ShapeDtypeStruct((b, t, d), F32),
        jax.ShapeDtypeStruct((b, ML_HEADS, ML_HEAD_DIM, ML_HEAD_DIM), F32),
        jax.ShapeDtypeStruct((b, ML_HEADS, ML_HEAD_DIM), F32),
        jax.ShapeDtypeStruct((b, 8, LANES), F32),
        jax.ShapeDtypeStruct((b, POOL_HIST, POOL_WIDTH), F32),
    )
    in_specs = [
        pl.BlockSpec((1, tt, d), lambda i, j: (i, j, 0)),
        _full((1, d)),
        _full((d, Z_MAIN)),
        _full((8, d)),
        _full((8, d)),
        _full((8, 1)),
        _full((8, 1)),
        _full((1, ML_WIDTH)),
        _full((len(POOL_WINDOWS), POOL_GROUP, POOL_GROUP)),
        _full((1, POOL_WIDTH)),
        _full((ML_WIDTH + POOL_WIDTH, d)),
    ]
    out_specs = (
        pl.BlockSpec((1, tt, d), lambda i, j: (i, j, 0)),
        pl.BlockSpec((1, ML_HEADS, ML_HEAD_DIM, ML_HEAD_DIM), lambda i, j: (i, 0, 0, 0)),
        pl.BlockSpec((1, ML_HEADS, ML_HEAD_DIM), lambda i, j: (i, 0, 0)),
        pl.BlockSpec((1, 8, LANES), lambda i, j: (i, 0, 0)),
        pl.BlockSpec((1, POOL_HIST, POOL_WIDTH), lambda i, j: (i, 0, 0)),
    )
    scratch = [
        pltpu.VMEM((tt, Z_MAIN), F32),
        pltpu.VMEM((tt, ML_WIDTH + POOL_WIDTH), BF16),
        pltpu.VMEM((tt + POOL_HIST, POOL_WIDTH), F32),
        pltpu.VMEM((ML_HEADS, ML_HEAD_DIM, ML_HEAD_DIM), F32),
        pltpu.VMEM((8, ML_HEAD_DIM), F32),
        pltpu.VMEM((8, LANES), F32),
    ]
    y, c, n, m, buf = pl.pallas_call(
        kern, grid=(b, t // tt), in_specs=in_specs, out_specs=out_specs, out_shape=out_shape,
        scratch_shapes=scratch, compiler_params=_params("arbitrary", "arbitrary"), name="mixer0_prompt",
    )(x, w["g_mix0"], w["w_in0_main"], w["w_gate_i_t"], w["w_gate_f_t"], w["b_i_col"], w["b_f_col"],
      w["mlstm_norm"], w["pool_w"], w["pool_scale"], w["w_out0"])
    return y, c, n, m[:, :ML_HEADS, 0], buf[:, 1:, :]


def _norm_matmul_kernel(x_ref, g_ref, w_ref, o_ref):
    o_ref[...] = _dot(_rms(x_ref[...], g_ref[...]).astype(BF16), w_ref[...])


def _norm_matmul(x, g, w_bf16):
    m, d = x.shape
    n = w_bf16.shape[1]
    tm = min(128, m)
    assert m % tm == 0
    return pl.pallas_call(
        _norm_matmul_kernel, grid=(m // tm,),
        in_specs=[pl.BlockSpec((tm, d), lambda i: (i, 0)), _full((1, d)), _full((d, n))],
        out_specs=pl.BlockSpec((tm, n), lambda i: (i, 0)),
        out_shape=jax.ShapeDtypeStruct((m, n), F32),
        compiler_params=_params("arbitrary"), name="norm_matmul",
    )(x, g, w_bf16)


def _mixer0_step_kernel(z_ref, x_ref, c0_ref, n0_ref, m0_ref, buf_ref, bi_ref, bf_ref, gn_ref, pw_ref,
                        ps_ref, wout_ref, y_ref, c_ref, n_ref, m_ref, bufo_ref, cq_s, asm_s, *, bb, cnts):
    dh = ML_HEAD_DIM
    gate_i = Z_MAIN
    gate_f = Z_MAIN + LANES
    ig = z_ref[:, gate_i:gate_i + ML_HEADS] + bi_ref[...]
    lf = _log_sigmoid(z_ref[:, gate_f:gate_f + ML_HEADS] + bf_ref[...])
    m0 = m0_ref[...]
    m_t = jnp.maximum(lf + m0, ig)
    w_i = jnp.exp(ig - m_t)
    w_f = jnp.exp(lf + m0 - m_t)
    e_m = jnp.exp(-m_t)
    m_ref[...] = m_t

    row8 = lax.broadcasted_iota(jnp.int32, (8, dh), 0) == 0
    for h in range(ML_HEADS):
        q = z_ref[:, h * dh:(h + 1) * dh]
        k = z_ref[:, ML_WIDTH + h * dh:ML_WIDTH + (h + 1) * dh] * (dh ** -0.5)
        v = z_ref[:, 2 * ML_WIDTH + h * dh:2 * ML_WIDTH + (h + 1) * dh]
        wi = w_i[:, h:h + 1]
        wf = w_f[:, h:h + 1]
        kw = k * wi
        for b in range(bb):
            c_mat = c0_ref[b, h]
            q8 = jnp.broadcast_to(q[b:b + 1, :], (8, dh))
            cq_s[b:b + 1, :] = _dot_nt(q8, c_mat)[0:1, :]
            v8 = jnp.where(row8, v[b:b + 1, :], 0.0)
            k8 = jnp.where(row8, kw[b:b + 1, :], 0.0)
            c_ref[b, h] = wf[b:b + 1, :] * c_mat + _dot_tn(v8, k8)
        n_old = n0_ref[:, h, :]
        s = jnp.sum(q * k, axis=1, keepdims=True) * wi
        num = s * v + wf * cq_s[...]
        den = s + wf * jnp.sum(n_old * q, axis=1, keepdims=True)
        hh = num / jnp.maximum(jnp.abs(den), e_m[:, h:h + 1])
        n_ref[:, h, :] = wf * n_old + kw
        o_gate = jax.nn.sigmoid(z_ref[:, 3 * ML_WIDTH + h * dh:3 * ML_WIDTH + (h + 1) * dh])
        hm = _head_norm(hh, gn_ref[:, h * dh:(h + 1) * dh]) * o_gate
        asm_s[:, h * dh:(h + 1) * dh] = hm

    u_all = z_ref[:, 4 * ML_WIDTH:Z_MAIN]
    for g, win in enumerate(POOL_WINDOWS):
        u = u_all[:, g * POOL_GROUP:(g + 1) * POOL_GROUP]
        wsum = u
        for j in range(1, win):
            off = (POOL_BUF - j) * POOL_WIDTH + g * POOL_GROUP
            wsum = wsum + buf_ref[:, off:off + POOL_GROUP]
        pooled = wsum / cnts[g] - u
        og = _dot(pooled.astype(BF16), pw_ref[g]) * ps_ref[:, g * POOL_GROUP:(g + 1) * POOL_GROUP]
        asm_s[:, ML_WIDTH + g * POOL_GROUP:ML_WIDTH + (g + 1) * POOL_GROUP] = og
    keep = (POOL_BUF - 1) * POOL_WIDTH
    bufo_ref[:, 0:keep] = buf_ref[:, POOL_WIDTH:POOL_WIDTH + keep]
    bufo_ref[:, keep:keep + POOL_WIDTH] = u_all

    y_ref[...] = x_ref[...] + _dot(asm_s[...].astype(BF16), wout_ref[...])


def _mixer0_sample(x, c0, n0, m0, buf0, w, pos0):
    b, d = x.shape
    z = _norm_matmul(x, w["g_mix0"], w["w_in0_gates"])
    zw = z.shape[1]
    bb = 8
    assert b % bb == 0
    cnts = tuple(float(min(win, pos0 + 1)) for win in POOL_WINDOWS)
    bufw = POOL_BUF * POOL_WIDTH
    kern = functools.partial(_mixer0_step_kernel, bb=bb, cnts=cnts)
    hd = (ML_HEADS, ML_HEAD_DIM)
    in_specs = [
        pl.BlockSpec((bb, zw), lambda i: (i, 0)),
        pl.BlockSpec((bb, d), lambda i: (i, 0)),
        pl.BlockSpec((bb,) + hd + (ML_HEAD_DIM,), lambda i: (i, 0, 0, 0)),
        pl.BlockSpec((bb,) + hd, lambda i: (i, 0, 0)),
        pl.BlockSpec((bb, ML_HEADS), lambda i: (i, 0)),
        pl.BlockSpec((bb, bufw), lambda i: (i, 0)),
        _full((1, ML_HEADS)),
        _full((1, ML_HEADS)),
        _full((1, ML_WIDTH)),
        _full((len(POOL_WINDOWS), POOL_GROUP, POOL_GROUP)),
        _full((1, POOL_WIDTH)),
        _full((ML_WIDTH + POOL_WIDTH, d)),
    ]
    out_specs = (
        pl.BlockSpec((bb, d), lambda i: (i, 0)),
        pl.BlockSpec((bb,) + hd + (ML_HEAD_DIM,), lambda i: (i, 0, 0, 0)),
        pl.BlockSpec((bb,) + hd, lambda i: (i, 0, 0)),
        pl.BlockSpec((bb, ML_HEADS), lambda i: (i, 0)),
        pl.BlockSpec((bb, bufw), lambda i: (i, 0)),
    )
    out_shape = (
        jax.ShapeDtypeStruct((b, d), F32),
        jax.ShapeDtypeStruct((b,) + hd + (ML_HEAD_DIM,), F32),
        jax.ShapeDtypeStruct((b,) + hd, F32),
        jax.ShapeDtypeStruct((b, ML_HEADS), F32),
        jax.ShapeDtypeStruct((b, bufw), F32),
    )
    scratch = [pltpu.VMEM((bb, ML_HEAD_DIM), F32), pltpu.VMEM((bb, ML_WIDTH + POOL_WIDTH), F32)]
    y, c, n, m, buf = pl.pallas_call(
        kern, grid=(b // bb,), in_specs=in_specs, out_specs=out_specs, out_shape=out_shape,
        scratch_shapes=scratch, compiler_params=_params("arbitrary"), name="mixer0_sample",
    )(z, x, c0, n0, m0, buf0.reshape(b, bufw), w["b_i_row"], w["b_f_row"], w["mlstm_norm"],
      w["pool_w"], w["pool_scale"], w["w_out0"])
    return y, c, n, m, buf.reshape(b, POOL_BUF, POOL_WIDTH)


def _ffn_kernel(x_ref, g_ref, wu_ref, wd_ref, gf_ref, y_ref, hn_s, acc_s, *, final_norm):
    f = pl.program_id(1)

    @pl.when(f == 0)
    def _():
        hn_s[...] = _rms(x_ref[...], g_ref[...]).astype(BF16)
        acc_s[...] = jnp.zeros_like(acc_s)

    a = jnp.maximum(_dot(hn_s[...], wu_ref[...]), 0.0)
    acc_s[...] += _dot((a * a).astype(BF16), wd_ref[...])

    @pl.when(f == pl.num_programs(1) - 1)
    def _():
        y = x_ref[...] + acc_s[...]
        if final_norm:
            y = _rms(y, gf_ref[...])
        y_ref[...] = y


def _ffn(x, g, w_up, w_down, g_final, final_norm):
    m, d = x.shape
    dff = w_up.shape[1]
    tm = min(512, m)
    tf = min(1024, dff)
    assert m % tm == 0 and dff % tf == 0
    kern = functools.partial(_ffn_kernel, final_norm=final_norm)
    return pl.pallas_call(
        kern, grid=(m // tm, dff // tf),
        in_specs=[
            pl.BlockSpec((tm, d), lambda i, j: (i, 0)),
            _full((1, d)),
            pl.BlockSpec((d, tf), lambda i, j: (0, j)),
            pl.BlockSpec((tf, d), lambda i, j: (j, 0)),
            _full((1, d)),
        ],
        out_specs=pl.BlockSpec((tm, d), lambda i, j: (i, 0)),
        out_shape=jax.ShapeDtypeStruct((m, d), F32),
        scratch_shapes=[pltpu.VMEM((tm, d), BF16), pltpu.VMEM((tm, d), F32)],
        compiler_params=_params("arbitrary", "arbitrary"), name="ffn",
    )(x, g, w_up, w_down, g_final)


def _mla_proj_kernel(x_ref, g_ref, win_ref, gq_ref, gkv_ref, wqn_ref, wqp_ref, wqs_ref, wuk_ref,
                     cos_ref, sin_ref, ckv_ref, kpe_ref, key_ref, q_ref):
    r = MLA_KV_RANK
    hn = _rms(x_ref[0], g_ref[...]).astype(BF16)
    z = _dot(hn, win_ref[...])
    cq = _rms(z[:, 0:MLA_Q_RANK], gq_ref[...]).astype(BF16)
    ckv = _rms(z[:, MLA_Q_RANK:MLA_Q_RANK + r], gkv_ref[...])
    cos = cos_ref[...]
    sin = sin_ref[...]
    kpe = z[:, MLA_Q_RANK + r:MLA_Q_RANK + r + LANES] * cos + z[:, MLA_Q_RANK + r + LANES:] * sin
    ckv_ref[0] = ckv
    kpe_ref[0] = kpe[:, 0:MLA_ROPE]
    key_ref[0, :, 0:r] = ckv.astype(BF16)
    key_ref[0, :, r:r + LANES] = kpe.astype(BF16)
    qn = _dot(cq, wqn_ref[...])
    qp = _dot(cq, wqp_ref[...])
    qs = _dot(cq, wqs_ref[...])
    for h in range(MLA_HEADS):
        lanes = slice(h * LANES, (h + 1) * LANES)
        q_ref[0, h, :, 0:r] = _dot(qn[:, lanes].astype(BF16), wuk_ref[h]).astype(BF16)
        q_ref[0, h, :, r:r + LANES] = (qp[:, lanes] * cos + qs[:, lanes] * sin).astype(BF16)


def _mla_proj(x, w, cos, sin):
    nb, t, d = x.shape
    tm = min(256, t)
    assert t % tm == 0
    kw = MLA_KEY_WIDTH
    out_shape = (
        jax.ShapeDtypeStruct((nb, t, MLA_KV_RANK), F32),
        jax.ShapeDtypeStruct((nb, t, MLA_ROPE), F32),
        jax.ShapeDtypeStruct((nb, t, kw), BF16),
        jax.ShapeDtypeStruct((nb, MLA_HEADS, t, kw), BF16),
    )
    in_specs = [
        pl.BlockSpec((1, tm, d), lambda i, j: (i, j, 0)),
        _full((1, d)),
        _full(w["w_in1"].shape),
        _full((1, MLA_Q_RANK)),
        _full((1, MLA_KV_RANK)),
        _full(w["w_q_nope"].shape),
        _full(w["w_q_pe"].shape),
        _full(w["w_q_pe_sw"].shape),
        _full(w["w_uk_t"].shape),
        pl.BlockSpec((tm, LANES), lambda i, j: (j, 0)),
        pl.BlockSpec((tm, LANES), lambda i, j: (j, 0)),
    ]
    out_specs = (
        pl.BlockSpec((1, tm, MLA_KV_RANK), lambda i, j: (i, j, 0)),
        pl.BlockSpec((1, tm, MLA_ROPE), lambda i, j: (i, j, 0)),
        pl.BlockSpec((1, tm, kw), lambda i, j: (i, j, 0)),
        pl.BlockSpec((1, MLA_HEADS, tm, kw), lambda i, j: (i, 0, j, 0)),
    )
    return pl.pallas_call(
        _mla_proj_kernel, grid=(nb, t // tm), in_specs=in_specs, out_specs=out_specs, out_shape=out_shape,
        compiler_params=_params("arbitrary", "arbitrary"), name="mla_proj",
    )(x, w["g_mix1"], w["w_in1"], w["mla_q_norm"], w["mla_kv_norm"], w["w_q_nope"], w["w_q_pe"],
      w["w_q_pe_sw"], w["w_uk_t"], cos, sin)


def _attn_out(o, x, wuv_ref, wo_ref, asm_s, tq):
    for h in range(MLA_HEADS):
        oh = o[h * tq:(h + 1) * tq, :].astype(BF16)
        asm_s[:, h * MLA_V:(h + 1) * MLA_V] = _dot(oh, wuv_ref[h]).astype(BF16)
    return x + _dot(asm_s[...], wo_ref[...])


def _flash_kernel(q_ref, k_ref, x_ref, wuv_ref, wo_ref, y_ref, acc_s, m_s, l_s, asm_s, *, tq, tk):
    qi = pl.program_id(1)
    rows = MLA_HEADS * tq
    q = q_ref[0].reshape(rows, MLA_KEY_WIDTH)
    m_s[...] = jnp.full_like(m_s, -jnp.inf)
    l_s[...] = jnp.zeros_like(l_s)
    acc_s[...] = jnp.zeros_like(acc_s)
    tok = qi * tq + (lax.broadcasted_iota(jnp.int32, (rows, tk), 0) & (tq - 1))
    col = lax.broadcasted_iota(jnp.int32, (rows, tk), 1)

    def body(kb, carry):
        start = pl.multiple_of(kb * tk, tk)
        kblk = k_ref[0, pl.ds(start, tk), :]
        s = _dot_nt(q, kblk) * MLA_SCALE
        s = jnp.where(col + kb * tk <= tok, s, -jnp.inf)
        m_old = m_s[...]
        m_new = jnp.maximum(m_old, jnp.max(s, axis=1, keepdims=True))
        alpha = jnp.exp(m_old - m_new)
        p = jnp.exp(s - m_new)
        l_s[...] = alpha * l_s[...] + jnp.sum(p, axis=1, keepdims=True)
        acc_s[...] = alpha * acc_s[...] + _dot(p.astype(BF16), kblk[:, 0:MLA_KV_RANK])
        m_s[...] = m_new
        return carry

    lax.fori_loop(0, qi + 1, body, 0)
    o = acc_s[...] / l_s[...]
    y_ref[0] = _attn_out(o, x_ref[0], wuv_ref, wo_ref, asm_s, tq)


def _mla_attend_prompt(q, keys, x, w):
    b, t, d = x.shape
    tq = min(256, t)
    tk = tq
    assert t % tq == 0 and tq & (tq - 1) == 0
    rows = MLA_HEADS * tq
    kern = functools.partial(_flash_kernel, tq=tq, tk=tk)
    return pl.pallas_call(
        kern, grid=(b, t // tq),
        in_specs=[
            pl.BlockSpec((1, MLA_HEADS, tq, MLA_KEY_WIDTH), lambda i, j: (i, 0, j, 0)),
            pl.BlockSpec((1, t, MLA_KEY_WIDTH), lambda i, j: (i, 0, 0)),
            pl.BlockSpec((1, tq, d), lambda i, j: (i, j, 0)),
            _full(w["w_uv_t"].shape),
            _full(w["w_out1"].shape),
        ],
        out_specs=pl.BlockSpec((1, tq, d), lambda i, j: (i, j, 0)),
        out_shape=jax.ShapeDtypeStruct((b, t, d), F32),
        scratch_shapes=[
            pltpu.VMEM((rows, MLA_KV_RANK), F32),
            pltpu.VMEM((rows, 1), F32),
            pltpu.VMEM((rows, 1), F32),
            pltpu.VMEM((tq, MLA_HEADS * MLA_V), BF16),
        ],
        compiler_params=_params("arbitrary", "arbitrary"), name="mla_flash",
    )(q, keys, x, w["w_uv_t"], w["w_out1"])


def _decode_kernel(pt_ref, q_ref, kn_ref, *refs, pages):
    lat_refs = refs[:pages]
    rope_refs = refs[pages:2 * pages]
    o_ref, m_s, l_s, acc_s = refs[2 * pages:]
    r = MLA_KV_RANK
    j = pl.program_id(1)
    q = q_ref[0]
    ql = q[:, 0:r]
    qp = q[:, r:r + MLA_ROPE]

    @pl.when(j == 0)
    def _():
        kn = kn_ref[0].astype(F32)
        m_s[...] = jnp.sum(q.astype(F32) * kn, axis=1, keepdims=True) * MLA_SCALE
        l_s[...] = jnp.ones_like(l_s)
        acc_s[...] = jnp.broadcast_to(kn[:, 0:r], acc_s.shape)

    lats = [lat_refs[p][0].astype(BF16) for p in range(pages)]
    ss = [_dot_nt(ql, lats[p]) + _dot_nt(qp, rope_refs[p][0].astype(BF16)) for p in range(pages)]
    s = jnp.concatenate(ss, axis=1) * MLA_SCALE
    m_old = m_s[...]
    m_new = jnp.maximum(m_old, jnp.max(s, axis=1, keepdims=True))
    alpha = jnp.exp(m_old - m_new)
    p_all = jnp.exp(s - m_new)
    l_s[...] = alpha * l_s[...] + jnp.sum(p_all, axis=1, keepdims=True)
    acc = alpha * acc_s[...]
    ps = p_all.shape[1] // pages
    for p in range(pages):
        acc = acc + _dot(p_all[:, p * ps:(p + 1) * ps].astype(BF16), lats[p])
    acc_s[...] = acc
    m_s[...] = m_new

    @pl.when(j == pl.num_programs(1) - 1)
    def _():
        o_ref[0] = acc_s[...] / l_s[...]


def _mla_attend_sample(q, key_new, cache_latent, cache_rope_k, page_table):
    b = q.shape[0]
    n_pages = page_table.shape[1]
    page = cache_latent.shape[1]
    pages = math.gcd(n_pages, 8)
    kw = MLA_KEY_WIDTH

    def lat_map(p):
        return lambda i, j, pt: (pt[i, j * pages + p], 0, 0)

    in_specs = [
        pl.BlockSpec((1, MLA_HEADS, kw), lambda i, j, pt: (i, 0, 0)),
        pl.BlockSpec((1, 1, kw), lambda i, j, pt: (i, 0, 0)),
    ]
    in_specs += [pl.BlockSpec((1, page, MLA_KV_RANK), lat_map(p)) for p in range(pages)]
    in_specs += [pl.BlockSpec((1, page, MLA_ROPE), lat_map(p)) for p in range(pages)]
    grid_spec = pltpu.PrefetchScalarGridSpec(
        num_scalar_prefetch=1, grid=(b, n_pages // pages), in_specs=in_specs,
        out_specs=pl.BlockSpec((1, MLA_HEADS, MLA_KV_RANK), lambda i, j, pt: (i, 0, 0)),
        scratch_shapes=[
            pltpu.VMEM((MLA_HEADS, 1), F32),
            pltpu.VMEM((MLA_HEADS, 1), F32),
            pltpu.VMEM((MLA_HEADS, MLA_KV_RANK), F32),
        ],
    )
    kern = functools.partial(_decode_kernel, pages=pages)
    return pl.pallas_call(
        kern, grid_spec=grid_spec, out_shape=jax.ShapeDtypeStruct((b, MLA_HEADS, MLA_KV_RANK), F32),
        compiler_params=_params("arbitrary", "arbitrary"), name="mla_decode",
    )(page_table, q, key_new, *([cache_latent] * pages), *([cache_rope_k] * pages))


def _attn_out_kernel(o_ref, x_ref, wuv_ref, wo_ref, y_ref, asm_s):
    for h in range(MLA_HEADS):
        oh = o_ref[:, h * MLA_KV_RANK:(h + 1) * MLA_KV_RANK].astype(BF16)
        asm_s[:, h * MLA_V:(h + 1) * MLA_V] = _dot(oh, wuv_ref[h]).astype(BF16)
    y_ref[...] = x_ref[...] + _dot(asm_s[...], wo_ref[...])


def _attn_out_sample(o, x, w):
    m, d = x.shape
    tm = min(128, m)
    assert m % tm == 0
    ow = MLA_HEADS * MLA_KV_RANK
    return pl.pallas_call(
        _attn_out_kernel, grid=(m // tm,),
        in_specs=[pl.BlockSpec((tm, ow), lambda i: (i, 0)), pl.BlockSpec((tm, d), lambda i: (i, 0)),
                  _full(w["w_uv_t"].shape), _full(w["w_out1"].shape)],
        out_specs=pl.BlockSpec((tm, d), lambda i: (i, 0)),
        out_shape=jax.ShapeDtypeStruct((m, d), F32),
        scratch_shapes=[pltpu.VMEM((tm, MLA_HEADS * MLA_V), BF16)],
        compiler_params=_params("arbitrary"), name="attn_out",
    )(o.reshape(m, ow), x, w["w_uv_t"], w["w_out1"])


def _rope_tables(pos):
    half = MLA_ROPE // 2
    inv = ROPE_THETA ** (-jnp.arange(half, dtype=F32) * 2.0 / MLA_ROPE)
    ang = pos.astype(F32)[:, None] * inv[None, :]
    cos = jnp.cos(ang)
    sin = jnp.sin(ang)
    pad = jnp.zeros((pos.shape[0], LANES - MLA_ROPE), F32)
    return jnp.concatenate([cos, cos, pad], axis=1), jnp.concatenate([-sin, sin, pad], axis=1)


def _pad_lanes(a, width):
    return jnp.pad(a, ((0, 0), (0, width - a.shape[1])))


def _prepare_weights(norm_mix, norm_ffn, norm_final, w_in0, mlstm_b_i, mlstm_b_f, mlstm_norm, pool_w,
                     pool_scale, w_out0, w_in1, mla_q_norm, mla_kv_norm, w_q_up, w_uk, w_uv, w_out1,
                     w_up, w_down):
    d = w_in0.shape[0]
    half = MLA_ROPE // 2
    gate_i = w_in0[:, Z_MAIN:Z_MAIN + ML_HEADS]
    gate_f = w_in0[:, Z_MAIN + ML_HEADS:Z_MAIN + 2 * ML_HEADS]
    rope_k = w_in1[:, MLA_Q_RANK + MLA_KV_RANK:]
    rope_k_sw = jnp.concatenate([rope_k[:, half:], rope_k[:, :half]], axis=1)
    q_up = w_q_up.reshape(MLA_Q_RANK, MLA_HEADS, MLA_NOPE + MLA_ROPE)
    q_pe = q_up[:, :, MLA_NOPE:]
    q_pe_sw = jnp.concatenate([q_pe[:, :, half:], q_pe[:, :, :half]], axis=2)
    pad_pe = ((0, 0), (0, 0), (0, LANES - MLA_ROPE))
    col8 = lambda v: jnp.pad(v.astype(F32), (0, 8 - ML_HEADS)).reshape(8, 1)
    return {
        "g_mix0": norm_mix[0].reshape(1, d), "g_mix1": norm_mix[1].reshape(1, d),
        "g_ffn0": norm_ffn[0].reshape(1, d), "g_ffn1": norm_ffn[1].reshape(1, d),
        "g_final": norm_final.reshape(1, d),
        "w_in0_main": w_in0[:, :Z_MAIN].astype(BF16),
        "w_in0_gates": jnp.concatenate(
            [w_in0[:, :Z_MAIN], _pad_lanes(gate_i, LANES), _pad_lanes(gate_f, LANES)], axis=1).astype(BF16),
        "w_gate_i_t": jnp.pad(gate_i.T, ((0, 8 - ML_HEADS), (0, 0))).astype(BF16),
        "w_gate_f_t": jnp.pad(gate_f.T, ((0, 8 - ML_HEADS), (0, 0))).astype(BF16),
        "b_i_col": col8(mlstm_b_i), "b_f_col": col8(mlstm_b_f),
        "b_i_row": mlstm_b_i.reshape(1, ML_HEADS), "b_f_row": mlstm_b_f.reshape(1, ML_HEADS),
        "mlstm_norm": mlstm_norm.reshape(1, ML_WIDTH),
        "pool_w": pool_w.astype(BF16), "pool_scale": pool_scale.reshape(1, POOL_WIDTH),
        "w_out0": w_out0.astype(BF16),
        "w_in1": jnp.concatenate(
            [w_in1[:, :MLA_Q_RANK + MLA_KV_RANK], _pad_lanes(rope_k, LANES), _pad_lanes(rope_k_sw, LANES)],
            axis=1).astype(BF16),
        "mla_q_norm": mla_q_norm.reshape(1, MLA_Q_RANK), "mla_kv_norm": mla_kv_norm.reshape(1, MLA_KV_RANK),
        "w_q_nope": q_up[:, :, :MLA_NOPE].reshape(MLA_Q_RANK, MLA_HEADS * MLA_NOPE).astype(BF16),
        "w_q_pe": jnp.pad(q_pe, pad_pe).reshape(MLA_Q_RANK, MLA_HEADS * LANES).astype(BF16),
        "w_q_pe_sw": jnp.pad(q_pe_sw, pad_pe).reshape(MLA_Q_RANK, MLA_HEADS * LANES).astype(BF16),
        "w_uk_t": jnp.transpose(w_uk, (1, 2, 0)).astype(BF16),
        "w_uv_t": jnp.transpose(w_uv, (1, 0, 2)).astype(BF16),
        "w_out1": w_out1.astype(BF16),
        "w_up0": w_up[0].astype(BF16), "w_up1": w_up[1].astype(BF16),
        "w_down0": w_down[0].astype(BF16), "w_down1": w_down[1].astype(BF16),
    }


def kernel(x_prompt, x_sample, state_mlstm_C, state_mlstm_n, state_mlstm_m, state_pool, cache_latent, cache_rope_k, page_table, norm_mix, norm_ffn, norm_final, w_in0, mlstm_b_i, mlstm_b_f, mlstm_norm, pool_w, pool_scale, w_out0, w_in1, mla_q_norm, mla_kv_norm, w_q_up, w_uk, w_uv, w_out1, w_up, w_down):
    w = _prepare_weights(norm_mix, norm_ffn, norm_final, w_in0, mlstm_b_i, mlstm_b_f, mlstm_norm, pool_w,
                         pool_scale, w_out0, w_in1, mla_q_norm, mla_kv_norm, w_q_up, w_uk, w_uv, w_out1,
                         w_up, w_down)
    bp, t, d = x_prompt.shape
    bs, ts, _ = x_sample.shape
    assert ts == 1
    past_len = page_table.shape[1] * cache_latent.shape[1]

    x1, c_p, n_p, m_p, buf_p = _mixer0_prompt(x_prompt, w, 0)
    x2 = _ffn(x1.reshape(bp * t, d), w["g_ffn0"], w["w_up0"], w["w_down0"], w["g_final"], False)
    x2 = x2.reshape(bp, t, d)
    cos_p, sin_p = _rope_tables(jnp.arange(t))
    ckv_p, kpe_p, keys_p, q_p = _mla_proj(x2, w, cos_p, sin_p)
    x3 = _mla_attend_prompt(q_p, keys_p, x2, w)
    y_p = _ffn(x3.reshape(bp * t, d), w["g_ffn1"], w["w_up1"], w["w_down1"], w["g_final"], True)
    y_p = y_p.reshape(bp, t, d)

    xs = x_sample.reshape(bs, d)
    xs1, c_s, n_s, m_s, buf_s = _mixer0_sample(xs, state_mlstm_C, state_mlstm_n, state_mlstm_m, state_pool,
                                               w, past_len)
    xs2 = _ffn(xs1, w["g_ffn0"], w["w_up0"], w["w_down0"], w["g_final"], False)
    cos_s, sin_s = _rope_tables(jnp.full((bs,), past_len))
    ckv_s, kpe_s, keys_s, q_s = _mla_proj(xs2.reshape(1, bs, d), w, cos_s, sin_s)
    q_s = jnp.transpose(q_s[0], (1, 0, 2))
    o_s = _mla_attend_sample(q_s, keys_s.reshape(bs, 1, MLA_KEY_WIDTH), cache_latent, cache_rope_k, page_table)
    xs3 = _attn_out_sample(o_s, xs2, w)
    y_s = _ffn(xs3, w["g_ffn1"], w["w_up1"], w["w_down1"], w["g_final"], True)

    return (y_p, y_s.reshape(bs, 1, d), c_p, n_p, m_p, buf_p, ckv_p, kpe_p,
            c_s, n_s, m_s, buf_s, ckv_s.reshape(bs, 1, MLA_KV_RANK), kpe_s.reshape(bs, 1, MLA_ROPE))
```

```python
import functools
import math

import jax
import jax.numpy as jnp
from jax import lax
from jax.experimental import pallas as pl
from jax.experimental.pallas import tpu as pltpu

F32 = jnp.float32
BF16 = jnp.bfloat16

EPS = 1e-6
ML_HEADS = 4
ML_HEAD_DIM = 128
ML_WIDTH = ML_HEADS * ML_HEAD_DIM
POOL_WINDOWS = (2, 4, 8, 16)
POOL_GROUP = 128
POOL_WIDTH = len(POOL_WINDOWS) * POOL_GROUP
POOL_BUF = max(POOL_WINDOWS) - 1
POOL_HIST = POOL_BUF + 1
MLA_HEADS = 8
MLA_NOPE = 128
MLA_ROPE = 64
MLA_V = 128
MLA_Q_RANK = 512
MLA_KV_RANK = 256
MLA_SCALE = (MLA_NOPE + MLA_ROPE) ** -0.5
ROPE_THETA = 10000.0
LANES = 128
MLA_KEY_WIDTH = MLA_KV_RANK + LANES
Z_MAIN = 4 * ML_WIDTH + POOL_WIDTH
FLASH_GROUP_LANES = 4096
MLA_TOKEN_TILE = 512

VMEM_LIMIT_BYTES = 48 * 1024 * 1024

_NT = (((1,), (1,)), ((), ()))
_TN = (((0,), (0,)), ((), ()))


def _dot(a, b):
    return jnp.dot(a, b, preferred_element_type=F32)


def _dot_nt(a, b):
    return lax.dot_general(a, b, _NT, preferred_element_type=F32)


def _dot_tn(a, b):
    return lax.dot_general(a, b, _TN, preferred_element_type=F32)


def _rms(x, g):
    return x * lax.rsqrt(jnp.mean(x * x, axis=-1, keepdims=True) + EPS) * g


def _log_sigmoid(x):
    return jnp.minimum(x, 0.0) - jnp.log1p(jnp.exp(-jnp.abs(x)))


def _head_norm(h, g):
    hc = h - jnp.mean(h, axis=-1, keepdims=True)
    return hc * lax.rsqrt(jnp.mean(hc * hc, axis=-1, keepdims=True) + EPS) * g


def _cumsum_lanes(x):
    n = x.shape[-1]
    lane = lax.broadcasted_iota(jnp.int32, x.shape, x.ndim - 1)
    s = 1
    while s < n:
        x = x + jnp.where(lane >= s, pltpu.roll(x, s, axis=x.ndim - 1), 0.0)
        s *= 2
    return x


def _params(*semantics):
    return pltpu.CompilerParams(dimension_semantics=semantics, vmem_limit_bytes=VMEM_LIMIT_BYTES)


def _full(shape):
    return pl.BlockSpec(shape, lambda *_: (0,) * len(shape))


def _mixer0_kernel(x_ref, g_ref, win_ref, wgi_ref, wgf_ref, bi_ref, bf_ref, gn_ref, pw_ref, ps_ref,
                   wout_ref, y_ref, c_out, n_out, m_out, buf_out,
                   z_s, asm_s, ext_s, c_s, n_s, m_s, *, tt, chunk, pos0):
    t = pl.program_id(1)

    @pl.when(t == 0)
    def _():
        c_s[...] = jnp.zeros_like(c_s)
        n_s[...] = jnp.zeros_like(n_s)
        m_s[...] = jnp.zeros_like(m_s)
        ext_s[0:POOL_HIST, :] = jnp.zeros((POOL_HIST, POOL_WIDTH), F32)

    x = x_ref[0]
    hn = _rms(x, g_ref[...]).astype(BF16)
    z_s[...] = _dot(hn, win_ref[...])
    ig_all = _dot_nt(wgi_ref[...], hn) + bi_ref[...]
    lf_all = _log_sigmoid(_dot_nt(wgf_ref[...], hn) + bf_ref[...])

    dh = ML_HEAD_DIM
    row = lax.broadcasted_iota(jnp.int32, (chunk, chunk), 0)
    col = lax.broadcasted_iota(jnp.int32, (chunk, chunk), 1)
    causal = row >= col
    diag = row == col
    neg_inf = jnp.float32(-jnp.inf)

    for c in range(tt // chunk):
        rows = slice(c * chunk, (c + 1) * chunk)
        ig = ig_all[:, rows]
        bcum = _cumsum_lanes(lf_all[:, rows])
        a = ig - bcum
        m_prev8 = m_s[:, 0:1]
        m_last8 = jnp.maximum(m_prev8, jnp.max(a, axis=1, keepdims=True))
        w_old8 = jnp.exp(m_prev8 - m_last8)
        m_new8 = bcum[:, chunk - 1:chunk] + m_last8
        for h in range(ML_HEADS):
            a_row = a[h:h + 1, :]
            b_row = bcum[h:h + 1, :]
            m_prev = m_prev8[h:h + 1, :]
            a_mat = jnp.where(causal, a_row, neg_inf)
            m_col = jnp.maximum(jnp.max(a_mat, axis=1, keepdims=True), m_prev)
            w_intra = jnp.exp(a_mat - m_col)
            w_inter = jnp.exp(m_prev - m_col)
            b_col = jnp.sum(jnp.where(diag, b_row, 0.0), axis=1, keepdims=True)
            a_col = jnp.sum(jnp.where(diag, a_row, 0.0), axis=1, keepdims=True)
            m_tok = b_col + m_col

            q = z_s[rows, h * dh:(h + 1) * dh]
            k = z_s[rows, ML_WIDTH + h * dh:ML_WIDTH + (h + 1) * dh] * (dh ** -0.5)
            v = z_s[rows, 2 * ML_WIDTH + h * dh:2 * ML_WIDTH + (h + 1) * dh]
            qb = q.astype(BF16)
            vb = v.astype(BF16)
            c_mat = c_s[h]
            n_row = n_s[h:h + 1, :]
            s = _dot_nt(qb, k.astype(BF16)) * w_intra
            num = _dot(s.astype(BF16), vb) + w_inter * _dot_nt(qb, c_mat.astype(BF16))
            den = jnp.sum(s, axis=1, keepdims=True) + w_inter * jnp.sum(q * n_row, axis=1, keepdims=True)
            hh = num / jnp.maximum(jnp.abs(den), jnp.exp(-m_tok))

            w_tok = jnp.exp(a_col - m_last8[h:h + 1, :])
            w_old = w_old8[h:h + 1, :]
            kw = k * w_tok
            c_s[h] = w_old * c_mat + _dot_tn(vb, kw.astype(BF16))
            n_s[h:h + 1, :] = w_old * n_row + jnp.sum(kw, axis=0, keepdims=True)

            o_gate = jax.nn.sigmoid(z_s[rows, 3 * ML_WIDTH + h * dh:3 * ML_WIDTH + (h + 1) * dh])
            hm = _head_norm(hh, gn_ref[:, h * dh:(h + 1) * dh]) * o_gate
            asm_s[rows, h * dh:(h + 1) * dh] = hm.astype(BF16)
        m_s[...] = jnp.broadcast_to(m_new8, m_s.shape)

    ext_s[POOL_HIST:POOL_HIST + tt, :] = z_s[:, 4 * ML_WIDTH:Z_MAIN]
    pos = pos0 + t * tt + lax.broadcasted_iota(jnp.int32, (tt, 1), 0)
    for g, win in enumerate(POOL_WINDOWS):
        lanes = slice(g * POOL_GROUP, (g + 1) * POOL_GROUP)
        u = ext_s[POOL_HIST:POOL_HIST + tt, lanes]
        wsum = u
        for j in range(1, win):
            wsum = wsum + ext_s[POOL_HIST - j:POOL_HIST - j + tt, lanes]
        cnt = jnp.minimum(win, pos + 1).astype(F32)
        pooled = wsum / cnt - u
        og = _dot(pooled.astype(BF16), pw_ref[g]) * ps_ref[:, lanes]
        asm_s[:, ML_WIDTH + g * POOL_GROUP:ML_WIDTH + (g + 1) * POOL_GROUP] = og.astype(BF16)
    hist = ext_s[tt:tt + POOL_HIST, :]
    ext_s[0:POOL_HIST, :] = hist

    y_ref[0] = x + _dot(asm_s[...], wout_ref[...])

    @pl.when(t == pl.num_programs(1) - 1)
    def _():
        c_out[0] = c_s[...]
        n_out[0] = n_s[0:ML_HEADS, :]
        m_out[0] = m_s[...]
        buf_out[0] = hist


def _mixer0_prompt(x, w, pos0):
    b, t, d = x.shape
    tt = min(256, t)
    chunk = min(128, tt)
    assert t % tt == 0 and tt % chunk == 0 and tt >= POOL_HIST
    kern = functools.partial(_mixer0_kernel, tt=tt, chunk=chunk, pos0=pos0)
    out_shape = (
        jax.ShapeDtypeStruct((b, t, d), F32),
        jax.ShapeDtypeStruct((b, ML_HEADS, ML_HEAD_DIM, ML_HEAD_DIM), F32),
        jax.ShapeDtypeStruct((b, ML_HEADS, ML_HEAD_DIM), F32),
        jax.ShapeDtypeStruct((b, 8, LANES), F32),
        jax.ShapeDtypeStruct((b, POOL_HIST, POOL_WIDTH), F32),
    )
    in_specs = [
        pl.BlockSpec((1, tt, d), lambda i, j: (i, j, 0)),
        _full((1, d)),
        _full((d, Z_MAIN)),
        _full((8, d)),
        _full((8, d)),
        _full((8, 1)),
        _full((8, 1)),
        _full((1, ML_WIDTH)),
        _full((len(POOL_WINDOWS), POOL_GROUP, POOL_GROUP)),
        _full((1, POOL_WIDTH)),
        _full((ML_WIDTH + POOL_WIDTH, d)),
    ]
    out_specs = (
        pl.BlockSpec((1, tt, d), lambda i, j: (i, j, 0)),
        pl.BlockSpec((1, ML_HEADS, ML_HEAD_DIM, ML_HEAD_DIM), lambda i, j: (i, 0, 0, 0)),
        pl.BlockSpec((1, ML_HEADS, ML_HEAD_DIM), lambda i, j: (i, 0, 0)),
        pl.BlockSpec((1, 8, LANES), lambda i, j: (i, 0, 0)),
        pl.BlockSpec((1, POOL_HIST, POOL_WIDTH), lambda i, j: (i, 0, 0)),
    )
    scratch = [
        pltpu.VMEM((tt, Z_MAIN), F32),
        pltpu.VMEM((tt, ML_WIDTH + POOL_WIDTH), BF16),
        pltpu.VMEM((tt + POOL_HIST, POOL_WIDTH), F32),
        pltpu.VMEM((ML_HEADS, ML_HEAD_DIM, ML_HEAD_DIM), F32),
        pltpu.VMEM((8, ML_HEAD_DIM), F32),
        pltpu.VMEM((8, LANES), F32),
    ]
    y, c, n, m, buf = pl.pallas_call(
        kern, grid=(b, t // tt), in_specs=in_specs, out_specs=out_specs, out_shape=out_shape,
        scratch_shapes=scratch, compiler_params=_params("arbitrary", "arbitrary"), name="mixer0_prompt",
    )(x, w["g_mix0"], w["w_in0_main"], w["w_gate_i_t"], w["w_gate_f_t"], w["b_i_col"], w["b_f_col"],
      w["mlstm_norm"], w["pool_w"], w["pool_scale"], w["w_out0"])
    return y, c, n, m[:, :ML_HEADS, 0], buf[:, 1:, :]


def _norm_matmul_kernel(x_ref, g_ref, w_ref, o_ref):
    o_ref[...] = _dot(_rms(x_ref[...], g_ref[...]).astype(BF16), w_ref[...])


def _norm_matmul(x, g, w_bf16):
    m, d = x.shape
    n = w_bf16.shape[1]
    tm = min(128, m)
    assert m % tm == 0
    return pl.pallas_call(
        _norm_matmul_kernel, grid=(m // tm,),
        in_specs=[pl.BlockSpec((tm, d), lambda i: (i, 0)), _full((1, d)), _full((d, n))],
        out_specs=pl.BlockSpec((tm, n), lambda i: (i, 0)),
        out_shape=jax.ShapeDtypeStruct((m, n), F32),
        compiler_params=_params("arbitrary"), name="norm_matmul",
    )(x, g, w_bf16)


def _mixer0_step_kernel(z_ref, x_ref, c0_ref, n0_ref, m0_ref, buf_ref, bi_ref, bf_ref, gn_ref, pw_ref,
                        ps_ref, wout_ref, y_ref, c_ref, n_ref, m_ref, bufo_ref, cq_s, asm_s, *, bb, cnts):
    dh = ML_HEAD_DIM
    gate_i = Z_MAIN
    gate_f = Z_MAIN + LANES
    ig = z_ref[:, gate_i:gate_i + ML_HEADS] + bi_ref[...]
    lf = _log_sigmoid(z_ref[:, gate_f:gate_f + ML_HEADS] + bf_ref[...])
    m0 = m0_ref[...]
    m_t = jnp.maximum(lf + m0, ig)
    w_i = jnp.exp(ig - m_t)
    w_f = jnp.exp(lf + m0 - m_t)
    e_m = jnp.exp(-m_t)
    m_ref[...] = m_t

    row8 = lax.broadcasted_iota(jnp.int32, (8, dh), 0) == 0
    for h in range(ML_HEADS):
        q = z_ref[:, h * dh:(h + 1) * dh]
        k = z_ref[:, ML_WIDTH + h * dh:ML_WIDTH + (h + 1) * dh] * (dh ** -0.5)
        v = z_ref[:, 2 * ML_WIDTH + h * dh:2 * ML_WIDTH + (h + 1) * dh]
        wi = w_i[:, h:h + 1]
        wf = w_f[:, h:h + 1]
        kw = k * wi
        for b in range(bb):
            c_mat = c0_ref[b, h]
            q8 = jnp.broadcast_to(q[b:b + 1, :], (8, dh))
            cq_s[b:b + 1, :] = _dot_nt(q8, c_mat)[0:1, :]
            v8 = jnp.where(row8, v[b:b + 1, :], 0.0)
            k8 = jnp.where(row8, kw[b:b + 1, :], 0.0)
            c_ref[b, h] = wf[b:b + 1, :] * c_mat + _dot_tn(v8, k8)
        n_old = n0_ref[:, h, :]
        s = jnp.sum(q * k, axis=1, keepdims=True) * wi
        num = s * v + wf * cq_s[...]
        den = s + wf * jnp.sum(n_old * q, axis=1, keepdims=True)
        hh = num / jnp.maximum(jnp.abs(den), e_m[:, h:h + 1])
        n_ref[:, h, :] = wf * n_old + kw
        o_gate = jax.nn.sigmoid(z_ref[:, 3 * ML_WIDTH + h * dh:3 * ML_WIDTH + (h + 1) * dh])
        hm = _head_norm(hh, gn_ref[:, h * dh:(h + 1) * dh]) * o_gate
        asm_s[:, h * dh:(h + 1) * dh] = hm

    u_all = z_ref[:, 4 * ML_WIDTH:Z_MAIN]
    for g, win in enumerate(POOL_WINDOWS):
        u = u_all[:, g * POOL_GROUP:(g + 1) * POOL_GROUP]
        wsum = u
        for j in range(1, win):
            off = (POOL_BUF - j) * POOL_WIDTH + g * POOL_GROUP
            wsum = wsum + buf_ref[:, off:off + POOL_GROUP]
        pooled = wsum / cnts[g] - u
        og = _dot(pooled.astype(BF16), pw_ref[g]) * ps_ref[:, g * POOL_GROUP:(g + 1) * POOL_GROUP]
        asm_s[:, ML_WIDTH + g * POOL_GROUP:ML_WIDTH + (g + 1) * POOL_GROUP] = og
    keep = (POOL_BUF - 1) * POOL_WIDTH
    bufo_ref[:, 0:keep] = buf_ref[:, POOL_WIDTH:POOL_WIDTH + keep]
    bufo_ref[:, keep:keep + POOL_WIDTH] = u_all

    y_ref[...] = x_ref[...] + _dot(asm_s[...].astype(BF16), wout_ref[...])


def _mixer0_sample(x, c0, n0, m0, buf0, w, pos0):
    b, d = x.shape
    z = _norm_matmul(x, w["g_mix0"], w["w_in0_gates"])
    zw = z.shape[1]
    bb = 8
    assert b % bb == 0
    cnts = tuple(float(min(win, pos0 + 1)) for win in POOL_WINDOWS)
    bufw = POOL_BUF * POOL_WIDTH
    kern = functools.partial(_mixer0_step_kernel, bb=bb, cnts=cnts)
    hd = (ML_HEADS, ML_HEAD_DIM)
    in_specs = [
        pl.BlockSpec((bb, zw), lambda i: (i, 0)),
        pl.BlockSpec((bb, d), lambda i: (i, 0)),
        pl.BlockSpec((bb,) + hd + (ML_HEAD_DIM,), lambda i: (i, 0, 0, 0)),
        pl.BlockSpec((bb,) + hd, lambda i: (i, 0, 0)),
        pl.BlockSpec((bb, ML_HEADS), lambda i: (i, 0)),
        pl.BlockSpec((bb, bufw), lambda i: (i, 0)),
        _full((1, ML_HEADS)),
        _full((1, ML_HEADS)),
        _full((1, ML_WIDTH)),
        _full((len(POOL_WINDOWS), POOL_GROUP, POOL_GROUP)),
        _full((1, POOL_WIDTH)),
        _full((ML_WIDTH + POOL_WIDTH, d)),
    ]
    out_specs = (
        pl.BlockSpec((bb, d), lambda i: (i, 0)),
        pl.BlockSpec((bb,) + hd + (ML_HEAD_DIM,), lambda i: (i, 0, 0, 0)),
        pl.BlockSpec((bb,) + hd, lambda i: (i, 0, 0)),
        pl.BlockSpec((bb, ML_HEADS), lambda i: (i, 0)),
        pl.BlockSpec((bb, bufw), lambda i: (i, 0)),
    )
    out_shape = (
        jax.ShapeDtypeStruct((b, d), F32),
        jax.ShapeDtypeStruct((b,) + hd + (ML_HEAD_DIM,), F32),
        jax.ShapeDtypeStruct((b,) + hd, F32),
        jax.ShapeDtypeStruct((b, ML_HEADS), F32),
        jax.ShapeDtypeStruct((b, bufw), F32),
    )
    scratch = [pltpu.VMEM((bb, ML_HEAD_DIM), F32), pltpu.VMEM((bb, ML_WIDTH + POOL_WIDTH), F32)]
    y, c, n, m, buf = pl.pallas_call(
        kern, grid=(b // bb,), in_specs=in_specs, out_specs=out_specs, out_shape=out_shape,
        scratch_shapes=scratch, compiler_params=_params("arbitrary"), name="mixer0_sample",
    )(z, x, c0, n0, m0, buf0.reshape(b, bufw), w["b_i_row"], w["b_f_row"], w["mlstm_norm"],
      w["pool_w"], w["pool_scale"], w["w_out0"])
    return y, c, n, m, buf.reshape(b, POOL_BUF, POOL_WIDTH)


def _ffn_kernel(x_ref, g_ref, wu_ref, wd_ref, gf_ref, y_ref, hn_s, acc_s, *, final_norm):
    f = pl.program_id(1)

    @pl.when(f == 0)
    def _():
        hn_s[...] = _rms(x_ref[...], g_ref[...]).astype(BF16)
        acc_s[...] = jnp.zeros_like(acc_s)

    a = jnp.maximum(_dot(hn_s[...], wu_ref[...]), 0.0)
    acc_s[...] += _dot((a * a).astype(BF16), wd_ref[...])

    @pl.when(f == pl.num_programs(1) - 1)
    def _():
        y = x_ref[...] + acc_s[...]
        if final_norm:
            y = _rms(y, gf_ref[...])
        y_ref[...] = y


def _ffn(x, g, w_up, w_down, g_final, final_norm):
    m, d = x.shape
    dff = w_up.shape[1]
    tm = min(512, m)
    tf = min(1024, dff)
    assert m % tm == 0 and dff % tf == 0
    kern = functools.partial(_ffn_kernel, final_norm=final_norm)
    return pl.pallas_call(
        kern, grid=(m // tm, dff // tf),
        in_specs=[
            pl.BlockSpec((tm, d), lambda i, j: (i, 0)),
            _full((1, d)),
            pl.BlockSpec((d, tf), lambda i, j: (0, j)),
            pl.BlockSpec((tf, d), lambda i, j: (j, 0)),
            _full((1, d)),
        ],
        out_specs=pl.BlockSpec((tm, d), lambda i, j: (i, 0)),
        out_shape=jax.ShapeDtypeStruct((m, d), F32),
        scratch_shapes=[pltpu.VMEM((tm, d), BF16), pltpu.VMEM((tm, d), F32)],
        compiler_params=_params("arbitrary", "arbitrary"), name="ffn",
    )(x, g, w_up, w_down, g_final)


def _mla_proj_kernel(x_ref, g_ref, win_ref, gq_ref, gkv_ref, wqn_ref, wqp_ref, wqs_ref, wuk_ref,
                     cos_ref, sin_ref, ckv_ref, kpe_ref, key_ref, vt_ref, qt_ref):
    r = MLA_KV_RANK
    tm = x_ref.shape[1]
    hn = _rms(x_ref[0], g_ref[...]).astype(BF16)
    z = _dot(hn, win_ref[...])
    cq = _rms(z[:, 0:MLA_Q_RANK], gq_ref[...]).astype(BF16)
    ckv = _rms(z[:, MLA_Q_RANK:MLA_Q_RANK + r], gkv_ref[...])
    cos = cos_ref[...]
    sin = sin_ref[...]
    kpe = z[:, MLA_Q_RANK + r:MLA_Q_RANK + r + LANES] * cos + z[:, MLA_Q_RANK + r + LANES:] * sin
    ckv_ref[0] = ckv
    kpe_ref[0] = kpe[:, 0:MLA_ROPE]
    key_ref[0, :, 0:r] = ckv.astype(BF16)
    key_ref[0, :, r:r + LANES] = kpe.astype(BF16)
    vt_ref[0, 0] = ckv.T.astype(BF16)
    qn = _dot(cq, wqn_ref[...])
    qp = _dot(cq, wqp_ref[...])
    qs = _dot(cq, wqs_ref[...])
    for h in range(MLA_HEADS):
        lanes = slice(h * LANES, (h + 1) * LANES)
        cols = slice(h * tm, (h + 1) * tm)
        q_lat_t = _dot_nt(wuk_ref[h], qn[:, lanes].astype(BF16))
        q_pe = qp[:, lanes] * cos + qs[:, lanes] * sin
        qt_ref[0, 0, 0:r, cols] = q_lat_t.astype(BF16)
        qt_ref[0, 0, r:r + LANES, cols] = q_pe.T.astype(BF16)


def _mla_proj(x, w, pos):
    nb, t, d = x.shape
    tm = min(MLA_TOKEN_TILE, t)
    assert t % tm == 0
    nt = t // tm
    kw = MLA_KEY_WIDTH
    cos, sin = _rope_tables(pos)
    out_shape = (
        jax.ShapeDtypeStruct((nb, t, MLA_KV_RANK), F32),
        jax.ShapeDtypeStruct((nb, t, MLA_ROPE), F32),
        jax.ShapeDtypeStruct((nb, t, kw), BF16),
        jax.ShapeDtypeStruct((nb, nt, MLA_KV_RANK, tm), BF16),
        jax.ShapeDtypeStruct((nb, nt, kw, MLA_HEADS * tm), BF16),
    )
    in_specs = [
        pl.BlockSpec((1, tm, d), lambda i, j: (i, j, 0)),
        _full((1, d)),
        _full(w["w_in1"].shape),
        _full((1, MLA_Q_RANK)),
        _full((1, MLA_KV_RANK)),
        _full(w["w_q_nope"].shape),
        _full(w["w_q_pe"].shape),
        _full(w["w_q_pe_sw"].shape),
        _full(w["w_uk_h"].shape),
        pl.BlockSpec((tm, LANES), lambda i, j: (j, 0)),
        pl.BlockSpec((tm, LANES), lambda i, j: (j, 0)),
    ]
    out_specs = (
        pl.BlockSpec((1, tm, MLA_KV_RANK), lambda i, j: (i, j, 0)),
        pl.BlockSpec((1, tm, MLA_ROPE), lambda i, j: (i, j, 0)),
        pl.BlockSpec((1, tm, kw), lambda i, j: (i, j, 0)),
        pl.BlockSpec((1, 1, MLA_KV_RANK, tm), lambda i, j: (i, j, 0, 0)),
        pl.BlockSpec((1, 1, kw, MLA_HEADS * tm), lambda i, j: (i, j, 0, 0)),
    )
    return pl.pallas_call(
        _mla_proj_kernel, grid=(nb, nt), in_specs=in_specs, out_specs=out_specs, out_shape=out_shape,
        compiler_params=_params("arbitrary", "arbitrary"), name="mla_proj",
    )(x, w["g_mix1"], w["w_in1"], w["mla_q_norm"], w["mla_kv_norm"], w["w_q_nope"], w["w_q_pe"],
      w["w_q_pe_sw"], w["w_uk_h"], cos, sin)


def _flash_kernel(qt_ref, k_ref, vt_ref, x_ref, wuv_ref, wo_ref, y_ref, acc_s, m_s, l_s, asm_s, *, tq):
    qi = pl.program_id(1)
    tk = tq
    rows = MLA_HEADS * tq
    m_s[...] = jnp.full_like(m_s, -jnp.inf)
    l_s[...] = jnp.zeros_like(l_s)
    acc_s[...] = jnp.zeros_like(acc_s)
    gw = min(rows, FLASH_GROUP_LANES)
    assert gw % tq == 0 and rows % gw == 0

    def block(kb, on_diagonal):
        start = pl.multiple_of(kb * tk, tk)
        kblk = k_ref[0, pl.ds(start, tk), :]
        vblk = vt_ref[0, kb]
        for g in range(rows // gw):
            cols = slice(g * gw, (g + 1) * gw)
            s = _dot(kblk, qt_ref[0, 0, :, cols])
            if on_diagonal:
                key = lax.broadcasted_iota(jnp.int32, (tk, gw), 0)
                tok = lax.broadcasted_iota(jnp.int32, (tk, gw), 1) & (tq - 1)
                s = jnp.where(key <= tok, s, -jnp.inf)
            m_old = m_s[:, cols]
            m_new = jnp.maximum(m_old, jnp.max(s, axis=0, keepdims=True))
            alpha = jnp.exp((m_old - m_new) * MLA_SCALE)
            p = jnp.exp((s - m_new) * MLA_SCALE)
            l_s[:, cols] = alpha * l_s[:, cols] + jnp.sum(p, axis=0, keepdims=True)
            acc_s[:, cols] = alpha * acc_s[:, cols] + _dot(vblk, p.astype(BF16))
            m_s[:, cols] = m_new

    def body(kb, carry):
        block(kb, False)
        return carry

    lax.fori_loop(0, qi, body, 0)
    block(qi, True)
    o_t = acc_s[...] / l_s[...]
    for h in range(MLA_HEADS):
        oh = o_t[:, h * tq:(h + 1) * tq].astype(BF16)
        asm_s[h * MLA_V:(h + 1) * MLA_V, :] = _dot(wuv_ref[h], oh).astype(BF16)
    y_ref[0] = x_ref[0] + _dot_tn(asm_s[...], wo_ref[...])


def _mla_attend_prompt(q_t, keys, v_t, x, w):
    b, t, d = x.shape
    nt, tq = v_t.shape[1], v_t.shape[3]
    assert t == nt * tq and tq & (tq - 1) == 0
    rows = MLA_HEADS * tq
    kern = functools.partial(_flash_kernel, tq=tq)
    return pl.pallas_call(
        kern, grid=(b, nt),
        in_specs=[
            pl.BlockSpec((1, 1, MLA_KEY_WIDTH, rows), lambda i, j: (i, j, 0, 0)),
            pl.BlockSpec((1, t, MLA_KEY_WIDTH), lambda i, j: (i, 0, 0)),
            pl.BlockSpec((1, nt, MLA_KV_RANK, tq), lambda i, j: (i, 0, 0, 0)),
            pl.BlockSpec((1, tq, d), lambda i, j: (i, j, 0)),
            _full(w["w_uv_h"].shape),
            _full(w["w_out1"].shape),
        ],
        out_specs=pl.BlockSpec((1, tq, d), lambda i, j: (i, j, 0)),
        out_shape=jax.ShapeDtypeStruct((b, t, d), F32),
        scratch_shapes=[
            pltpu.VMEM((MLA_KV_RANK, rows), F32),
            pltpu.VMEM((1, rows), F32),
            pltpu.VMEM((1, rows), F32),
            pltpu.VMEM((MLA_HEADS * MLA_V, tq), BF16),
        ],
        compiler_params=_params("arbitrary", "arbitrary"), name="mla_flash",
    )(q_t, keys, v_t, x, w["w_uv_h"], w["w_out1"])


def _page_copies(pt_ref, lat_hbm, rope_hbm, lat_buf, rope_buf, sem, seq, slot, start):
    def body(p, carry):
        page = pt_ref[seq, p]
        copies = (pltpu.make_async_copy(lat_hbm.at[page], lat_buf.at[slot, p], sem.at[0, slot]),
                  pltpu.make_async_copy(rope_hbm.at[page], rope_buf.at[slot, p], sem.at[1, slot]))
        for c in copies:
            if start:
                c.start()
            else:
                c.wait()
        return carry

    lax.fori_loop(0, lat_buf.shape[1], body, 0)


def _decode_kernel(pt_ref, q_ref, kn_ref, lat_hbm, rope_hbm, o_ref, lat_buf, rope_buf, s_buf, sem, *, group):
    r = MLA_KV_RANK
    n_pages, page = lat_buf.shape[1], lat_buf.shape[2]
    b = pl.program_id(0)
    slot = lax.rem(b, 2)
    args = (pt_ref, lat_hbm, rope_hbm, lat_buf, rope_buf, sem)

    @pl.when(b == 0)
    def _():
        _page_copies(*args, 0, 0, True)

    @pl.when(b + 1 < pl.num_programs(0))
    def _():
        _page_copies(*args, b + 1, 1 - slot, True)

    _page_copies(*args, b, slot, False)

    q = q_ref[0]
    ql = q[:, 0:r]
    qp = q[:, r:r + MLA_ROPE]
    kn = kn_ref[0].astype(F32)
    s_new = jnp.sum(q.astype(F32) * kn, axis=1, keepdims=True)
    lat = lat_buf.at[slot]
    rope = rope_buf.at[slot]
    gk = group * page
    for g in range(n_pages // group):
        lat_g = lat[g * group:(g + 1) * group].reshape(gk, r).astype(BF16)
        s_pe = [_dot(qp, rope[g * group + i].astype(BF16)) for i in range(group)]
        s_buf[:, g * gk:(g + 1) * gk] = _dot_nt(ql, lat_g) + jnp.concatenate(s_pe, axis=1)
    s = s_buf[...]
    m = jnp.maximum(jnp.max(s, axis=1, keepdims=True), s_new)
    p_all = jnp.exp((s - m) * MLA_SCALE)
    p_new = jnp.exp((s_new - m) * MLA_SCALE)
    denom = jnp.sum(p_all, axis=1, keepdims=True) + p_new
    acc = p_new * kn[:, 0:r]
    for g in range(n_pages // group):
        lat_g = lat[g * group:(g + 1) * group].reshape(gk, r).astype(BF16)
        acc = acc + _dot(p_all[:, g * gk:(g + 1) * gk].astype(BF16), lat_g)
    o_ref[0] = acc / denom


def _mla_attend_sample(q, key_new, cache_latent, cache_rope_t, page_table):
    b = q.shape[0]
    n_pages = page_table.shape[1]
    page = cache_latent.shape[1]
    kw = MLA_KEY_WIDTH
    grid_spec = pltpu.PrefetchScalarGridSpec(
        num_scalar_prefetch=1, grid=(b,),
        in_specs=[
            pl.BlockSpec((1, MLA_HEADS, kw), lambda i, pt: (i, 0, 0)),
            pl.BlockSpec((1, 1, kw), lambda i, pt: (i, 0, 0)),
            pl.BlockSpec(memory_space=pl.ANY),
            pl.BlockSpec(memory_space=pl.ANY),
        ],
        out_specs=pl.BlockSpec((1, MLA_HEADS, MLA_KV_RANK), lambda i, pt: (i, 0, 0)),
        scratch_shapes=[
            pltpu.VMEM((2, n_pages, page, MLA_KV_RANK), F32),
            pltpu.VMEM((2, n_pages, MLA_ROPE, page), F32),
            pltpu.VMEM((MLA_HEADS, n_pages * page), F32),
            pltpu.SemaphoreType.DMA((2, 2)),
        ],
    )
    kern = functools.partial(_decode_kernel, group=math.gcd(n_pages, 8))
    return pl.pallas_call(
        kern, grid_spec=grid_spec, out_shape=jax.ShapeDtypeStruct((b, MLA_HEADS, MLA_KV_RANK), F32),
        compiler_params=_params("arbitrary"), name="mla_decode",
    )(page_table, q, key_new, cache_latent, cache_rope_t)


def _attn_out_kernel(o_ref, x_ref, wuv_ref, wo_ref, y_ref, asm_s):
    for h in range(MLA_HEADS):
        oh = o_ref[:, h * MLA_KV_RANK:(h + 1) * MLA_KV_RANK].astype(BF16)
        asm_s[:, h * MLA_V:(h + 1) * MLA_V] = _dot(oh, wuv_ref[h]).astype(BF16)
    y_ref[...] = x_ref[...] + _dot(asm_s[...], wo_ref[...])


def _attn_out_sample(o, x, w):
    m, d = x.shape
    tm = min(128, m)
    assert m % tm == 0
    ow = MLA_HEADS * MLA_KV_RANK
    return pl.pallas_call(
        _attn_out_kernel, grid=(m // tm,),
        in_specs=[pl.BlockSpec((tm, ow), lambda i: (i, 0)), pl.BlockSpec((tm, d), lambda i: (i, 0)),
                  _full(w["w_uv_t"].shape), _full(w["w_out1"].shape)],
        out_specs=pl.BlockSpec((tm, d), lambda i: (i, 0)),
        out_shape=jax.ShapeDtypeStruct((m, d), F32),
        scratch_shapes=[pltpu.VMEM((tm, MLA_HEADS * MLA_V), BF16)],
        compiler_params=_params("arbitrary"), name="attn_out",
    )(o.reshape(m, ow), x, w["w_uv_t"], w["w_out1"])


def _rope_tables(pos):
    half = MLA_ROPE // 2
    inv = ROPE_THETA ** (-jnp.arange(half, dtype=F32) * 2.0 / MLA_ROPE)
    ang = pos.astype(F32)[:, None] * inv[None, :]
    cos = jnp.cos(ang)
    sin = jnp.sin(ang)
    pad = jnp.zeros((pos.shape[0], LANES - MLA_ROPE), F32)
    return jnp.concatenate([cos, cos, pad], axis=1), jnp.concatenate([-sin, sin, pad], axis=1)


def _pad_lanes(a, width):
    return jnp.pad(a, ((0, 0), (0, width - a.shape[1])))


def _prepare_weights(norm_mix, norm_ffn, norm_final, w_in0, mlstm_b_i, mlstm_b_f, mlstm_norm, pool_w,
                     pool_scale, w_out0, w_in1, mla_q_norm, mla_kv_norm, w_q_up, w_uk, w_uv, w_out1,
                     w_up, w_down):
    d = w_in0.shape[0]
    half = MLA_ROPE // 2
    gate_i = w_in0[:, Z_MAIN:Z_MAIN + ML_HEADS]
    gate_f = w_in0[:, Z_MAIN + ML_HEADS:Z_MAIN + 2 * ML_HEADS]
    rope_k = w_in1[:, MLA_Q_RANK + MLA_KV_RANK:]
    rope_k_sw = jnp.concatenate([rope_k[:, half:], rope_k[:, :half]], axis=1)
    q_up = w_q_up.reshape(MLA_Q_RANK, MLA_HEADS, MLA_NOPE + MLA_ROPE)
    q_pe = q_up[:, :, MLA_NOPE:]
    q_pe_sw = jnp.concatenate([q_pe[:, :, half:], q_pe[:, :, :half]], axis=2)
    pad_pe = ((0, 0), (0, 0), (0, LANES - MLA_ROPE))
    col8 = lambda v: jnp.pad(v.astype(F32), (0, 8 - ML_HEADS)).reshape(8, 1)
    return {
        "g_mix0": norm_mix[0].reshape(1, d), "g_mix1": norm_mix[1].reshape(1, d),
        "g_ffn0": norm_ffn[0].reshape(1, d), "g_ffn1": norm_ffn[1].reshape(1, d),
        "g_final": norm_final.reshape(1, d),
        "w_in0_main": w_in0[:, :Z_MAIN].astype(BF16),
        "w_in0_gates": jnp.concatenate(
            [w_in0[:, :Z_MAIN], _pad_lanes(gate_i, LANES), _pad_lanes(gate_f, LANES)], axis=1).astype(BF16),
        "w_gate_i_t": jnp.pad(gate_i.T, ((0, 8 - ML_HEADS), (0, 0))).astype(BF16),
        "w_gate_f_t": jnp.pad(gate_f.T, ((0, 8 - ML_HEADS), (0, 0))).astype(BF16),
        "b_i_col": col8(mlstm_b_i), "b_f_col": col8(mlstm_b_f),
        "b_i_row": mlstm_b_i.reshape(1, ML_HEADS), "b_f_row": mlstm_b_f.reshape(1, ML_HEADS),
        "mlstm_norm": mlstm_norm.reshape(1, ML_WIDTH),
        "pool_w": pool_w.astype(BF16), "pool_scale": pool_scale.reshape(1, POOL_WIDTH),
        "w_out0": w_out0.astype(BF16),
        "w_in1": jnp.concatenate(
            [w_in1[:, :MLA_Q_RANK + MLA_KV_RANK], _pad_lanes(rope_k, LANES), _pad_lanes(rope_k_sw, LANES)],
            axis=1).astype(BF16),
        "mla_q_norm": mla_q_norm.reshape(1, MLA_Q_RANK), "mla_kv_norm": mla_kv_norm.reshape(1, MLA_KV_RANK),
        "w_q_nope": q_up[:, :, :MLA_NOPE].reshape(MLA_Q_RANK, MLA_HEADS * MLA_NOPE).astype(BF16),
        "w_q_pe": jnp.pad(q_pe, pad_pe).reshape(MLA_Q_RANK, MLA_HEADS * LANES).astype(BF16),
        "w_q_pe_sw": jnp.pad(q_pe_sw, pad_pe).reshape(MLA_Q_RANK, MLA_HEADS * LANES).astype(BF16),
        "w_uk_h": jnp.transpose(w_uk, (1, 0, 2)).astype(BF16),
        "w_uv_t": jnp.transpose(w_uv, (1, 0, 2)).astype(BF16),
        "w_uv_h": jnp.transpose(w_uv, (1, 2, 0)).astype(BF16),
        "w_out1": w_out1.astype(BF16),
        "w_up0": w_up[0].astype(BF16), "w_up1": w_up[1].astype(BF16),
        "w_down0": w_down[0].astype(BF16), "w_down1": w_down[1].astype(BF16),
    }


def kernel(x_prompt, x_sample, state_mlstm_C, state_mlstm_n, state_mlstm_m, state_pool, cache_latent, cache_rope_k, page_table, norm_mix, norm_ffn, norm_final, w_in0, mlstm_b_i, mlstm_b_f, mlstm_norm, pool_w, pool_scale, w_out0, w_in1, mla_q_norm, mla_kv_norm, w_q_up, w_uk, w_uv, w_out1, w_up, w_down):
    w = _prepare_weights(norm_mix, norm_ffn, norm_final, w_in0, mlstm_b_i, mlstm_b_f, mlstm_norm, pool_w,
                         pool_scale, w_out0, w_in1, mla_q_norm, mla_kv_norm, w_q_up, w_uk, w_uv, w_out1,
                         w_up, w_down)
    bp, t, d = x_prompt.shape
    bs, ts, _ = x_sample.shape
    assert ts == 1
    past_len = page_table.shape[1] * cache_latent.shape[1]

    x1, c_p, n_p, m_p, buf_p = _mixer0_prompt(x_prompt, w, 0)
    x2 = _ffn(x1.reshape(bp * t, d), w["g_ffn0"], w["w_up0"], w["w_down0"], w["g_final"], False)
    x2 = x2.reshape(bp, t, d)
    ckv_p, kpe_p, keys_p, vt_p, qt_p = _mla_proj(x2, w, jnp.arange(t))
    x3 = _mla_attend_prompt(qt_p, keys_p, vt_p, x2, w)
    y_p = _ffn(x3.reshape(bp * t, d), w["g_ffn1"], w["w_up1"], w["w_down1"], w["g_final"], True)
    y_p = y_p.reshape(bp, t, d)

    xs = x_sample.reshape(bs, d)
    xs1, c_s, n_s, m_s, buf_s = _mixer0_sample(xs, state_mlstm_C, state_mlstm_n, state_mlstm_m, state_pool,
                                               w, past_len)
    xs2 = _ffn(xs1, w["g_ffn0"], w["w_up0"], w["w_down0"], w["g_final"], False)
    ckv_s, kpe_s, keys_s, _, qt_s = _mla_proj(xs2.reshape(1, bs, d), w, jnp.full((bs,), past_len))
    assert qt_s.shape == (1, 1, MLA_KEY_WIDTH, MLA_HEADS * bs)
    q_s = jnp.transpose(qt_s.reshape(MLA_KEY_WIDTH, MLA_HEADS, bs), (2, 1, 0))
    rope_t = jnp.transpose(cache_rope_k, (0, 2, 1))
    o_s = _mla_attend_sample(q_s, keys_s.reshape(bs, 1, MLA_KEY_WIDTH), cache_latent, rope_t, page_table)
    xs3 = _attn_out_sample(o_s, xs2, w)
    y_s = _ffn(xs3, w["g_ffn1"], w["w_up1"], w["w_down1"], w["g_final"], True)

    return (y_p, y_s.reshape(bs, 1, d), c_p, n_p, m_p, buf_p, ckv_p, kpe_p,
            c_s, n_s, m_s, buf_s, ckv_s.reshape(bs, 1, MLA_KV_RANK), kpe_s.reshape(bs, 1, MLA_ROPE))
```

```python
import functools
import math

import jax
import jax.numpy as jnp
from jax import lax
from jax.experimental import pallas as pl
from jax.experimental.pallas import tpu as pltpu

F32 = jnp.float32
BF16 = jnp.bfloat16

EPS = 1e-6
ML_HEADS = 4
ML_HEAD_DIM = 128
ML_WIDTH = ML_HEADS * ML_HEAD_DIM
POOL_WINDOWS = (2, 4, 8, 16)
POOL_GROUP = 128
POOL_WIDTH = len(POOL_WINDOWS) * POOL_GROUP
POOL_BUF = max(POOL_WINDOWS) - 1
POOL_HIST = POOL_BUF + 1
MLA_HEADS = 8
MLA_NOPE = 128
MLA_ROPE = 64
MLA_V = 128
MLA_Q_RANK = 512
MLA_KV_RANK = 256
MLA_SCALE = (MLA_NOPE + MLA_ROPE) ** -0.5
ROPE_THETA = 10000.0
LANES = 128
MLA_KEY_WIDTH = MLA_KV_RANK + LANES
Z_MAIN = 4 * ML_WIDTH + POOL_WIDTH
FLASH_GROUP_LANES = 4096
MLA_TOKEN_TILE = 512

VMEM_LIMIT_BYTES = 48 * 1024 * 1024

_NT = (((1,), (1,)), ((), ()))
_TN = (((0,), (0,)), ((), ()))


def _dot(a, b):
    return jnp.dot(a, b, preferred_element_type=F32)


def _dot_nt(a, b):
    return lax.dot_general(a, b, _NT, preferred_element_type=F32)


def _dot_tn(a, b):
    return lax.dot_general(a, b, _TN, preferred_element_type=F32)


def _rms(x, g):
    return x * lax.rsqrt(jnp.mean(x * x, axis=-1, keepdims=True) + EPS) * g


def _log_sigmoid(x):
    return jnp.minimum(x, 0.0) - jnp.log1p(jnp.exp(-jnp.abs(x)))


def _head_norm(h, g):
    hc = h - jnp.mean(h, axis=-1, keepdims=True)
    return hc * lax.rsqrt(jnp.mean(hc * hc, axis=-1, keepdims=True) + EPS) * g


def _cumsum_lanes(x):
    n = x.shape[-1]
    lane = lax.broadcasted_iota(jnp.int32, x.shape, x.ndim - 1)
    s = 1
    while s < n:
        x = x + jnp.where(lane >= s, pltpu.roll(x, s, axis=x.ndim - 1), 0.0)
        s *= 2
    return x


def _params(*semantics):
    return pltpu.CompilerParams(dimension_semantics=semantics, vmem_limit_bytes=VMEM_LIMIT_BYTES)


def _full(shape):
    return pl.BlockSpec(shape, lambda *_: (0,) * len(shape))


def _in_proj(x, w_in, z_out, gi_out, gf_out):
    g_ref, win_ref, wgi_ref, wgf_ref, bi_ref, bf_ref = w_in
    hn = _rms(x, g_ref[...]).astype(BF16)
    z_out[...] = _dot(hn, win_ref[...])
    gi_out[...] = _dot_nt(wgi_ref[...], hn) + bi_ref[...]
    gf_out[...] = _log_sigmoid(_dot_nt(wgf_ref[...], hn) + bf_ref[...])


def _mixer0_tile(x, cur, nxt, w_in, gn_ref, pw_ref, ps_ref, wout_ref, asm_s, ext_s, c_s, n_s, m_s,
                 *, chunk, pos):
    z_s, gi_s, gf_s = cur
    tt = z_s.shape[0]
    dh = ML_HEAD_DIM
    units = [(c, h) for c in range(tt // chunk) for h in range(ML_HEADS)]
    rows_of = lambda c: slice(c * chunk, (c + 1) * chunk)
    head = lambda part, h: slice(part * ML_WIDTH + h * dh, part * ML_WIDTH + (h + 1) * dh)
    row = lax.broadcasted_iota(jnp.int32, (chunk, chunk), 0)
    col = lax.broadcasted_iota(jnp.int32, (chunk, chunk), 1)
    causal = row >= col
    diag = row == col

    gates = []
    m_prev8 = m_s[:, 0:1]
    for c in range(tt // chunk):
        bcum = _cumsum_lanes(gf_s[:, rows_of(c)])
        a = gi_s[:, rows_of(c)] - bcum
        m_last8 = jnp.maximum(m_prev8, jnp.max(a, axis=1, keepdims=True))
        gates.append((a, bcum, m_prev8, m_last8, jnp.exp(m_prev8 - m_last8)))
        m_prev8 = bcum[:, chunk - 1:chunk] + m_last8
    m_s[...] = jnp.broadcast_to(m_prev8, m_s.shape)

    scores = {}
    for c, h in units:
        qb = z_s[rows_of(c), head(0, h)].astype(BF16)
        kb = (z_s[rows_of(c), head(1, h)] * (dh ** -0.5)).astype(BF16)
        scores[c, h] = _dot_nt(qb, kb)

    _in_proj(nxt[0](), w_in, *nxt[1])

    parts = {}
    for c, h in units:
        a, bcum, m_prev8, m_last8, _ = gates[c]
        a_row = a[h:h + 1, :]
        m_prev = m_prev8[h:h + 1, :]
        a_mat = jnp.where(causal, a_row, -jnp.inf)
        m_col = jnp.maximum(jnp.max(a_mat, axis=1, keepdims=True), m_prev)
        b_col = jnp.sum(jnp.where(diag, bcum[h:h + 1, :], 0.0), axis=1, keepdims=True)
        a_col = jnp.sum(jnp.where(diag, a_row, 0.0), axis=1, keepdims=True)
        s = scores[c, h] * jnp.exp(a_mat - m_col)
        vb = z_s[rows_of(c), head(2, h)].astype(BF16)
        parts[c, h] = (_dot(s.astype(BF16), vb), jnp.sum(s, axis=1, keepdims=True),
                       jnp.exp(m_prev - m_col), jnp.exp(-(b_col + m_col)),
                       jnp.exp(a_col - m_last8[h:h + 1, :]))

    for c, h in units:
        num_intra, den_intra, w_inter, e_neg_m, w_tok = parts[c, h]
        w_old = gates[c][4][h:h + 1, :]
        q = z_s[rows_of(c), head(0, h)]
        k = z_s[rows_of(c), head(1, h)] * (dh ** -0.5)
        vb = z_s[rows_of(c), head(2, h)].astype(BF16)
        c_mat = c_s[h]
        n_row = n_s[h:h + 1, :]
        num = num_intra + w_inter * _dot_nt(q.astype(BF16), c_mat.astype(BF16))
        den = den_intra + w_inter * jnp.sum(q * n_row, axis=1, keepdims=True)
        hh = num / jnp.maximum(jnp.abs(den), e_neg_m)
        kw = k * w_tok
        c_s[h] = w_old * c_mat + _dot_tn(vb, kw.astype(BF16))
        n_s[h:h + 1, :] = w_old * n_row + jnp.sum(kw, axis=0, keepdims=True)
        o_gate = jax.nn.sigmoid(z_s[rows_of(c), head(3, h)])
        hm = _head_norm(hh, gn_ref[:, h * dh:(h + 1) * dh]) * o_gate
        asm_s[rows_of(c), h * dh:(h + 1) * dh] = hm.astype(BF16)

    ext_s[POOL_HIST:POOL_HIST + tt, :] = z_s[:, 4 * ML_WIDTH:Z_MAIN]
    for g, win in enumerate(POOL_WINDOWS):
        lanes = slice(g * POOL_GROUP, (g + 1) * POOL_GROUP)
        u = ext_s[POOL_HIST:POOL_HIST + tt, lanes]
        wsum = u
        for j in range(1, win):
            wsum = wsum + ext_s[POOL_HIST - j:POOL_HIST - j + tt, lanes]
        cnt = jnp.minimum(win, pos + 1).astype(F32)
        pooled = wsum / cnt - u
        og = _dot(pooled.astype(BF16), pw_ref[g]) * ps_ref[:, lanes]
        asm_s[:, ML_WIDTH + g * POOL_GROUP:ML_WIDTH + (g + 1) * POOL_GROUP] = og.astype(BF16)
    ext_s[0:POOL_HIST, :] = ext_s[tt:tt + POOL_HIST, :]

    return x() + _dot(asm_s[...], wout_ref[...])


def _mixer0_kernel(x_ref, xn_ref, g_ref, win_ref, wgi_ref, wgf_ref, bi_ref, bf_ref, gn_ref, pw_ref, ps_ref,
                   wout_ref, y_ref, c_out, n_out, m_out, buf_out,
                   za_s, gia_s, gfa_s, zb_s, gib_s, gfb_s, asm_s, ext_s, c_s, n_s, m_s, *, tt, chunk, pos0):
    b = pl.program_id(0)
    i = pl.program_id(1)
    w_in = (g_ref, win_ref, wgi_ref, wgf_ref, bi_ref, bf_ref)
    set_a = (za_s, gia_s, gfa_s)
    set_b = (zb_s, gib_s, gfb_s)

    @pl.when((b == 0) & (i == 0))
    def _():
        _in_proj(x_ref[0, 0:tt, :], w_in, *set_a)

    @pl.when(i == 0)
    def _():
        c_s[...] = jnp.zeros_like(c_s)
        n_s[...] = jnp.zeros_like(n_s)
        m_s[...] = jnp.zeros_like(m_s)
        ext_s[0:POOL_HIST, :] = jnp.zeros((POOL_HIST, POOL_WIDTH), F32)

    tile = functools.partial(_mixer0_tile, w_in=w_in, gn_ref=gn_ref, pw_ref=pw_ref, ps_ref=ps_ref,
                             wout_ref=wout_ref, asm_s=asm_s, ext_s=ext_s, c_s=c_s, n_s=n_s, m_s=m_s,
                             chunk=chunk)
    pos = pos0 + 2 * i * tt + lax.broadcasted_iota(jnp.int32, (tt, 1), 0)
    x0 = lambda: x_ref[0, 0:tt, :]
    x1 = lambda: x_ref[0, tt:2 * tt, :]
    y_ref[0, 0:tt, :] = tile(x0, set_a, (x1, set_b), pos=pos)
    y_ref[0, tt:2 * tt, :] = tile(x1, set_b, (lambda: xn_ref[0], set_a), pos=pos + tt)

    @pl.when(i == pl.num_programs(1) - 1)
    def _():
        c_out[0] = c_s[...]
        n_out[0] = n_s[0:ML_HEADS, :]
        m_out[0] = m_s[...]
        buf_out[0] = ext_s[0:POOL_HIST, :]


def _mixer0_prompt(x, w, pos0):
    b, t, d = x.shape
    tt = min(256, t // 2)
    chunk = min(128, tt)
    assert t % (2 * tt) == 0 and tt % chunk == 0 and tt >= POOL_HIST
    n_pairs = t // (2 * tt)
    kern = functools.partial(_mixer0_kernel, tt=tt, chunk=chunk, pos0=pos0)

    def next_tile(i, j):
        nxt = jnp.minimum(i * n_pairs + j + 1, b * n_pairs - 1)
        return (nxt // n_pairs, 2 * (nxt % n_pairs), 0)

    out_shape = (
        jax.ShapeDtypeStruct((b, t, d), F32),
        jax.ShapeDtypeStruct((b, ML_HEADS, ML_HEAD_DIM, ML_HEAD_DIM), F32),
        jax.ShapeDtypeStruct((b, ML_HEADS, ML_HEAD_DIM), F32),
        jax.ShapeDtypeStruct((b, 8, LANES), F32),
        jax.ShapeDtypeStruct((b, POOL_HIST, POOL_WIDTH), F32),
    )
    in_specs = [
        pl.BlockSpec((1, 2 * tt, d), lambda i, j: (i, j, 0)),
        pl.BlockSpec((1, tt, d), next_tile),
        _full((1, d)),
        _full((d, Z_MAIN)),
        _full((8, d)),
        _full((8, d)),
        _full((8, 1)),
        _full((8, 1)),
        _full((1, ML_WIDTH)),
        _full((len(POOL_WINDOWS), POOL_GROUP, POOL_GROUP)),
        _full((1, POOL_WIDTH)),
        _full((ML_WIDTH + POOL_WIDTH, d)),
    ]
    out_specs = (
        pl.BlockSpec((1, 2 * tt, d), lambda i, j: (i, j, 0)),
        pl.BlockSpec((1, ML_HEADS, ML_HEAD_DIM, ML_HEAD_DIM), lambda i, j: (i, 0, 0, 0)),
        pl.BlockSpec((1, ML_HEADS, ML_HEAD_DIM), lambda i, j: (i, 0, 0)),
        pl.BlockSpec((1, 8, LANES), lambda i, j: (i, 0, 0)),
        pl.BlockSpec((1, POOL_HIST, POOL_WIDTH), lambda i, j: (i, 0, 0)),
    )
    proj_set = [pltpu.VMEM((tt, Z_MAIN), F32), pltpu.VMEM((8, tt), F32), pltpu.VMEM((8, tt), F32)]
    scratch = proj_set + proj_set + [
        pltpu.VMEM((tt, ML_WIDTH + POOL_WIDTH), BF16),
        pltpu.VMEM((tt + POOL_HIST, POOL_WIDTH), F32),
        pltpu.VMEM((ML_HEADS, ML_HEAD_DIM, ML_HEAD_DIM), F32),
        pltpu.VMEM((8, ML_HEAD_DIM), F32),
        pltpu.VMEM((8, LANES), F32),
    ]
    y, c, n, m, buf = pl.pallas_call(
        kern, grid=(b, n_pairs), in_specs=in_specs, out_specs=out_specs, out_shape=out_shape,
        scratch_shapes=scratch, compiler_params=_params("arbitrary", "arbitrary"), name="mixer0_prompt",
    )(x, x, w["g_mix0"], w["w_in0_main"], w["w_gate_i_t"], w["w_gate_f_t"], w["b_i_col"], w["b_f_col"],
      w["mlstm_norm"], w["pool_w"], w["pool_scale"], w["w_out0"])
    return y, c, n, m[:, :ML_HEADS, 0], buf[:, 1:, :]


def _norm_matmul_kernel(x_ref, g_ref, w_ref, o_ref):
    o_ref[...] = _dot(_rms(x_ref[...], g_ref[...]).astype(BF16), w_ref[...])


def _norm_matmul(x, g, w_bf16):
    m, d = x.shape
    n = w_bf16.shape[1]
    tm = min(128, m)
    assert m % tm == 0
    return pl.pallas_call(
        _norm_matmul_kernel, grid=(m // tm,),
        in_specs=[pl.BlockSpec((tm, d), lambda i: (i, 0)), _full((1, d)), _full((d, n))],
        out_specs=pl.BlockSpec((tm, n), lambda i: (i, 0)),
        out_shape=jax.ShapeDtypeStruct((m, n), F32),
        compiler_params=_params("arbitrary"), name="norm_matmul",
    )(x, g, w_bf16)


def _mixer0_step_kernel(z_ref, x_ref, c0_ref, n0_ref, m0_ref, buf_ref, bi_ref, bf_ref, gn_ref, pw_ref,
                        ps_ref, wout_ref, y_ref, c_ref, n_ref, m_ref, bufo_ref, cq_s, asm_s, *, bb, cnts):
    dh = ML_HEAD_DIM
    gate_i = Z_MAIN
    gate_f = Z_MAIN + LANES
    ig = z_ref[:, gate_i:gate_i + ML_HEADS] + bi_ref[...]
    lf = _log_sigmoid(z_ref[:, gate_f:gate_f + ML_HEADS] + bf_ref[...])
    m0 = m0_ref[...]
    m_t = jnp.maximum(lf + m0, ig)
    w_i = jnp.exp(ig - m_t)
    w_f = jnp.exp(lf + m0 - m_t)
    e_m = jnp.exp(-m_t)
    m_ref[...] = m_t

    row8 = lax.broadcasted_iota(jnp.int32, (8, dh), 0) == 0
    for h in range(ML_HEADS):
        q = z_ref[:, h * dh:(h + 1) * dh]
        k = z_ref[:, ML_WIDTH + h * dh:ML_WIDTH + (h + 1) * dh] * (dh ** -0.5)
        v = z_ref[:, 2 * ML_WIDTH + h * dh:2 * ML_WIDTH + (h + 1) * dh]
        wi = w_i[:, h:h + 1]
        wf = w_f[:, h:h + 1]
        kw = k * wi
        for b in range(bb):
            c_mat = c0_ref[b, h]
            q8 = jnp.broadcast_to(q[b:b + 1, :], (8, dh))
            cq_s[b:b + 1, :] = _dot_nt(q8, c_mat)[0:1, :]
            v8 = jnp.where(row8, v[b:b + 1, :], 0.0)
            k8 = jnp.where(row8, kw[b:b + 1, :], 0.0)
            c_ref[b, h] = wf[b:b + 1, :] * c_mat + _dot_tn(v8, k8)
        n_old = n0_ref[:, h, :]
        s = jnp.sum(q * k, axis=1, keepdims=True) * wi
        num = s * v + wf * cq_s[...]
        den = s + wf * jnp.sum(n_old * q, axis=1, keepdims=True)
        hh = num / jnp.maximum(jnp.abs(den), e_m[:, h:h + 1])
        n_ref[:, h, :] = wf * n_old + kw
        o_gate = jax.nn.sigmoid(z_ref[:, 3 * ML_WIDTH + h * dh:3 * ML_WIDTH + (h + 1) * dh])
        hm = _head_norm(hh, gn_ref[:, h * dh:(h + 1) * dh]) * o_gate
        asm_s[:, h * dh:(h + 1) * dh] = hm

    u_all = z_ref[:, 4 * ML_WIDTH:Z_MAIN]
    for g, win in enumerate(POOL_WINDOWS):
        u = u_all[:, g * POOL_GROUP:(g + 1) * POOL_GROUP]
        wsum = u
        for j in range(1, win):
            off = (POOL_BUF - j) * POOL_WIDTH + g * POOL_GROUP
            wsum = wsum + buf_ref[:, off:off + POOL_GROUP]
        pooled = wsum / cnts[g] - u
        og = _dot(pooled.astype(BF16), pw_ref[g]) * ps_ref[:, g * POOL_GROUP:(g + 1) * POOL_GROUP]
        asm_s[:, ML_WIDTH + g * POOL_GROUP:ML_WIDTH + (g + 1) * POOL_GROUP] = og
    keep = (POOL_BUF - 1) * POOL_WIDTH
    bufo_ref[:, 0:keep] = buf_ref[:, POOL_WIDTH:POOL_WIDTH + keep]
    bufo_ref[:, keep:keep + POOL_WIDTH] = u_all

    y_ref[...] = x_ref[...] + _dot(asm_s[...].astype(BF16), wout_ref[...])


def _mixer0_sample(x, c0, n0, m0, buf0, w, pos0):
    b, d = x.shape
    z = _norm_matmul(x, w["g_mix0"], w["w_in0_gates"])
    zw = z.shape[1]
    bb = 8
    assert b % bb == 0
    cnts = tuple(float(min(win, pos0 + 1)) for win in POOL_WINDOWS)
    bufw = POOL_BUF * POOL_WIDTH
    kern = functools.partial(_mixer0_step_kernel, bb=bb, cnts=cnts)
    hd = (ML_HEADS, ML_HEAD_DIM)
    in_specs = [
        pl.BlockSpec((bb, zw), lambda i: (i, 0)),
        pl.BlockSpec((bb, d), lambda i: (i, 0)),
        pl.BlockSpec((bb,) + hd + (ML_HEAD_DIM,), lambda i: (i, 0, 0, 0)),
        pl.BlockSpec((bb,) + hd, lambda i: (i, 0, 0)),
        pl.BlockSpec((bb, ML_HEADS), lambda i: (i, 0)),
        pl.BlockSpec((bb, bufw), lambda i: (i, 0)),
        _full((1, ML_HEADS)),
        _full((1, ML_HEADS)),
        _full((1, ML_WIDTH)),
        _full((len(POOL_WINDOWS), POOL_GROUP, POOL_GROUP)),
        _full((1, POOL_WIDTH)),
        _full((ML_WIDTH + POOL_WIDTH, d)),
    ]
    out_specs = (
        pl.BlockSpec((bb, d), lambda i: (i, 0)),
        pl.BlockSpec((bb,) + hd + (ML_HEAD_DIM,), lambda i: (i, 0, 0, 0)),
        pl.BlockSpec((bb,) + hd, lambda i: (i, 0, 0)),
        pl.BlockSpec((bb, ML_HEADS), lambda i: (i, 0)),
        pl.BlockSpec((bb, bufw), lambda i: (i, 0)),
    )
    out_shape = (
        jax.ShapeDtypeStruct((b, d), F32),
        jax.ShapeDtypeStruct((b,) + hd + (ML_HEAD_DIM,), F32),
        jax.ShapeDtypeStruct((b,) + hd, F32),
        jax.ShapeDtypeStruct((b, ML_HEADS), F32),
        jax.ShapeDtypeStruct((b, bufw), F32),
    )
    scratch = [pltpu.VMEM((bb, ML_HEAD_DIM), F32), pltpu.VMEM((bb, ML_WIDTH + POOL_WIDTH), F32)]
    y, c, n, m, buf = pl.pallas_call(
        kern, grid=(b // bb,), in_specs=in_specs, out_specs=out_specs, out_shape=out_shape,
        scratch_shapes=scratch, compiler_params=_params("arbitrary"), name="mixer0_sample",
    )(z, x, c0, n0, m0, buf0.reshape(b, bufw), w["b_i_row"], w["b_f_row"], w["mlstm_norm"],
      w["pool_w"], w["pool_scale"], w["w_out0"])
    return y, c, n, m, buf.reshape(b, POOL_BUF, POOL_WIDTH)


def _ffn_kernel(x_ref, g_ref, wu_ref, wd_ref, gf_ref, y_ref, hn_s, acc_s, *, final_norm):
    f = pl.program_id(1)

    @pl.when(f == 0)
    def _():
        hn_s[...] = _rms(x_ref[...], g_ref[...]).astype(BF16)
        acc_s[...] = jnp.zeros_like(acc_s)

    a = jnp.maximum(_dot(hn_s[...], wu_ref[...]), 0.0)
    acc_s[...] += _dot((a * a).astype(BF16), wd_ref[...])

    @pl.when(f == pl.num_programs(1) - 1)
    def _():
        y = x_ref[...] + acc_s[...]
        if final_norm:
            y = _rms(y, gf_ref[...])
        y_ref[...] = y


def _ffn(x, g, w_up, w_down, g_final, final_norm):
    m, d = x.shape
    dff = w_up.shape[1]
    tm = min(512, m)
    tf = min(1024, dff)
    assert m % tm == 0 and dff % tf == 0
    kern = functools.partial(_ffn_kernel, final_norm=final_norm)
    return pl.pallas_call(
        kern, grid=(m // tm, dff // tf),
        in_specs=[
            pl.BlockSpec((tm, d), lambda i, j: (i, 0)),
            _full((1, d)),
            pl.BlockSpec((d, tf), lambda i, j: (0, j)),
            pl.BlockSpec((tf, d), lambda i, j: (j, 0)),
            _full((1, d)),
        ],
        out_specs=pl.BlockSpec((tm, d), lambda i, j: (i, 0)),
        out_shape=jax.ShapeDtypeStruct((m, d), F32),
        scratch_shapes=[pltpu.VMEM((tm, d), BF16), pltpu.VMEM((tm, d), F32)],
        compiler_params=_params("arbitrary", "arbitrary"), name="ffn",
    )(x, g, w_up, w_down, g_final)


def _mla_proj_kernel(x_ref, g_ref, win_ref, gq_ref, gkv_ref, wqn_ref, wqp_ref, wqs_ref, wuk_ref,
                     cos_ref, sin_ref, ckv_ref, kpe_ref, key_ref, vt_ref, qt_ref):
    r = MLA_KV_RANK
    tm = x_ref.shape[1]
    hn = _rms(x_ref[0], g_ref[...]).astype(BF16)
    z = _dot(hn, win_ref[...])
    cq = _rms(z[:, 0:MLA_Q_RANK], gq_ref[...]).astype(BF16)
    ckv = _rms(z[:, MLA_Q_RANK:MLA_Q_RANK + r], gkv_ref[...])
    cos = cos_ref[...]
    sin = sin_ref[...]
    kpe = z[:, MLA_Q_RANK + r:MLA_Q_RANK + r + LANES] * cos + z[:, MLA_Q_RANK + r + LANES:] * sin
    ckv_ref[0] = ckv
    kpe_ref[0] = kpe[:, 0:MLA_ROPE]
    key_ref[0, :, 0:r] = ckv.astype(BF16)
    key_ref[0, :, r:r + LANES] = kpe.astype(BF16)
    vt_ref[0, 0] = ckv.T.astype(BF16)
    qn = _dot(cq, wqn_ref[...])
    qp = _dot(cq, wqp_ref[...])
    qs = _dot(cq, wqs_ref[...])
    for h in range(MLA_HEADS):
        lanes = slice(h * LANES, (h + 1) * LANES)
        cols = slice(h * tm, (h + 1) * tm)
        q_lat_t = _dot_nt(wuk_ref[h], qn[:, lanes].astype(BF16))
        q_pe = qp[:, lanes] * cos + qs[:, lanes] * sin
        qt_ref[0, 0, 0:r, cols] = q_lat_t.astype(BF16)
        qt_ref[0, 0, r:r + LANES, cols] = q_pe.T.astype(BF16)


def _mla_proj(x, w, pos):
    nb, t, d = x.shape
    tm = min(MLA_TOKEN_TILE, t)
    assert t % tm == 0
    nt = t // tm
    kw = MLA_KEY_WIDTH
    cos, sin = _rope_tables(pos)
    out_shape = (
        jax.ShapeDtypeStruct((nb, t, MLA_KV_RANK), F32),
        jax.ShapeDtypeStruct((nb, t, MLA_ROPE), F32),
        jax.ShapeDtypeStruct((nb, t, kw), BF16),
        jax.ShapeDtypeStruct((nb, nt, MLA_KV_RANK, tm), BF16),
        jax.ShapeDtypeStruct((nb, nt, kw, MLA_HEADS * tm), BF16),
    )
    in_specs = [
        pl.BlockSpec((1, tm, d), lambda i, j: (i, j, 0)),
        _full((1, d)),
        _full(w["w_in1"].shape),
        _full((1, MLA_Q_RANK)),
        _full((1, MLA_KV_RANK)),
        _full(w["w_q_nope"].shape),
        _full(w["w_q_pe"].shape),
        _full(w["w_q_pe_sw"].shape),
        _full(w["w_uk_h"].shape),
        pl.BlockSpec((tm, LANES), lambda i, j: (j, 0)),
        pl.BlockSpec((tm, LANES), lambda i, j: (j, 0)),
    ]
    out_specs = (
        pl.BlockSpec((1, tm, MLA_KV_RANK), lambda i, j: (i, j, 0)),
        pl.BlockSpec((1, tm, MLA_ROPE), lambda i, j: (i, j, 0)),
        pl.BlockSpec((1, tm, kw), lambda i, j: (i, j, 0)),
        pl.BlockSpec((1, 1, MLA_KV_RANK, tm), lambda i, j: (i, j, 0, 0)),
        pl.BlockSpec((1, 1, kw, MLA_HEADS * tm), lambda i, j: (i, j, 0, 0)),
    )
    return pl.pallas_call(
        _mla_proj_kernel, grid=(nb, nt), in_specs=in_specs, out_specs=out_specs, out_shape=out_shape,
        compiler_params=_params("arbitrary", "arbitrary"), name="mla_proj",
    )(x, w["g_mix1"], w["w_in1"], w["mla_q_norm"], w["mla_kv_norm"], w["w_q_nope"], w["w_q_pe"],
      w["w_q_pe_sw"], w["w_uk_h"], cos, sin)


def _flash_kernel(qt_ref, k_ref, vt_ref, x_ref, wuv_ref, wo_ref, y_ref, acc_s, m_s, l_s, asm_s, *, tq):
    qi = pl.program_id(1)
    tk = tq
    rows = MLA_HEADS * tq
    m_s[...] = jnp.full_like(m_s, -jnp.inf)
    l_s[...] = jnp.zeros_like(l_s)
    acc_s[...] = jnp.zeros_like(acc_s)
    gw = min(rows, FLASH_GROUP_LANES)
    assert gw % tq == 0 and rows % gw == 0

    def block(kb, on_diagonal):
        start = pl.multiple_of(kb * tk, tk)
        kblk = k_ref[0, pl.ds(start, tk), :]
        vblk = vt_ref[0, kb]
        n_groups = rows // gw
        scores = None
        for g in range(n_groups + 1):
            prev = scores
            if g < n_groups:
                scores = _dot(kblk, qt_ref[0, 0, :, g * gw:(g + 1) * gw])
            if g == 0:
                continue
            cols = slice((g - 1) * gw, g * gw)
            s = prev
            if on_diagonal:
                key = lax.broadcasted_iota(jnp.int32, (tk, gw), 0)
                tok = lax.broadcasted_iota(jnp.int32, (tk, gw), 1) & (tq - 1)
                s = jnp.where(key <= tok, s, -jnp.inf)
            m_old = m_s[:, cols]
            m_new = jnp.maximum(m_old, jnp.max(s, axis=0, keepdims=True))
            alpha = jnp.exp((m_old - m_new) * MLA_SCALE)
            p = jnp.exp((s - m_new) * MLA_SCALE)
            l_s[:, cols] = alpha * l_s[:, cols] + jnp.sum(p, axis=0, keepdims=True)
            acc_s[:, cols] = alpha * acc_s[:, cols] + _dot(vblk, p.astype(BF16))
            m_s[:, cols] = m_new

    def body(kb, carry):
        block(kb, False)
        return carry

    lax.fori_loop(0, qi, body, 0)
    block(qi, True)
    o_t = acc_s[...] / l_s[...]
    for h in range(MLA_HEADS):
        oh = o_t[:, h * tq:(h + 1) * tq].astype(BF16)
        asm_s[h * MLA_V:(h + 1) * MLA_V, :] = _dot(wuv_ref[h], oh).astype(BF16)
    y_ref[0] = x_ref[0] + _dot_tn(asm_s[...], wo_ref[...])


def _mla_attend_prompt(q_t, keys, v_t, x, w):
    b, t, d = x.shape
    nt, tq = v_t.shape[1], v_t.shape[3]
    assert t == nt * tq and tq & (tq - 1) == 0
    rows = MLA_HEADS * tq
    kern = functools.partial(_flash_kernel, tq=tq)
    return pl.pallas_call(
        kern, grid=(b, nt),
        in_specs=[
            pl.BlockSpec((1, 1, MLA_KEY_WIDTH, rows), lambda i, j: (i, j, 0, 0)),
            pl.BlockSpec((1, t, MLA_KEY_WIDTH), lambda i, j: (i, 0, 0)),
            pl.BlockSpec((1, nt, MLA_KV_RANK, tq), lambda i, j: (i, 0, 0, 0)),
            pl.BlockSpec((1, tq, d), lambda i, j: (i, j, 0)),
            _full(w["w_uv_h"].shape),
            _full(w["w_out1"].shape),
        ],
        out_specs=pl.BlockSpec((1, tq, d), lambda i, j: (i, j, 0)),
        out_shape=jax.ShapeDtypeStruct((b, t, d), F32),
        scratch_shapes=[
            pltpu.VMEM((MLA_KV_RANK, rows), F32),
            pltpu.VMEM((1, rows), F32),
            pltpu.VMEM((1, rows), F32),
            pltpu.VMEM((MLA_HEADS * MLA_V, tq), BF16),
        ],
        compiler_params=_params("arbitrary", "arbitrary"), name="mla_flash",
    )(q_t, keys, v_t, x, w["w_uv_h"], w["w_out1"])


def _page_copies(pt_ref, lat_hbm, rope_hbm, lat_buf, rope_buf, sem, seq, slot, start, pages=None):
    def body(p, carry):
        page = pt_ref[seq, p]
        copies = (pltpu.make_async_copy(lat_hbm.at[page], lat_buf.at[slot, p], sem.at[0, slot]),
                  pltpu.make_async_copy(rope_hbm.at[page], rope_buf.at[slot, p], sem.at[1, slot]))
        for c in copies:
            if start:
                c.start()
            else:
                c.wait()
        return carry

    if pages is None:
        lax.fori_loop(0, lat_buf.shape[1], body, 0)
    else:
        for p in pages:
            body(p, 0)


def _decode_kernel(pt_ref, q_ref, kn_ref, lat_hbm, rope_hbm, o_ref, lat_buf, rope_buf, s_buf, sem, *, group):
    r = MLA_KV_RANK
    n_pages, page = lat_buf.shape[1], lat_buf.shape[2]
    b = pl.program_id(0)
    last = pl.num_programs(0) - 1
    slot = lax.rem(b, 2)
    args = (pt_ref, lat_hbm, rope_hbm, lat_buf, rope_buf, sem)

    @pl.when(b == 0)
    def _():
        _page_copies(*args, 0, 0, True)

    _page_copies(*args, b, slot, False, range(n_pages))
    nxt = jnp.minimum(b + 1, last)

    q = q_ref[0]
    ql = q[:, 0:r]
    qp = q[:, r:r + MLA_ROPE]
    kn = kn_ref[0].astype(F32)
    s_new = jnp.sum(q.astype(F32) * kn, axis=1, keepdims=True)
    lat = lat_buf.at[slot]
    rope = rope_buf.at[slot]
    gk = group * page
    for g in range(n_pages // group):
        lat_g = lat[g * group:(g + 1) * group].reshape(gk, r).astype(BF16)
        s_pe = [_dot(qp, rope[g * group + i].astype(BF16)) for i in range(group)]
        s_buf[:, g * gk:(g + 1) * gk] = _dot_nt(ql, lat_g) + jnp.concatenate(s_pe, axis=1)
        _page_copies(*args, nxt, 1 - slot, True, range(g * group, (g + 1) * group))
    s = s_buf[...]
    m = jnp.maximum(jnp.max(s, axis=1, keepdims=True), s_new)
    p_all = jnp.exp((s - m) * MLA_SCALE)
    p_new = jnp.exp((s_new - m) * MLA_SCALE)
    denom = jnp.sum(p_all, axis=1, keepdims=True) + p_new
    acc = p_new * kn[:, 0:r]
    for g in range(n_pages // group):
        lat_g = lat[g * group:(g + 1) * group].reshape(gk, r).astype(BF16)
        acc = acc + _dot(p_all[:, g * gk:(g + 1) * gk].astype(BF16), lat_g)
    o_ref[0] = acc / denom

    @pl.when(b == last)
    def _():
        _page_copies(*args, last, 1 - slot, False)


def _mla_attend_sample(q, key_new, cache_latent, cache_rope_t, page_table):
    b = q.shape[0]
    n_pages = page_table.shape[1]
    page = cache_latent.shape[1]
    kw = MLA_KEY_WIDTH
    grid_spec = pltpu.PrefetchScalarGridSpec(
        num_scalar_prefetch=1, grid=(b,),
        in_specs=[
            pl.BlockSpec((1, MLA_HEADS, kw), lambda i, pt: (i, 0, 0)),
            pl.BlockSpec((1, 1, kw), lambda i, pt: (i, 0, 0)),
            pl.BlockSpec(memory_space=pl.ANY),
            pl.BlockSpec(memory_space=pl.ANY),
        ],
        out_specs=pl.BlockSpec((1, MLA_HEADS, MLA_KV_RANK), lambda i, pt: (i, 0, 0)),
        scratch_shapes=[
            pltpu.VMEM((2, n_pages, page, MLA_KV_RANK), F32),
            pltpu.VMEM((2, n_pages, MLA_ROPE, page), F32),
            pltpu.VMEM((MLA_HEADS, n_pages * page), F32),
            pltpu.SemaphoreType.DMA((2, 2)),
        ],
    )
    kern = functools.partial(_decode_kernel, group=math.gcd(n_pages, 8))
    return pl.pallas_call(
        kern, grid_spec=grid_spec, out_shape=jax.ShapeDtypeStruct((b, MLA_HEADS, MLA_KV_RANK), F32),
        compiler_params=_params("arbitrary"), name="mla_decode",
    )(page_table, q, key_new, cache_latent, cache_rope_t)


def _attn_out_kernel(o_ref, x_ref, wuv_ref, wo_ref, y_ref, asm_s):
    for h in range(MLA_HEADS):
        oh = o_ref[:, h * MLA_KV_RANK:(h + 1) * MLA_KV_RANK].astype(BF16)
        asm_s[:, h * MLA_V:(h + 1) * MLA_V] = _dot(oh, wuv_ref[h]).astype(BF16)
    y_ref[...] = x_ref[...] + _dot(asm_s[...], wo_ref[...])


def _attn_out_sample(o, x, w):
    m, d = x.shape
    tm = min(128, m)
    assert m % tm == 0
    ow = MLA_HEADS * MLA_KV_RANK
    return pl.pallas_call(
        _attn_out_kernel, grid=(m // tm,),
        in_specs=[pl.BlockSpec((tm, ow), lambda i: (i, 0)), pl.BlockSpec((tm, d), lambda i: (i, 0)),
                  _full(w["w_uv_t"].shape), _full(w["w_out1"].shape)],
        out_specs=pl.BlockSpec((tm, d), lambda i: (i, 0)),
        out_shape=jax.ShapeDtypeStruct((m, d), F32),
        scratch_shapes=[pltpu.VMEM((tm, MLA_HEADS * MLA_V), BF16)],
        compiler_params=_params("arbitrary"), name="attn_out",
    )(o.reshape(m, ow), x, w["w_uv_t"], w["w_out1"])


def _rope_tables(pos):
    half = MLA_ROPE // 2
    inv = ROPE_THETA ** (-jnp.arange(half, dtype=F32) * 2.0 / MLA_ROPE)
    ang = pos.astype(F32)[:, None] * inv[None, :]
    cos = jnp.cos(ang)
    sin = jnp.sin(ang)
    pad = jnp.zeros((pos.shape[0], LANES - MLA_ROPE), F32)
    return jnp.concatenate([cos, cos, pad], axis=1), jnp.concatenate([-sin, sin, pad], axis=1)


def _pad_lanes(a, width):
    return jnp.pad(a, ((0, 0), (0, width - a.shape[1])))


def _prepare_weights(norm_mix, norm_ffn, norm_final, w_in0, mlstm_b_i, mlstm_b_f, mlstm_norm, pool_w,
                     pool_scale, w_out0, w_in1, mla_q_norm, mla_kv_norm, w_q_up, w_uk, w_uv, w_out1,
                     w_up, w_down):
    d = w_in0.shape[0]
    half = MLA_ROPE // 2
    gate_i = w_in0[:, Z_MAIN:Z_MAIN + ML_HEADS]
    gate_f = w_in0[:, Z_MAIN + ML_HEADS:Z_MAIN + 2 * ML_HEADS]
    rope_k = w_in1[:, MLA_Q_RANK + MLA_KV_RANK:]
    rope_k_sw = jnp.concatenate([rope_k[:, half:], rope_k[:, :half]], axis=1)
    q_up = w_q_up.reshape(MLA_Q_RANK, MLA_HEADS, MLA_NOPE + MLA_ROPE)
    q_pe = q_up[:, :, MLA_NOPE:]
    q_pe_sw = jnp.concatenate([q_pe[:, :, half:], q_pe[:, :, :half]], axis=2)
    pad_pe = ((0, 0), (0, 0), (0, LANES - MLA_ROPE))
    col8 = lambda v: jnp.pad(v.astype(F32), (0, 8 - ML_HEADS)).reshape(8, 1)
    return {
        "g_mix0": norm_mix[0].reshape(1, d), "g_mix1": norm_mix[1].reshape(1, d),
        "g_ffn0": norm_ffn[0].reshape(1, d), "g_ffn1": norm_ffn[1].reshape(1, d),
        "g_final": norm_final.reshape(1, d),
        "w_in0_main": w_in0[:, :Z_MAIN].astype(BF16),
        "w_in0_gates": jnp.concatenate(
            [w_in0[:, :Z_MAIN], _pad_lanes(gate_i, LANES), _pad_lanes(gate_f, LANES)], axis=1).astype(BF16),
        "w_gate_i_t": jnp.pad(gate_i.T, ((0, 8 - ML_HEADS), (0, 0))).astype(BF16),
        "w_gate_f_t": jnp.pad(gate_f.T, ((0, 8 - ML_HEADS), (0, 0))).astype(BF16),
        "b_i_col": col8(mlstm_b_i), "b_f_col": col8(mlstm_b_f),
        "b_i_row": mlstm_b_i.reshape(1, ML_HEADS), "b_f_row": mlstm_b_f.reshape(1, ML_HEADS),
        "mlstm_norm": mlstm_norm.reshape(1, ML_WIDTH),
        "pool_w": pool_w.astype(BF16), "pool_scale": pool_scale.reshape(1, POOL_WIDTH),
        "w_out0": w_out0.astype(BF16),
        "w_in1": jnp.concatenate(
            [w_in1[:, :MLA_Q_RANK + MLA_KV_RANK], _pad_lanes(rope_k, LANES), _pad_lanes(rope_k_sw, LANES)],
            axis=1).astype(BF16),
        "mla_q_norm": mla_q_norm.reshape(1, MLA_Q_RANK), "mla_kv_norm": mla_kv_norm.reshape(1, MLA_KV_RANK),
        "w_q_nope": q_up[:, :, :MLA_NOPE].reshape(MLA_Q_RANK, MLA_HEADS * MLA_NOPE).astype(BF16),
        "w_q_pe": jnp.pad(q_pe, pad_pe).reshape(MLA_Q_RANK, MLA_HEADS * LANES).astype(BF16),
        "w_q_pe_sw": jnp.pad(q_pe_sw, pad_pe).reshape(MLA_Q_RANK, MLA_HEADS * LANES).astype(BF16),
        "w_uk_h": jnp.transpose(w_uk, (1, 0, 2)).astype(BF16),
        "w_uv_t": jnp.transpose(w_uv, (1, 0, 2)).astype(BF16),
        "w_uv_h": jnp.transpose(w_uv, (1, 2, 0)).astype(BF16),
        "w_out1": w_out1.astype(BF16),
        "w_up0": w_up[0].astype(BF16), "w_up1": w_up[1].astype(BF16),
        "w_down0": w_down[0].astype(BF16), "w_down1": w_down[1].astype(BF16),
    }


def kernel(x_prompt, x_sample, state_mlstm_C, state_mlstm_n, state_mlstm_m, state_pool, cache_latent, cache_rope_k, page_table, norm_mix, norm_ffn, norm_final, w_in0, mlstm_b_i, mlstm_b_f, mlstm_norm, pool_w, pool_scale, w_out0, w_in1, mla_q_norm, mla_kv_norm, w_q_up, w_uk, w_uv, w_out1, w_up, w_down):
    w = _prepare_weights(norm_mix, norm_ffn, norm_final, w_in0, mlstm_b_i, mlstm_b_f, mlstm_norm, pool_w,
                         pool_scale, w_out0, w_in1, mla_q_norm, mla_kv_norm, w_q_up, w_uk, w_uv, w_out1,
                         w_up, w_down)
    bp, t, d = x_prompt.shape
    bs, ts, _ = x_sample.shape
    assert ts == 1
    past_len = page_table.shape[1] * cache_latent.shape[1]

    x1, c_p, n_p, m_p, buf_p = _mixer0_prompt(x_prompt, w, 0)
    x2 = _ffn(x1.reshape(bp * t, d), w["g_ffn0"], w["w_up0"], w["w_down0"], w["g_final"], False)
    x2 = x2.reshape(bp, t, d)
    ckv_p, kpe_p, keys_p, vt_p, qt_p = _mla_proj(x2, w, jnp.arange(t))
    x3 = _mla_attend_prompt(qt_p, keys_p, vt_p, x2, w)
    y_p = _ffn(x3.reshape(bp * t, d), w["g_ffn1"], w["w_up1"], w["w_down1"], w["g_final"], True)
    y_p = y_p.reshape(bp, t, d)

    xs = x_sample.reshape(bs, d)
    xs1, c_s, n_s, m_s, buf_s = _mixer0_sample(xs, state_mlstm_C, state_mlstm_n, state_mlstm_m, state_pool,
                                               w, past_len)
    xs2 = _ffn(xs1, w["g_ffn0"], w["w_up0"], w["w_down0"], w["g_final"], False)
    ckv_s, kpe_s, keys_s, _, qt_s = _mla_proj(xs2.reshape(1, bs, d), w, jnp.full((bs,), past_len))
    assert qt_s.shape == (1, 1, MLA_KEY_WIDTH, MLA_HEADS * bs)
    q_s = jnp.transpose(qt_s.reshape(MLA_KEY_WIDTH, MLA_HEADS, bs), (2, 1, 0))
    rope_t = jnp.transpose(cache_rope_k, (0, 2, 1))
    o_s = _mla_attend_sample(q_s, keys_s.reshape(bs, 1, MLA_KEY_WIDTH), cache_latent, rope_t, page_table)
    xs3 = _attn_out_sample(o_s, xs2, w)
    y_s = _ffn(xs3, w["g_ffn1"], w["w_up1"], w["w_down1"], w["g_final"], True)

    return (y_p, y_s.reshape(bs, 1, d), c_p, n_p, m_p, buf_p, ckv_p, kpe_p,
            c_s, n_s, m_s, buf_s, ckv_s.reshape(bs, 1, MLA_KV_RANK), kpe_s.reshape(bs, 1, MLA_ROPE))
```

```python
import functools
import math

import jax
import jax.numpy as jnp
from jax import lax
from jax.experimental import pallas as pl
from jax.experimental.pallas import tpu as pltpu

F32 = jnp.float32
BF16 = jnp.bfloat16

EPS = 1e-6
ML_HEADS = 4
ML_HEAD_DIM = 128
ML_WIDTH = ML_HEADS * ML_HEAD_DIM
POOL_WINDOWS = (2, 4, 8, 16)
POOL_GROUP = 128
POOL_WIDTH = len(POOL_WINDOWS) * POOL_GROUP
POOL_BUF = max(POOL_WINDOWS) - 1
POOL_HIST = POOL_BUF + 1
MLA_HEADS = 8
MLA_NOPE = 128
MLA_ROPE = 64
MLA_V = 128
MLA_Q_RANK = 512
MLA_KV_RANK = 256
MLA_SCALE = (MLA_NOPE + MLA_ROPE) ** -0.5
MLA_SCALE_LOG2E = MLA_SCALE * math.log2(math.e)
ROPE_THETA = 10000.0
LANES = 128
MLA_KEY_WIDTH = MLA_KV_RANK + LANES
Z_MAIN = 4 * ML_WIDTH + POOL_WIDTH
FLASH_GROUP_LANES = 4096
MLA_TOKEN_TILE = 512

VMEM_LIMIT_BYTES = 48 * 1024 * 1024

_NT = (((1,), (1,)), ((), ()))
_TN = (((0,), (0,)), ((), ()))


def _dot(a, b):
    return jnp.dot(a, b, preferred_element_type=F32)


def _dot_nt(a, b):
    return lax.dot_general(a, b, _NT, preferred_element_type=F32)


def _dot_tn(a, b):
    return lax.dot_general(a, b, _TN, preferred_element_type=F32)


def _rms(x, g):
    return x * lax.rsqrt(jnp.mean(x * x, axis=-1, keepdims=True) + EPS) * g


def _log_sigmoid(x):
    return jnp.minimum(x, 0.0) - jnp.log1p(jnp.exp(-jnp.abs(x)))


def _head_norm(h, g):
    hc = h - jnp.mean(h, axis=-1, keepdims=True)
    return hc * lax.rsqrt(jnp.mean(hc * hc, axis=-1, keepdims=True) + EPS) * g


def _cumsum_lanes(x):
    n = x.shape[-1]
    lane = lax.broadcasted_iota(jnp.int32, x.shape, x.ndim - 1)
    s = 1
    while s < n:
        x = x + jnp.where(lane >= s, pltpu.roll(x, s, axis=x.ndim - 1), 0.0)
        s *= 2
    return x


def _params(*semantics):
    return pltpu.CompilerParams(dimension_semantics=semantics, vmem_limit_bytes=VMEM_LIMIT_BYTES)


def _full(shape):
    return pl.BlockSpec(shape, lambda *_: (0,) * len(shape))


def _in_proj(x, w_in, z_out, gi_out, gf_out):
    g_ref, win_ref, wgi_ref, wgf_ref, bi_ref, bf_ref = w_in
    hn = _rms(x, g_ref[...]).astype(BF16)
    z_out[...] = _dot(hn, win_ref[...])
    gi_out[...] = _dot_nt(wgi_ref[...], hn) + bi_ref[...]
    gf_out[...] = _log_sigmoid(_dot_nt(wgf_ref[...], hn) + bf_ref[...])


def _mixer0_tile(x, cur, nxt, w_in, gn_ref, pw_ref, ps_ref, wout_ref, asm_s, ext_s, c_s, n_s, m_s,
                 *, chunk, pos):
    z_s, gi_s, gf_s = cur
    tt = z_s.shape[0]
    dh = ML_HEAD_DIM
    units = [(c, h) for c in range(tt // chunk) for h in range(ML_HEADS)]
    rows_of = lambda c: slice(c * chunk, (c + 1) * chunk)
    head = lambda part, h: slice(part * ML_WIDTH + h * dh, part * ML_WIDTH + (h + 1) * dh)
    row = lax.broadcasted_iota(jnp.int32, (chunk, chunk), 0)
    col = lax.broadcasted_iota(jnp.int32, (chunk, chunk), 1)
    causal = row >= col
    diag = row == col

    gates = []
    m_prev8 = m_s[:, 0:1]
    for c in range(tt // chunk):
        bcum = _cumsum_lanes(gf_s[:, rows_of(c)])
        a = gi_s[:, rows_of(c)] - bcum
        m_last8 = jnp.maximum(m_prev8, jnp.max(a, axis=1, keepdims=True))
        gates.append((a, bcum, m_prev8, m_last8, jnp.exp(m_prev8 - m_last8)))
        m_prev8 = bcum[:, chunk - 1:chunk] + m_last8
    m_s[...] = jnp.broadcast_to(m_prev8, m_s.shape)

    scores = {}
    for c, h in units:
        qb = z_s[rows_of(c), head(0, h)].astype(BF16)
        kb = (z_s[rows_of(c), head(1, h)] * (dh ** -0.5)).astype(BF16)
        scores[c, h] = _dot_nt(qb, kb)

    _in_proj(nxt[0](), w_in, *nxt[1])

    parts = {}
    for c, h in units:
        a, bcum, m_prev8, m_last8, _ = gates[c]
        a_row = a[h:h + 1, :]
        m_prev = m_prev8[h:h + 1, :]
        a_mat = jnp.where(causal, a_row, -jnp.inf)
        m_col = jnp.maximum(jnp.max(a_mat, axis=1, keepdims=True), m_prev)
        b_col = jnp.sum(jnp.where(diag, bcum[h:h + 1, :], 0.0), axis=1, keepdims=True)
        a_col = jnp.sum(jnp.where(diag, a_row, 0.0), axis=1, keepdims=True)
        s = scores[c, h] * jnp.exp(a_mat - m_col)
        vb = z_s[rows_of(c), head(2, h)].astype(BF16)
        parts[c, h] = (_dot(s.astype(BF16), vb), jnp.sum(s, axis=1, keepdims=True),
                       jnp.exp(m_prev - m_col), jnp.exp(-(b_col + m_col)),
                       jnp.exp(a_col - m_last8[h:h + 1, :]))

    for c, h in units:
        num_intra, den_intra, w_inter, e_neg_m, w_tok = parts[c, h]
        w_old = gates[c][4][h:h + 1, :]
        q = z_s[rows_of(c), head(0, h)]
        k = z_s[rows_of(c), head(1, h)] * (dh ** -0.5)
        vb = z_s[rows_of(c), head(2, h)].astype(BF16)
        c_mat = c_s[h]
        n_row = n_s[h:h + 1, :]
        num = num_intra + w_inter * _dot_nt(q.astype(BF16), c_mat.astype(BF16))
        den = den_intra + w_inter * jnp.sum(q * n_row, axis=1, keepdims=True)
        hh = num / jnp.maximum(jnp.abs(den), e_neg_m)
        kw = k * w_tok
        c_s[h] = w_old * c_mat + _dot_tn(vb, kw.astype(BF16))
        n_s[h:h + 1, :] = w_old * n_row + jnp.sum(kw, axis=0, keepdims=True)
        o_gate = jax.nn.sigmoid(z_s[rows_of(c), head(3, h)])
        hm = _head_norm(hh, gn_ref[:, h * dh:(h + 1) * dh]) * o_gate
        asm_s[rows_of(c), h * dh:(h + 1) * dh] = hm.astype(BF16)

    ext_s[POOL_HIST:POOL_HIST + tt, :] = z_s[:, 4 * ML_WIDTH:Z_MAIN]
    for g, win in enumerate(POOL_WINDOWS):
        lanes = slice(g * POOL_GROUP, (g + 1) * POOL_GROUP)
        u = ext_s[POOL_HIST:POOL_HIST + tt, lanes]
        wsum = u
        for j in range(1, win):
            wsum = wsum + ext_s[POOL_HIST - j:POOL_HIST - j + tt, lanes]
        cnt = jnp.minimum(win, pos + 1).astype(F32)
        pooled = wsum / cnt - u
        og = _dot(pooled.astype(BF16), pw_ref[g]) * ps_ref[:, lanes]
        asm_s[:, ML_WIDTH + g * POOL_GROUP:ML_WIDTH + (g + 1) * POOL_GROUP] = og.astype(BF16)
    ext_s[0:POOL_HIST, :] = ext_s[tt:tt + POOL_HIST, :]

    return x() + _dot(asm_s[...], wout_ref[...])


def _mixer0_kernel(x_ref, xn_ref, g_ref, win_ref, wgi_ref, wgf_ref, bi_ref, bf_ref, gn_ref, pw_ref, ps_ref,
                   wout_ref, y_ref, c_out, n_out, m_out, buf_out,
                   za_s, gia_s, gfa_s, zb_s, gib_s, gfb_s, asm_s, ext_s, c_s, n_s, m_s, *, tt, chunk, pos0):
    b = pl.program_id(0)
    i = pl.program_id(1)
    w_in = (g_ref, win_ref, wgi_ref, wgf_ref, bi_ref, bf_ref)
    set_a = (za_s, gia_s, gfa_s)
    set_b = (zb_s, gib_s, gfb_s)

    @pl.when((b == 0) & (i == 0))
    def _():
        _in_proj(x_ref[0, 0:tt, :], w_in, *set_a)

    @pl.when(i == 0)
    def _():
        c_s[...] = jnp.zeros_like(c_s)
        n_s[...] = jnp.zeros_like(n_s)
        m_s[...] = jnp.zeros_like(m_s)
        ext_s[0:POOL_HIST, :] = jnp.zeros((POOL_HIST, POOL_WIDTH), F32)

    tile = functools.partial(_mixer0_tile, w_in=w_in, gn_ref=gn_ref, pw_ref=pw_ref, ps_ref=ps_ref,
                             wout_ref=wout_ref, asm_s=asm_s, ext_s=ext_s, c_s=c_s, n_s=n_s, m_s=m_s,
                             chunk=chunk)
    pos = pos0 + 2 * i * tt + lax.broadcasted_iota(jnp.int32, (tt, 1), 0)
    x0 = lambda: x_ref[0, 0:tt, :]
    x1 = lambda: x_ref[0, tt:2 * tt, :]
    y_ref[0, 0:tt, :] = tile(x0, set_a, (x1, set_b), pos=pos)
    y_ref[0, tt:2 * tt, :] = tile(x1, set_b, (lambda: xn_ref[0], set_a), pos=pos + tt)

    @pl.when(i == pl.num_programs(1) - 1)
    def _():
        c_out[0] = c_s[...]
        n_out[0] = n_s[0:ML_HEADS, :]
        m_out[0] = m_s[...]
        buf_out[0] = ext_s[0:POOL_HIST, :]


def _mixer0_prompt(x, w, pos0):
    b, t, d = x.shape
    tt = min(256, t // 2)
    chunk = min(128, tt)
    assert t % (2 * tt) == 0 and tt % chunk == 0 and tt >= POOL_HIST
    n_pairs = t // (2 * tt)
    kern = functools.partial(_mixer0_kernel, tt=tt, chunk=chunk, pos0=pos0)

    def next_tile(i, j):
        nxt = jnp.minimum(i * n_pairs + j + 1, b * n_pairs - 1)
        return (nxt // n_pairs, 2 * (nxt % n_pairs), 0)

    out_shape = (
        jax.ShapeDtypeStruct((b, t, d), F32),
        jax.ShapeDtypeStruct((b, ML_HEADS, ML_HEAD_DIM, ML_HEAD_DIM), F32),
        jax.ShapeDtypeStruct((b, ML_HEADS, ML_HEAD_DIM), F32),
        jax.ShapeDtypeStruct((b, 8, LANES), F32),
        jax.ShapeDtypeStruct((b, POOL_HIST, POOL_WIDTH), F32),
    )
    in_specs = [
        pl.BlockSpec((1, 2 * tt, d), lambda i, j: (i, j, 0)),
        pl.BlockSpec((1, tt, d), next_tile),
        _full((1, d)),
        _full((d, Z_MAIN)),
        _full((8, d)),
        _full((8, d)),
        _full((8, 1)),
        _full((8, 1)),
        _full((1, ML_WIDTH)),
        _full((len(POOL_WINDOWS), POOL_GROUP, POOL_GROUP)),
        _full((1, POOL_WIDTH)),
        _full((ML_WIDTH + POOL_WIDTH, d)),
    ]
    out_specs = (
        pl.BlockSpec((1, 2 * tt, d), lambda i, j: (i, j, 0)),
        pl.BlockSpec((1, ML_HEADS, ML_HEAD_DIM, ML_HEAD_DIM), lambda i, j: (i, 0, 0, 0)),
        pl.BlockSpec((1, ML_HEADS, ML_HEAD_DIM), lambda i, j: (i, 0, 0)),
        pl.BlockSpec((1, 8, LANES), lambda i, j: (i, 0, 0)),
        pl.BlockSpec((1, POOL_HIST, POOL_WIDTH), lambda i, j: (i, 0, 0)),
    )
    proj_set = [pltpu.VMEM((tt, Z_MAIN), F32), pltpu.VMEM((8, tt), F32), pltpu.VMEM((8, tt), F32)]
    scratch = proj_set + proj_set + [
        pltpu.VMEM((tt, ML_WIDTH + POOL_WIDTH), BF16),
        pltpu.VMEM((tt + POOL_HIST, POOL_WIDTH), F32),
        pltpu.VMEM((ML_HEADS, ML_HEAD_DIM, ML_HEAD_DIM), F32),
        pltpu.VMEM((8, ML_HEAD_DIM), F32),
        pltpu.VMEM((8, LANES), F32),
    ]
    y, c, n, m, buf = pl.pallas_call(
        kern, grid=(b, n_pairs), in_specs=in_specs, out_specs=out_specs, out_shape=out_shape,
        scratch_shapes=scratch, compiler_params=_params("arbitrary", "arbitrary"), name="mixer0_prompt",
    )(x, x, w["g_mix0"], w["w_in0_main"], w["w_gate_i_t"], w["w_gate_f_t"], w["b_i_col"], w["b_f_col"],
      w["mlstm_norm"], w["pool_w"], w["pool_scale"], w["w_out0"])
    return y, c, n, m[:, :ML_HEADS, 0], buf[:, 1:, :]


def _norm_matmul_kernel(x_ref, g_ref, w_ref, o_ref):
    o_ref[...] = _dot(_rms(x_ref[...], g_ref[...]).astype(BF16), w_ref[...])


def _norm_matmul(x, g, w_bf16):
    m, d = x.shape
    n = w_bf16.shape[1]
    tm = min(128, m)
    assert m % tm == 0
    return pl.pallas_call(
        _norm_matmul_kernel, grid=(m // tm,),
        in_specs=[pl.BlockSpec((tm, d), lambda i: (i, 0)), _full((1, d)), _full((d, n))],
        out_specs=pl.BlockSpec((tm, n), lambda i: (i, 0)),
        out_shape=jax.ShapeDtypeStruct((m, n), F32),
        compiler_params=_params("arbitrary"), name="norm_matmul",
    )(x, g, w_bf16)


def _mixer0_step_kernel(z_ref, x_ref, c0_ref, n0_ref, m0_ref, buf_ref, bi_ref, bf_ref, gn_ref, pw_ref,
                        ps_ref, wout_ref, y_ref, c_ref, n_ref, m_ref, bufo_ref, cq_s, asm_s, *, bb, cnts):
    dh = ML_HEAD_DIM
    gate_i = Z_MAIN
    gate_f = Z_MAIN + LANES
    ig = z_ref[:, gate_i:gate_i + ML_HEADS] + bi_ref[...]
    lf = _log_sigmoid(z_ref[:, gate_f:gate_f + ML_HEADS] + bf_ref[...])
    m0 = m0_ref[...]
    m_t = jnp.maximum(lf + m0, ig)
    w_i = jnp.exp(ig - m_t)
    w_f = jnp.exp(lf + m0 - m_t)
    e_m = jnp.exp(-m_t)
    m_ref[...] = m_t

    row8 = lax.broadcasted_iota(jnp.int32, (8, dh), 0) == 0
    for h in range(ML_HEADS):
        q = z_ref[:, h * dh:(h + 1) * dh]
        k = z_ref[:, ML_WIDTH + h * dh:ML_WIDTH + (h + 1) * dh] * (dh ** -0.5)
        v = z_ref[:, 2 * ML_WIDTH + h * dh:2 * ML_WIDTH + (h + 1) * dh]
        wi = w_i[:, h:h + 1]
        wf = w_f[:, h:h + 1]
        kw = k * wi
        for b in range(bb):
            c_mat = c0_ref[b, h]
            q8 = jnp.broadcast_to(q[b:b + 1, :], (8, dh))
            cq_s[b:b + 1, :] = _dot_nt(q8, c_mat)[0:1, :]
            v8 = jnp.where(row8, v[b:b + 1, :], 0.0)
            k8 = jnp.where(row8, kw[b:b + 1, :], 0.0)
            c_ref[b, h] = wf[b:b + 1, :] * c_mat + _dot_tn(v8, k8)
        n_old = n0_ref[:, h, :]
        s = jnp.sum(q * k, axis=1, keepdims=True) * wi
        num = s * v + wf * cq_s[...]
        den = s + wf * jnp.sum(n_old * q, axis=1, keepdims=True)
        hh = num / jnp.maximum(jnp.abs(den), e_m[:, h:h + 1])
        n_ref[:, h, :] = wf * n_old + kw
        o_gate = jax.nn.sigmoid(z_ref[:, 3 * ML_WIDTH + h * dh:3 * ML_WIDTH + (h + 1) * dh])
        hm = _head_norm(hh, gn_ref[:, h * dh:(h + 1) * dh]) * o_gate
        asm_s[:, h * dh:(h + 1) * dh] = hm

    u_all = z_ref[:, 4 * ML_WIDTH:Z_MAIN]
    for g, win in enumerate(POOL_WINDOWS):
        u = u_all[:, g * POOL_GROUP:(g + 1) * POOL_GROUP]
        wsum = u
        for j in range(1, win):
            off = (POOL_BUF - j) * POOL_WIDTH + g * POOL_GROUP
            wsum = wsum + buf_ref[:, off:off + POOL_GROUP]
        pooled = wsum / cnts[g] - u
        og = _dot(pooled.astype(BF16), pw_ref[g]) * ps_ref[:, g * POOL_GROUP:(g + 1) * POOL_GROUP]
        asm_s[:, ML_WIDTH + g * POOL_GROUP:ML_WIDTH + (g + 1) * POOL_GROUP] = og
    keep = (POOL_BUF - 1) * POOL_WIDTH
    bufo_ref[:, 0:keep] = buf_ref[:, POOL_WIDTH:POOL_WIDTH + keep]
    bufo_ref[:, keep:keep + POOL_WIDTH] = u_all

    y_ref[...] = x_ref[...] + _dot(asm_s[...].astype(BF16), wout_ref[...])


def _mixer0_sample(x, c0, n0, m0, buf0, w, pos0):
    b, d = x.shape
    z = _norm_matmul(x, w["g_mix0"], w["w_in0_gates"])
    zw = z.shape[1]
    bb = 8
    assert b % bb == 0
    cnts = tuple(float(min(win, pos0 + 1)) for win in POOL_WINDOWS)
    bufw = POOL_BUF * POOL_WIDTH
    kern = functools.partial(_mixer0_step_kernel, bb=bb, cnts=cnts)
    hd = (ML_HEADS, ML_HEAD_DIM)
    in_specs = [
        pl.BlockSpec((bb, zw), lambda i: (i, 0)),
        pl.BlockSpec((bb, d), lambda i: (i, 0)),
        pl.BlockSpec((bb,) + hd + (ML_HEAD_DIM,), lambda i: (i, 0, 0, 0)),
        pl.BlockSpec((bb,) + hd, lambda i: (i, 0, 0)),
        pl.BlockSpec((bb, ML_HEADS), lambda i: (i, 0)),
        pl.BlockSpec((bb, bufw), lambda i: (i, 0)),
        _full((1, ML_HEADS)),
        _full((1, ML_HEADS)),
        _full((1, ML_WIDTH)),
        _full((len(POOL_WINDOWS), POOL_GROUP, POOL_GROUP)),
        _full((1, POOL_WIDTH)),
        _full((ML_WIDTH + POOL_WIDTH, d)),
    ]
    out_specs = (
        pl.BlockSpec((bb, d), lambda i: (i, 0)),
        pl.BlockSpec((bb,) + hd + (ML_HEAD_DIM,), lambda i: (i, 0, 0, 0)),
        pl.BlockSpec((bb,) + hd, lambda i: (i, 0, 0)),
        pl.BlockSpec((bb, ML_HEADS), lambda i: (i, 0)),
        pl.BlockSpec((bb, bufw), lambda i: (i, 0)),
    )
    out_shape = (
        jax.ShapeDtypeStruct((b, d), F32),
        jax.ShapeDtypeStruct((b,) + hd + (ML_HEAD_DIM,), F32),
        jax.ShapeDtypeStruct((b,) + hd, F32),
        jax.ShapeDtypeStruct((b, ML_HEADS), F32),
        jax.ShapeDtypeStruct((b, bufw), F32),
    )
    scratch = [pltpu.VMEM((bb, ML_HEAD_DIM), F32), pltpu.VMEM((bb, ML_WIDTH + POOL_WIDTH), F32)]
    y, c, n, m, buf = pl.pallas_call(
        kern, grid=(b // bb,), in_specs=in_specs, out_specs=out_specs, out_shape=out_shape,
        scratch_shapes=scratch, compiler_params=_params("arbitrary"), name="mixer0_sample",
    )(z, x, c0, n0, m0, buf0.reshape(b, bufw), w["b_i_row"], w["b_f_row"], w["mlstm_norm"],
      w["pool_w"], w["pool_scale"], w["w_out0"])
    return y, c, n, m, buf.reshape(b, POOL_BUF, POOL_WIDTH)


def _ffn_kernel(x_ref, g_ref, wu_ref, wd_ref, gf_ref, y_ref, hn_s, acc_s, *, ts, tf, final_norm):
    n_sub = x_ref.shape[0] // ts
    for s in range(n_sub):
        hn_s[s] = _rms(x_ref[s * ts:(s + 1) * ts, :], g_ref[...]).astype(BF16)
    for s in range(n_sub):
        rows = slice(s * ts, (s + 1) * ts)
        for f in range(wu_ref.shape[1] // tf):
            a = jnp.maximum(_dot(hn_s[s], wu_ref[:, f * tf:(f + 1) * tf]), 0.0)
            part = _dot((a * a).astype(BF16), wd_ref[f * tf:(f + 1) * tf, :])
            if f == 0:
                acc_s[s] = part
            else:
                acc_s[s] += part
        y = x_ref[rows, :] + acc_s[s]
        if final_norm:
            y = _rms(y, gf_ref[...])
        y_ref[rows, :] = y


def _ffn(x, g, w_up, w_down, g_final, final_norm):
    m, d = x.shape
    dff = w_up.shape[1]
    n_sub = 2 if m % 16 == 0 else 1
    ts = min(512, m // n_sub)
    tf = min(1024, dff)
    tm = n_sub * ts
    assert m % tm == 0 and dff % tf == 0 and ts % 8 == 0
    kern = functools.partial(_ffn_kernel, ts=ts, tf=tf, final_norm=final_norm)
    resident = pl.Buffered(1)
    return pl.pallas_call(
        kern, grid=(m // tm,),
        in_specs=[
            pl.BlockSpec((tm, d), lambda i: (i, 0)),
            _full((1, d)),
            pl.BlockSpec((d, dff), lambda i: (0, 0), pipeline_mode=resident),
            pl.BlockSpec((dff, d), lambda i: (0, 0), pipeline_mode=resident),
            _full((1, d)),
        ],
        out_specs=pl.BlockSpec((tm, d), lambda i: (i, 0)),
        out_shape=jax.ShapeDtypeStruct((m, d), F32),
        scratch_shapes=[pltpu.VMEM((n_sub, ts, d), BF16), pltpu.VMEM((n_sub, ts, d), F32)],
        compiler_params=_params("arbitrary"), name="ffn",
    )(x, g, w_up, w_down, g_final)


def _mla_proj_kernel(x_ref, g_ref, win_ref, gq_ref, gkv_ref, wqn_ref, wqp_ref, wqs_ref, wuk_ref,
                     cos_ref, sin_ref, ckv_ref, kpe_ref, key_ref, vt_ref, qt_ref):
    r = MLA_KV_RANK
    tm = x_ref.shape[1]
    hn = _rms(x_ref[0], g_ref[...]).astype(BF16)
    z = _dot(hn, win_ref[...])
    cq = _rms(z[:, 0:MLA_Q_RANK], gq_ref[...]).astype(BF16)
    ckv = _rms(z[:, MLA_Q_RANK:MLA_Q_RANK + r], gkv_ref[...])
    cos = cos_ref[...]
    sin = sin_ref[...]
    kpe = z[:, MLA_Q_RANK + r:MLA_Q_RANK + r + LANES] * cos + z[:, MLA_Q_RANK + r + LANES:] * sin
    ckv_ref[0] = ckv
    kpe_ref[0] = kpe[:, 0:MLA_ROPE]
    key_ref[0, :, 0:r] = ckv.astype(BF16)
    key_ref[0, :, r:r + LANES] = kpe.astype(BF16)
    vt_ref[0, 0] = ckv.T.astype(BF16)
    qn = _dot(cq, wqn_ref[...])
    qp = _dot(cq, wqp_ref[...])
    qs = _dot(cq, wqs_ref[...])
    for h in range(MLA_HEADS):
        lanes = slice(h * LANES, (h + 1) * LANES)
        cols = slice(h * tm, (h + 1) * tm)
        q_lat_t = _dot_nt(wuk_ref[h], qn[:, lanes].astype(BF16))
        q_pe = qp[:, lanes] * cos + qs[:, lanes] * sin
        qt_ref[0, 0, 0:r, cols] = q_lat_t.astype(BF16)
        qt_ref[0, 0, r:r + LANES, cols] = q_pe.T.astype(BF16)


def _mla_proj(x, w, pos):
    nb, t, d = x.shape
    tm = min(MLA_TOKEN_TILE, t)
    assert t % tm == 0
    nt = t // tm
    kw = MLA_KEY_WIDTH
    cos, sin = _rope_tables(pos)
    out_shape = (
        jax.ShapeDtypeStruct((nb, t, MLA_KV_RANK), F32),
        jax.ShapeDtypeStruct((nb, t, MLA_ROPE), F32),
        jax.ShapeDtypeStruct((nb, t, kw), BF16),
        jax.ShapeDtypeStruct((nb, nt, MLA_KV_RANK, tm), BF16),
        jax.ShapeDtypeStruct((nb, nt, kw, MLA_HEADS * tm), BF16),
    )
    in_specs = [
        pl.BlockSpec((1, tm, d), lambda i, j: (i, j, 0)),
        _full((1, d)),
        _full(w["w_in1"].shape),
        _full((1, MLA_Q_RANK)),
        _full((1, MLA_KV_RANK)),
        _full(w["w_q_nope"].shape),
        _full(w["w_q_pe"].shape),
        _full(w["w_q_pe_sw"].shape),
        _full(w["w_uk_h"].shape),
        pl.BlockSpec((tm, LANES), lambda i, j: (j, 0)),
        pl.BlockSpec((tm, LANES), lambda i, j: (j, 0)),
    ]
    out_specs = (
        pl.BlockSpec((1, tm, MLA_KV_RANK), lambda i, j: (i, j, 0)),
        pl.BlockSpec((1, tm, MLA_ROPE), lambda i, j: (i, j, 0)),
        pl.BlockSpec((1, tm, kw), lambda i, j: (i, j, 0)),
        pl.BlockSpec((1, 1, MLA_KV_RANK, tm), lambda i, j: (i, j, 0, 0)),
        pl.BlockSpec((1, 1, kw, MLA_HEADS * tm), lambda i, j: (i, j, 0, 0)),
    )
    return pl.pallas_call(
        _mla_proj_kernel, grid=(nb, nt), in_specs=in_specs, out_specs=out_specs, out_shape=out_shape,
        compiler_params=_params("arbitrary", "arbitrary"), name="mla_proj",
    )(x, w["g_mix1"], w["w_in1"], w["mla_q_norm"], w["mla_kv_norm"], w["w_q_nope"], w["w_q_pe"],
      w["w_q_pe_sw"], w["w_uk_h"], cos, sin)


def _flash_kernel(qt_ref, k_ref, vt_ref, x_ref, wuv_ref, wo_ref, y_ref, acc_s, m_s, l_s, asm_s, *, tq):
    qi = pl.program_id(1)
    tk = tq
    rows = MLA_HEADS * tq
    m_s[...] = jnp.full_like(m_s, -jnp.inf)
    l_s[...] = jnp.zeros_like(l_s)
    acc_s[...] = jnp.zeros_like(acc_s)
    gw = min(rows, FLASH_GROUP_LANES)
    assert gw % tq == 0 and rows % gw == 0

    def block(kb, on_diagonal):
        start = pl.multiple_of(kb * tk, tk)
        kblk = k_ref[0, pl.ds(start, tk), :]
        vblk = vt_ref[0, kb]
        n_groups = rows // gw
        scores = None
        for g in range(n_groups + 1):
            prev = scores
            if g < n_groups:
                scores = _dot(kblk, qt_ref[0, 0, :, g * gw:(g + 1) * gw])
            if g == 0:
                continue
            cols = slice((g - 1) * gw, g * gw)
            s = prev
            if on_diagonal:
                key = lax.broadcasted_iota(jnp.int32, (tk, gw), 0)
                tok = lax.broadcasted_iota(jnp.int32, (tk, gw), 1) & (tq - 1)
                s = jnp.where(key <= tok, s, -jnp.inf)
            m_old = m_s[:, cols]
            m_new = jnp.maximum(m_old, jnp.max(s, axis=0, keepdims=True))
            alpha = jnp.exp2((m_old - m_new) * MLA_SCALE_LOG2E)
            p = jnp.exp2((s - m_new) * MLA_SCALE_LOG2E)
            l_s[:, cols] = alpha * l_s[:, cols] + jnp.sum(p, axis=0, keepdims=True)
            acc_s[:, cols] = alpha * acc_s[:, cols] + _dot(vblk, p.astype(BF16))
            m_s[:, cols] = m_new

    def body(kb, carry):
        block(kb, False)
        return carry

    lax.fori_loop(0, qi, body, 0)
    block(qi, True)
    o_t = acc_s[...] / l_s[...]
    for h in range(MLA_HEADS):
        oh = o_t[:, h * tq:(h + 1) * tq].astype(BF16)
        asm_s[h * MLA_V:(h + 1) * MLA_V, :] = _dot(wuv_ref[h], oh).astype(BF16)
    y_ref[0] = x_ref[0] + _dot_tn(asm_s[...], wo_ref[...])


def _mla_attend_prompt(q_t, keys, v_t, x, w):
    b, t, d = x.shape
    nt, tq = v_t.shape[1], v_t.shape[3]
    assert t == nt * tq and tq & (tq - 1) == 0
    rows = MLA_HEADS * tq
    kern = functools.partial(_flash_kernel, tq=tq)
    return pl.pallas_call(
        kern, grid=(b, nt),
        in_specs=[
            pl.BlockSpec((1, 1, MLA_KEY_WIDTH, rows), lambda i, j: (i, j, 0, 0)),
            pl.BlockSpec((1, t, MLA_KEY_WIDTH), lambda i, j: (i, 0, 0)),
            pl.BlockSpec((1, nt, MLA_KV_RANK, tq), lambda i, j: (i, 0, 0, 0)),
            pl.BlockSpec((1, tq, d), lambda i, j: (i, j, 0)),
            _full(w["w_uv_h"].shape),
            _full(w["w_out1"].shape),
        ],
        out_specs=pl.BlockSpec((1, tq, d), lambda i, j: (i, j, 0)),
        out_shape=jax.ShapeDtypeStruct((b, t, d), F32),
        scratch_shapes=[
            pltpu.VMEM((MLA_KV_RANK, rows), F32),
            pltpu.VMEM((1, rows), F32),
            pltpu.VMEM((1, rows), F32),
            pltpu.VMEM((MLA_HEADS * MLA_V, tq), BF16),
        ],
        compiler_params=_params("arbitrary", "arbitrary"), name="mla_flash",
    )(q_t, keys, v_t, x, w["w_uv_h"], w["w_out1"])


def _page_copies(pt_ref, lat_hbm, rope_hbm, lat_buf, rope_buf, sem, seq, slot, start, pages=None):
    def body(p, carry):
        page = pt_ref[seq, p]
        copies = (pltpu.make_async_copy(lat_hbm.at[page], lat_buf.at[slot, p], sem.at[0, slot]),
                  pltpu.make_async_copy(rope_hbm.at[page], rope_buf.at[slot, p], sem.at[1, slot]))
        for c in copies:
            if start:
                c.start(priority=p % 2 if isinstance(p, int) else 0)
            else:
                c.wait()
        return carry

    if pages is None:
        lax.fori_loop(0, lat_buf.shape[1], body, 0)
    else:
        for p in pages:
            body(p, 0)


def _decode_kernel(pt_ref, q_ref, kn_ref, lat_hbm, rope_hbm, o_ref, lat_buf, rope_buf, s_buf, sem, *, group):
    r = MLA_KV_RANK
    n_pages, page = lat_buf.shape[1], lat_buf.shape[2]
    b = pl.program_id(0)
    last = pl.num_programs(0) - 1
    slot = lax.rem(b, 2)
    args = (pt_ref, lat_hbm, rope_hbm, lat_buf, rope_buf, sem)

    @pl.when(b == 0)
    def _():
        _page_copies(*args, 0, 0, True)

    _page_copies(*args, b, slot, False, range(n_pages))
    nxt = jnp.minimum(b + 1, last)

    q = q_ref[0]
    ql = q[:, 0:r]
    qp = q[:, r:r + MLA_ROPE]
    kn = kn_ref[0].astype(F32)
    s_new = jnp.sum(q.astype(F32) * kn, axis=1, keepdims=True)
    lat = lat_buf.at[slot]
    rope = rope_buf.at[slot]
    gk = group * page
    for g in range(n_pages // group):
        lat_g = lat[g * group:(g + 1) * group].reshape(gk, r).astype(BF16)
        s_pe = [_dot(qp, rope[g * group + i].astype(BF16)) for i in range(group)]
        s_buf[:, g * gk:(g + 1) * gk] = _dot_nt(ql, lat_g) + jnp.concatenate(s_pe, axis=1)
        _page_copies(*args, nxt, 1 - slot, True, range(g * group, (g + 1) * group))
    s = s_buf[...]
    m = jnp.maximum(jnp.max(s, axis=1, keepdims=True), s_new)
    p_all = jnp.exp((s - m) * MLA_SCALE)
    p_new = jnp.exp((s_new - m) * MLA_SCALE)
    denom = jnp.sum(p_all, axis=1, keepdims=True) + p_new
    acc = p_new * kn[:, 0:r]
    for g in range(n_pages // group):
        lat_g = lat[g * group:(g + 1) * group].reshape(gk, r).astype(BF16)
        acc = acc + _dot(p_all[:, g * gk:(g + 1) * gk].astype(BF16), lat_g)
    o_ref[0] = acc / denom

    @pl.when(b == last)
    def _():
        _page_copies(*args, last, 1 - slot, False)


def _mla_attend_sample(q, key_new, cache_latent, cache_rope_t, page_table):
    b = q.shape[0]
    n_pages = page_table.shape[1]
    page = cache_latent.shape[1]
    kw = MLA_KEY_WIDTH
    grid_spec = pltpu.PrefetchScalarGridSpec(
        num_scalar_prefetch=1, grid=(b,),
        in_specs=[
            pl.BlockSpec((1, MLA_HEADS, kw), lambda i, pt: (i, 0, 0)),
            pl.BlockSpec((1, 1, kw), lambda i, pt: (i, 0, 0)),
            pl.BlockSpec(memory_space=pl.ANY),
            pl.BlockSpec(memory_space=pl.ANY),
        ],
        out_specs=pl.BlockSpec((1, MLA_HEADS, MLA_KV_RANK), lambda i, pt: (i, 0, 0)),
        scratch_shapes=[
            pltpu.VMEM((2, n_pages, page, MLA_KV_RANK), F32),
            pltpu.VMEM((2, n_pages, MLA_ROPE, page), F32),
            pltpu.VMEM((MLA_HEADS, n_pages * page), F32),
            pltpu.SemaphoreType.DMA((2, 2)),
        ],
    )
    kern = functools.partial(_decode_kernel, group=math.gcd(n_pages, 8))
    return pl.pallas_call(
        kern, grid_spec=grid_spec, out_shape=jax.ShapeDtypeStruct((b, MLA_HEADS, MLA_KV_RANK), F32),
        compiler_params=_params("arbitrary"), name="mla_decode",
    )(page_table, q, key_new, cache_latent, cache_rope_t)


def _attn_out_kernel(o_ref, x_ref, wuv_ref, wo_ref, y_ref, asm_s):
    for h in range(MLA_HEADS):
        oh = o_ref[:, h * MLA_KV_RANK:(h + 1) * MLA_KV_RANK].astype(BF16)
        asm_s[:, h * MLA_V:(h + 1) * MLA_V] = _dot(oh, wuv_ref[h]).astype(BF16)
    y_ref[...] = x_ref[...] + _dot(asm_s[...], wo_ref[...])


def _attn_out_sample(o, x, w):
    m, d = x.shape
    tm = min(128, m)
    assert m % tm == 0
    ow = MLA_HEADS * MLA_KV_RANK
    return pl.pallas_call(
        _attn_out_kernel, grid=(m // tm,),
        in_specs=[pl.BlockSpec((tm, ow), lambda i: (i, 0)), pl.BlockSpec((tm, d), lambda i: (i, 0)),
                  _full(w["w_uv_t"].shape), _full(w["w_out1"].shape)],
        out_specs=pl.BlockSpec((tm, d), lambda i: (i, 0)),
        out_shape=jax.ShapeDtypeStruct((m, d), F32),
        scratch_shapes=[pltpu.VMEM((tm, MLA_HEADS * MLA_V), BF16)],
        compiler_params=_params("arbitrary"), name="attn_out",
    )(o.reshape(m, ow), x, w["w_uv_t"], w["w_out1"])


def _rope_tables(pos):
    half = MLA_ROPE // 2
    inv = ROPE_THETA ** (-jnp.arange(half, dtype=F32) * 2.0 / MLA_ROPE)
    ang = pos.astype(F32)[:, None] * inv[None, :]
    cos = jnp.cos(ang)
    sin = jnp.sin(ang)
    pad = jnp.zeros((pos.shape[0], LANES - MLA_ROPE), F32)
    return jnp.concatenate([cos, cos, pad], axis=1), jnp.concatenate([-sin, sin, pad], axis=1)


def _pad_lanes(a, width):
    return jnp.pad(a, ((0, 0), (0, width - a.shape[1])))


def _prepare_weights(norm_mix, norm_ffn, norm_final, w_in0, mlstm_b_i, mlstm_b_f, mlstm_norm, pool_w,
                     pool_scale, w_out0, w_in1, mla_q_norm, mla_kv_norm, w_q_up, w_uk, w_uv, w_out1,
                     w_up, w_down):
    d = w_in0.shape[0]
    half = MLA_ROPE // 2
    gate_i = w_in0[:, Z_MAIN:Z_MAIN + ML_HEADS]
    gate_f = w_in0[:, Z_MAIN + ML_HEADS:Z_MAIN + 2 * ML_HEADS]
    rope_k = w_in1[:, MLA_Q_RANK + MLA_KV_RANK:]
    rope_k_sw = jnp.concatenate([rope_k[:, half:], rope_k[:, :half]], axis=1)
    q_up = w_q_up.reshape(MLA_Q_RANK, MLA_HEADS, MLA_NOPE + MLA_ROPE)
    q_pe = q_up[:, :, MLA_NOPE:]
    q_pe_sw = jnp.concatenate([q_pe[:, :, half:], q_pe[:, :, :half]], axis=2)
    pad_pe = ((0, 0), (0, 0), (0, LANES - MLA_ROPE))
    col8 = lambda v: jnp.pad(v.astype(F32), (0, 8 - ML_HEADS)).reshape(8, 1)
    return {
        "g_mix0": norm_mix[0].reshape(1, d), "g_mix1": norm_mix[1].reshape(1, d),
        "g_ffn0": norm_ffn[0].reshape(1, d), "g_ffn1": norm_ffn[1].reshape(1, d),
        "g_final": norm_final.reshape(1, d),
        "w_in0_main": w_in0[:, :Z_MAIN].astype(BF16),
        "w_in0_gates": jnp.concatenate(
            [w_in0[:, :Z_MAIN], _pad_lanes(gate_i, LANES), _pad_lanes(gate_f, LANES)], axis=1).astype(BF16),
        "w_gate_i_t": jnp.pad(gate_i.T, ((0, 8 - ML_HEADS), (0, 0))).astype(BF16),
        "w_gate_f_t": jnp.pad(gate_f.T, ((0, 8 - ML_HEADS), (0, 0))).astype(BF16),
        "b_i_col": col8(mlstm_b_i), "b_f_col": col8(mlstm_b_f),
        "b_i_row": mlstm_b_i.reshape(1, ML_HEADS), "b_f_row": mlstm_b_f.reshape(1, ML_HEADS),
        "mlstm_norm": mlstm_norm.reshape(1, ML_WIDTH),
        "pool_w": pool_w.astype(BF16), "pool_scale": pool_scale.reshape(1, POOL_WIDTH),
        "w_out0": w_out0.astype(BF16),
        "w_in1": jnp.concatenate(
            [w_in1[:, :MLA_Q_RANK + MLA_KV_RANK], _pad_lanes(rope_k, LANES), _pad_lanes(rope_k_sw, LANES)],
            axis=1).astype(BF16),
        "mla_q_norm": mla_q_norm.reshape(1, MLA_Q_RANK), "mla_kv_norm": mla_kv_norm.reshape(1, MLA_KV_RANK),
        "w_q_nope": q_up[:, :, :MLA_NOPE].reshape(MLA_Q_RANK, MLA_HEADS * MLA_NOPE).astype(BF16),
        "w_q_pe": jnp.pad(q_pe, pad_pe).reshape(MLA_Q_RANK, MLA_HEADS * LANES).astype(BF16),
        "w_q_pe_sw": jnp.pad(q_pe_sw, pad_pe).reshape(MLA_Q_RANK, MLA_HEADS * LANES).astype(BF16),
        "w_uk_h": jnp.transpose(w_uk, (1, 0, 2)).astype(BF16),
        "w_uv_t": jnp.transpose(w_uv, (1, 0, 2)).astype(BF16),
        "w_uv_h": jnp.transpose(w_uv, (1, 2, 0)).astype(BF16),
        "w_out1": w_out1.astype(BF16),
        "w_up0": w_up[0].astype(BF16), "w_up1": w_up[1].astype(BF16),
        "w_down0": w_down[0].astype(BF16), "w_down1": w_down[1].astype(BF16),
    }


def kernel(x_prompt, x_sample, state_mlstm_C, state_mlstm_n, state_mlstm_m, state_pool, cache_latent, cache_rope_k, page_table, norm_mix, norm_ffn, norm_final, w_in0, mlstm_b_i, mlstm_b_f, mlstm_norm, pool_w, pool_scale, w_out0, w_in1, mla_q_norm, mla_kv_norm, w_q_up, w_uk, w_uv, w_out1, w_up, w_down):
    w = _prepare_weights(norm_mix, norm_ffn, norm_final, w_in0, mlstm_b_i, mlstm_b_f, mlstm_norm, pool_w,
                         pool_scale, w_out0, w_in1, mla_q_norm, mla_kv_norm, w_q_up, w_uk, w_uv, w_out1,
                         w_up, w_down)
    bp, t, d = x_prompt.shape
    bs, ts, _ = x_sample.shape
    assert ts == 1
    past_len = page_table.shape[1] * cache_latent.shape[1]

    x1, c_p, n_p, m_p, buf_p = _mixer0_prompt(x_prompt, w, 0)
    x2 = _ffn(x1.reshape(bp * t, d), w["g_ffn0"], w["w_up0"], w["w_down0"], w["g_final"], False)
    x2 = x2.reshape(bp, t, d)
    ckv_p, kpe_p, keys_p, vt_p, qt_p = _mla_proj(x2, w, jnp.arange(t))
    x3 = _mla_attend_prompt(qt_p, keys_p, vt_p, x2, w)
    y_p = _ffn(x3.reshape(bp * t, d), w["g_ffn1"], w["w_up1"], w["w_down1"], w["g_final"], True)
    y_p = y_p.reshape(bp, t, d)

    xs = x_sample.reshape(bs, d)
    xs1, c_s, n_s, m_s, buf_s = _mixer0_sample(xs, state_mlstm_C, state_mlstm_n, state_mlstm_m, state_pool,
                                               w, past_len)
    xs2 = _ffn(xs1, w["g_ffn0"], w["w_up0"], w["w_down0"], w["g_final"], False)
    ckv_s, kpe_s, keys_s, _, qt_s = _mla_proj(xs2.reshape(1, bs, d), w, jnp.full((bs,), past_len))
    assert qt_s.shape == (1, 1, MLA_KEY_WIDTH, MLA_HEADS * bs)
    q_s = jnp.transpose(qt_s.reshape(MLA_KEY_WIDTH, MLA_HEADS, bs), (2, 1, 0))
    rope_t = jnp.transpose(cache_rope_k, (0, 2, 1))
    o_s = _mla_attend_sample(q_s, keys_s.reshape(bs, 1, MLA_KEY_WIDTH), cache_latent, rope_t, page_table)
    xs3 = _attn_out_sample(o_s, xs2, w)
    y_s = _ffn(xs3, w["g_ffn1"], w["w_up1"], w["w_down1"], w["g_final"], True)

    return (y_p, y_s.reshape(bs, 1, d), c_p, n_p, m_p, buf_p, ckv_p, kpe_p,
            c_s, n_s, m_s, buf_s, ckv_s.reshape(bs, 1, MLA_KV_RANK), kpe_s.reshape(bs, 1, MLA_ROPE))
```

```python
import functools
import math

import jax
import jax.numpy as jnp
from jax import lax
from jax.experimental import pallas as pl
from jax.experimental.pallas import tpu as pltpu

F32 = jnp.float32
BF16 = jnp.bfloat16

EPS = 1e-6
ML_HEADS = 4
ML_HEAD_DIM = 128
ML_WIDTH = ML_HEADS * ML_HEAD_DIM
POOL_WINDOWS = (2, 4, 8, 16)
POOL_GROUP = 128
POOL_WIDTH = len(POOL_WINDOWS) * POOL_GROUP
POOL_BUF = max(POOL_WINDOWS) - 1
POOL_HIST = POOL_BUF + 1
POOL_KEEP = 2 * POOL_HIST
MLA_HEADS = 8
MLA_NOPE = 128
MLA_ROPE = 64
MLA_V = 128
MLA_Q_RANK = 512
MLA_KV_RANK = 256
MLA_SCALE = (MLA_NOPE + MLA_ROPE) ** -0.5
MLA_SCALE_LOG2E = MLA_SCALE * math.log2(math.e)
ROPE_THETA = 10000.0
LANES = 128
MLA_KEY_WIDTH = MLA_KV_RANK + LANES
Z_MAIN = 4 * ML_WIDTH + POOL_WIDTH
MLA_TOKEN_TILE = 512

VMEM_LIMIT_BYTES = 48 * 1024 * 1024

_NT = (((1,), (1,)), ((), ()))
_TN = (((0,), (0,)), ((), ()))


def _dot(a, b):
    return jnp.dot(a, b, preferred_element_type=F32)


def _dot_nt(a, b):
    return lax.dot_general(a, b, _NT, preferred_element_type=F32)


def _dot_tn(a, b):
    return lax.dot_general(a, b, _TN, preferred_element_type=F32)


def _rms(x, g):
    return x * lax.rsqrt(jnp.mean(x * x, axis=-1, keepdims=True) + EPS) * g


def _log_sigmoid(x):
    return jnp.minimum(x, 0.0) - jnp.log1p(jnp.exp(-jnp.abs(x)))


def _head_norm(h, g):
    hc = h - jnp.mean(h, axis=-1, keepdims=True)
    return hc * lax.rsqrt(jnp.mean(hc * hc, axis=-1, keepdims=True) + EPS) * g


def _cumsum_lanes(x):
    n = x.shape[-1]
    lane = lax.broadcasted_iota(jnp.int32, x.shape, x.ndim - 1)
    s = 1
    while s < n:
        x = x + jnp.where(lane >= s, pltpu.roll(x, s, axis=x.ndim - 1), 0.0)
        s *= 2
    return x


def _params(*semantics):
    return pltpu.CompilerParams(dimension_semantics=semantics, vmem_limit_bytes=VMEM_LIMIT_BYTES)


def _full(shape):
    return pl.BlockSpec(shape, lambda *_: (0,) * len(shape))


def _in_proj(x, w_in, z_out, qkv_out, gi_out, gf_out):
    g_ref, win_ref, wgi_ref, wgf_ref, bi_ref, bf_ref = w_in
    hn = _rms(x(), g_ref[...]).astype(BF16)
    z = _dot(hn, win_ref[...])
    z_out[...] = z
    qkv_out[:, 0:ML_WIDTH] = z[:, 0:ML_WIDTH].astype(BF16)
    qkv_out[:, ML_WIDTH:2 * ML_WIDTH] = (z[:, ML_WIDTH:2 * ML_WIDTH] * (ML_HEAD_DIM ** -0.5)).astype(BF16)
    qkv_out[:, 2 * ML_WIDTH:3 * ML_WIDTH] = z[:, 2 * ML_WIDTH:3 * ML_WIDTH].astype(BF16)
    gi_out[...] = _dot_nt(wgi_ref[...], hn) + bi_ref[...]
    gf_out[...] = _log_sigmoid(_dot_nt(wgf_ref[...], hn) + bf_ref[...])


def _mixer0_tile(x, cur, nxt, w_in, gn_ref, pw_ref, ps_ref, wout_ref, asm_s, ext_s, lvl_s, c_s, n_s, m_s,
                 *, chunk, pos):
    z_s, qkv_s, gi_s, gf_s = cur
    tt = z_s.shape[0]
    dh = ML_HEAD_DIM
    units = [(c, h) for c in range(tt // chunk) for h in range(ML_HEADS)]
    rows_of = lambda c: slice(c * chunk, (c + 1) * chunk)
    head = lambda part, h: slice(part * ML_WIDTH + h * dh, part * ML_WIDTH + (h + 1) * dh)
    row = lax.broadcasted_iota(jnp.int32, (chunk, chunk), 0)
    col = lax.broadcasted_iota(jnp.int32, (chunk, chunk), 1)
    causal = row >= col
    diag = row == col

    gates = []
    m_prev8 = m_s[:, 0:1]
    for c in range(tt // chunk):
        bcum = _cumsum_lanes(gf_s[:, rows_of(c)])
        a = gi_s[:, rows_of(c)] - bcum
        m_last8 = jnp.maximum(m_prev8, jnp.max(a, axis=1, keepdims=True))
        gates.append((a, bcum, m_prev8, m_last8, jnp.exp(m_prev8 - m_last8)))
        m_prev8 = bcum[:, chunk - 1:chunk] + m_last8
    m_s[...] = jnp.broadcast_to(m_prev8, m_s.shape)

    scores = {}
    for c, h in units:
        scores[c, h] = _dot_nt(qkv_s[rows_of(c), head(0, h)], qkv_s[rows_of(c), head(1, h)])

    _in_proj(nxt[0], w_in, *nxt[1])

    parts = {}
    for c, h in units:
        a, bcum, m_prev8, m_last8, _ = gates[c]
        a_row = a[h:h + 1, :]
        m_prev = m_prev8[h:h + 1, :]
        a_mat = jnp.where(causal, a_row, -jnp.inf)
        m_col = jnp.maximum(jnp.max(a_mat, axis=1, keepdims=True), m_prev)
        b_col = jnp.sum(jnp.where(diag, bcum[h:h + 1, :], 0.0), axis=1, keepdims=True)
        a_col = jnp.sum(jnp.where(diag, a_row, 0.0), axis=1, keepdims=True)
        s = scores[c, h] * jnp.exp(a_mat - m_col)
        vb = qkv_s[rows_of(c), head(2, h)]
        kw = z_s[rows_of(c), head(1, h)] * (dh ** -0.5) * jnp.exp(a_col - m_last8[h:h + 1, :])
        parts[c, h] = (_dot(s.astype(BF16), vb), jnp.sum(s, axis=1, keepdims=True),
                       jnp.exp(m_prev - m_col), jnp.exp(-(b_col + m_col)),
                       _dot_tn(vb, kw.astype(BF16)), jnp.sum(kw, axis=0, keepdims=True))

    for c, h in units:
        num_intra, den_intra, w_inter, e_neg_m, c_add, n_add = parts[c, h]
        w_old = gates[c][4][h:h + 1, :]
        q = z_s[rows_of(c), head(0, h)]
        c_mat = c_s[h]
        n_row = n_s[h:h + 1, :]
        num = num_intra + w_inter * _dot_nt(qkv_s[rows_of(c), head(0, h)], c_mat.astype(BF16))
        den = den_intra + w_inter * jnp.sum(q * n_row, axis=1, keepdims=True)
        hh = num / jnp.maximum(jnp.abs(den), e_neg_m)
        c_s[h] = w_old * c_mat + c_add
        n_s[h:h + 1, :] = w_old * n_row + n_add
        o_gate = jax.nn.sigmoid(z_s[rows_of(c), head(3, h)])
        hm = _head_norm(hh, gn_ref[:, h * dh:(h + 1) * dh]) * o_gate
        asm_s[rows_of(c), h * dh:(h + 1) * dh] = hm.astype(BF16)

    keep = POOL_KEEP
    end = keep + tt
    ext_s[keep:end, :] = z_s[:, 4 * ML_WIDTH:Z_MAIN]
    for g, win in enumerate(POOL_WINDOWS):
        lanes = slice(g * POOL_GROUP, (g + 1) * POOL_GROUP)
        src, src_lanes = ext_s, lanes
        width, lo, level = 1, 0, 0
        while 2 * width < win:
            lo += 8
            dst = lvl_s.at[level % 2]
            dst[lo:end, :] = src[lo:end, src_lanes] + src[lo - width:end - width, src_lanes]
            src, src_lanes = dst, slice(0, POOL_GROUP)
            width, level = 2 * width, level + 1
        wsum = src[keep:end, src_lanes] + src[keep - width:end - width, src_lanes]
        u = ext_s[keep:end, lanes]
        cnt = jnp.minimum(win, pos + 1).astype(F32)
        pooled = wsum / cnt - u
        og = _dot(pooled.astype(BF16), pw_ref[g]) * ps_ref[:, lanes]
        asm_s[:, ML_WIDTH + g * POOL_GROUP:ML_WIDTH + (g + 1) * POOL_GROUP] = og.astype(BF16)
    ext_s[0:keep, :] = ext_s[tt:end, :]

    return x() + _dot(asm_s[...], wout_ref[...])


def _mixer0_kernel(x_ref, xn_ref, g_ref, win_ref, wgi_ref, wgf_ref, bi_ref, bf_ref, gn_ref, pw_ref, ps_ref,
                   wout_ref, y_ref, c_out, n_out, m_out, buf_out,
                   za_s, qa_s, gia_s, gfa_s, zb_s, qb_s, gib_s, gfb_s, asm_s, ext_s, lvl_s, c_s, n_s, m_s,
                   *, tt, chunk, pos0):
    b = pl.program_id(0)
    i = pl.program_id(1)
    w_in = (g_ref, win_ref, wgi_ref, wgf_ref, bi_ref, bf_ref)
    set_a = (za_s, qa_s, gia_s, gfa_s)
    set_b = (zb_s, qb_s, gib_s, gfb_s)

    @pl.when((b == 0) & (i == 0))
    def _():
        _in_proj(lambda: x_ref[0, 0:tt, :], w_in, *set_a)

    @pl.when(i == 0)
    def _():
        c_s[...] = jnp.zeros_like(c_s)
        n_s[...] = jnp.zeros_like(n_s)
        m_s[...] = jnp.zeros_like(m_s)
        ext_s[0:POOL_KEEP, :] = jnp.zeros((POOL_KEEP, POOL_WIDTH), F32)

    tile = functools.partial(_mixer0_tile, w_in=w_in, gn_ref=gn_ref, pw_ref=pw_ref, ps_ref=ps_ref,
                             wout_ref=wout_ref, asm_s=asm_s, ext_s=ext_s, lvl_s=lvl_s, c_s=c_s, n_s=n_s,
                             m_s=m_s, chunk=chunk)
    pos = pos0 + 2 * i * tt + lax.broadcasted_iota(jnp.int32, (tt, 1), 0)
    x0 = lambda: x_ref[0, 0:tt, :]
    x1 = lambda: x_ref[0, tt:2 * tt, :]
    y_ref[0, 0:tt, :] = tile(x0, set_a, (x1, set_b), pos=pos)
    y_ref[0, tt:2 * tt, :] = tile(x1, set_b, (lambda: xn_ref[0], set_a), pos=pos + tt)

    @pl.when(i == pl.num_programs(1) - 1)
    def _():
        c_out[0] = c_s[...]
        n_out[0] = n_s[0:ML_HEADS, :]
        m_out[0] = m_s[...]
        buf_out[0] = ext_s[POOL_KEEP - POOL_HIST:POOL_KEEP, :]


def _mixer0_prompt(x, w, pos0):
    b, t, d = x.shape
    tt = min(256, t // 2)
    chunk = min(256, tt)
    assert t % (2 * tt) == 0 and tt % chunk == 0 and tt >= POOL_KEEP
    n_pairs = t // (2 * tt)
    kern = functools.partial(_mixer0_kernel, tt=tt, chunk=chunk, pos0=pos0)

    def next_tile(i, j):
        nxt = jnp.minimum(i * n_pairs + j + 1, b * n_pairs - 1)
        return (nxt // n_pairs, 2 * (nxt % n_pairs), 0)

    out_shape = (
        jax.ShapeDtypeStruct((b, t, d), F32),
        jax.ShapeDtypeStruct((b, ML_HEADS, ML_HEAD_DIM, ML_HEAD_DIM), F32),
        jax.ShapeDtypeStruct((b, ML_HEADS, ML_HEAD_DIM), F32),
        jax.ShapeDtypeStruct((b, 8, LANES), F32),
        jax.ShapeDtypeStruct((b, POOL_HIST, POOL_WIDTH), F32),
    )
    in_specs = [
        pl.BlockSpec((1, 2 * tt, d), lambda i, j: (i, j, 0)),
        pl.BlockSpec((1, tt, d), next_tile),
        _full((1, d)),
        _full((d, Z_MAIN)),
        _full((8, d)),
        _full((8, d)),
        _full((8, 1)),
        _full((8, 1)),
        _full((1, ML_WIDTH)),
        _full((len(POOL_WINDOWS), POOL_GROUP, POOL_GROUP)),
        _full((1, POOL_WIDTH)),
        _full((ML_WIDTH + POOL_WIDTH, d)),
    ]
    out_specs = (
        pl.BlockSpec((1, 2 * tt, d), lambda i, j: (i, j, 0)),
        pl.BlockSpec((1, ML_HEADS, ML_HEAD_DIM, ML_HEAD_DIM), lambda i, j: (i, 0, 0, 0)),
        pl.BlockSpec((1, ML_HEADS, ML_HEAD_DIM), lambda i, j: (i, 0, 0)),
        pl.BlockSpec((1, 8, LANES), lambda i, j: (i, 0, 0)),
        pl.BlockSpec((1, POOL_HIST, POOL_WIDTH), lambda i, j: (i, 0, 0)),
    )
    proj_set = [pltpu.VMEM((tt, Z_MAIN), F32), pltpu.VMEM((tt, 3 * ML_WIDTH), BF16),
                pltpu.VMEM((8, tt), F32), pltpu.VMEM((8, tt), F32)]
    scratch = proj_set + proj_set + [
        pltpu.VMEM((tt, ML_WIDTH + POOL_WIDTH), BF16),
        pltpu.VMEM((tt + POOL_KEEP, POOL_WIDTH), F32),
        pltpu.VMEM((2, tt + POOL_KEEP, POOL_GROUP), F32),
        pltpu.VMEM((ML_HEADS, ML_HEAD_DIM, ML_HEAD_DIM), F32),
        pltpu.VMEM((8, ML_HEAD_DIM), F32),
        pltpu.VMEM((8, LANES), F32),
    ]
    y, c, n, m, buf = pl.pallas_call(
        kern, grid=(b, n_pairs), in_specs=in_specs, out_specs=out_specs, out_shape=out_shape,
        scratch_shapes=scratch, compiler_params=_params("arbitrary", "arbitrary"), name="mixer0_prompt",
    )(x, x, w["g_mix0"], w["w_in0_gates"], w["w_gate_i_t"], w["w_gate_f_t"], w["b_i_col"], w["b_f_col"],
      w["mlstm_norm"], w["pool_w"], w["pool_scale"], w["w_out0"])
    return y, c, n, m[:, :ML_HEADS, 0], buf[:, 1:, :]


def _norm_matmul_kernel(x_ref, g_ref, w_ref, o_ref):
    o_ref[...] = _dot(_rms(x_ref[...], g_ref[...]).astype(BF16), w_ref[...])


def _norm_matmul(x, g, w_bf16):
    m, d = x.shape
    n = w_bf16.shape[1]
    tm = min(128, m)
    assert m % tm == 0
    return pl.pallas_call(
        _norm_matmul_kernel, grid=(m // tm,),
        in_specs=[pl.BlockSpec((tm, d), lambda i: (i, 0)), _full((1, d)), _full((d, n))],
        out_specs=pl.BlockSpec((tm, n), lambda i: (i, 0)),
        out_shape=jax.ShapeDtypeStruct((m, n), F32),
        compiler_params=_params("arbitrary"), name="norm_matmul",
    )(x, g, w_bf16)


def _mixer0_step_kernel(z_ref, x_ref, c0_ref, n0_ref, m0_ref, buf_ref, bi_ref, bf_ref, gn_ref, pw_ref,
                        ps_ref, wout_ref, y_ref, c_ref, n_ref, m_ref, bufo_ref, cq_s, asm_s, *, bb, cnts):
    dh = ML_HEAD_DIM
    gate_i = Z_MAIN
    gate_f = Z_MAIN + LANES
    ig = z_ref[:, gate_i:gate_i + ML_HEADS] + bi_ref[...]
    lf = _log_sigmoid(z_ref[:, gate_f:gate_f + ML_HEADS] + bf_ref[...])
    m0 = m0_ref[...]
    m_t = jnp.maximum(lf + m0, ig)
    w_i = jnp.exp(ig - m_t)
    w_f = jnp.exp(lf + m0 - m_t)
    e_m = jnp.exp(-m_t)
    m_ref[...] = m_t

    row8 = lax.broadcasted_iota(jnp.int32, (8, dh), 0) == 0
    for h in range(ML_HEADS):
        q = z_ref[:, h * dh:(h + 1) * dh]
        k = z_ref[:, ML_WIDTH + h * dh:ML_WIDTH + (h + 1) * dh] * (dh ** -0.5)
        v = z_ref[:, 2 * ML_WIDTH + h * dh:2 * ML_WIDTH + (h + 1) * dh]
        wi = w_i[:, h:h + 1]
        wf = w_f[:, h:h + 1]
        kw = k * wi
        for b in range(bb):
            c_mat = c0_ref[b, h]
            q8 = jnp.broadcast_to(q[b:b + 1, :], (8, dh))
            cq_s[b:b + 1, :] = _dot_nt(q8, c_mat)[0:1, :]
            v8 = jnp.where(row8, v[b:b + 1, :], 0.0)
            k8 = jnp.where(row8, kw[b:b + 1, :], 0.0)
            c_ref[b, h] = wf[b:b + 1, :] * c_mat + _dot_tn(v8, k8)
        n_old = n0_ref[:, h, :]
        s = jnp.sum(q * k, axis=1, keepdims=True) * wi
        num = s * v + wf * cq_s[...]
        den = s + wf * jnp.sum(n_old * q, axis=1, keepdims=True)
        hh = num / jnp.maximum(jnp.abs(den), e_m[:, h:h + 1])
        n_ref[:, h, :] = wf * n_old + kw
        o_gate = jax.nn.sigmoid(z_ref[:, 3 * ML_WIDTH + h * dh:3 * ML_WIDTH + (h + 1) * dh])
        hm = _head_norm(hh, gn_ref[:, h * dh:(h + 1) * dh]) * o_gate
        asm_s[:, h * dh:(h + 1) * dh] = hm

    u_all = z_ref[:, 4 * ML_WIDTH:Z_MAIN]
    for g, win in enumerate(POOL_WINDOWS):
        u = u_all[:, g * POOL_GROUP:(g + 1) * POOL_GROUP]
        wsum = u
        for j in range(1, win):
            off = (POOL_BUF - j) * POOL_WIDTH + g * POOL_GROUP
            wsum = wsum + buf_ref[:, off:off + POOL_GROUP]
        pooled = wsum / cnts[g] - u
        og = _dot(pooled.astype(BF16), pw_ref[g]) * ps_ref[:, g * POOL_GROUP:(g + 1) * POOL_GROUP]
        asm_s[:, ML_WIDTH + g * POOL_GROUP:ML_WIDTH + (g + 1) * POOL_GROUP] = og
    keep = (POOL_BUF - 1) * POOL_WIDTH
    bufo_ref[:, 0:keep] = buf_ref[:, POOL_WIDTH:POOL_WIDTH + keep]
    bufo_ref[:, keep:keep + POOL_WIDTH] = u_all

    y_ref[...] = x_ref[...] + _dot(asm_s[...].astype(BF16), wout_ref[...])


def _mixer0_sample(x, c0, n0, m0, buf0, w, pos0):
    b, d = x.shape
    z = _norm_matmul(x, w["g_mix0"], w["w_in0_gates"])
    zw = z.shape[1]
    bb = 8
    assert b % bb == 0
    cnts = tuple(float(min(win, pos0 + 1)) for win in POOL_WINDOWS)
    bufw = POOL_BUF * POOL_WIDTH
    kern = functools.partial(_mixer0_step_kernel, bb=bb, cnts=cnts)
    hd = (ML_HEADS, ML_HEAD_DIM)
    in_specs = [
        pl.BlockSpec((bb, zw), lambda i: (i, 0)),
        pl.BlockSpec((bb, d), lambda i: (i, 0)),
        pl.BlockSpec((bb,) + hd + (ML_HEAD_DIM,), lambda i: (i, 0, 0, 0)),
        pl.BlockSpec((bb,) + hd, lambda i: (i, 0, 0)),
        pl.BlockSpec((bb, ML_HEADS), lambda i: (i, 0)),
        pl.BlockSpec((bb, bufw), lambda i: (i, 0)),
        _full((1, ML_HEADS)),
        _full((1, ML_HEADS)),
        _full((1, ML_WIDTH)),
        _full((len(POOL_WINDOWS), POOL_GROUP, POOL_GROUP)),
        _full((1, POOL_WIDTH)),
        _full((ML_WIDTH + POOL_WIDTH, d)),
    ]
    out_specs = (
        pl.BlockSpec((bb, d), lambda i: (i, 0)),
        pl.BlockSpec((bb,) + hd + (ML_HEAD_DIM,), lambda i: (i, 0, 0, 0)),
        pl.BlockSpec((bb,) + hd, lambda i: (i, 0, 0)),
        pl.BlockSpec((bb, ML_HEADS), lambda i: (i, 0)),
        pl.BlockSpec((bb, bufw), lambda i: (i, 0)),
    )
    out_shape = (
        jax.ShapeDtypeStruct((b, d), F32),
        jax.ShapeDtypeStruct((b,) + hd + (ML_HEAD_DIM,), F32),
        jax.ShapeDtypeStruct((b,) + hd, F32),
        jax.ShapeDtypeStruct((b, ML_HEADS), F32),
        jax.ShapeDtypeStruct((b, bufw), F32),
    )
    scratch = [pltpu.VMEM((bb, ML_HEAD_DIM), F32), pltpu.VMEM((bb, ML_WIDTH + POOL_WIDTH), F32)]
    y, c, n, m, buf = pl.pallas_call(
        kern, grid=(b // bb,), in_specs=in_specs, out_specs=out_specs, out_shape=out_shape,
        scratch_shapes=scratch, compiler_params=_params("arbitrary"), name="mixer0_sample",
    )(z, x, c0, n0, m0, buf0.reshape(b, bufw), w["b_i_row"], w["b_f_row"], w["mlstm_norm"],
      w["pool_w"], w["pool_scale"], w["w_out0"])
    return y, c, n, m, buf.reshape(b, POOL_BUF, POOL_WIDTH)


def _ffn_kernel(x_ref, g_ref, wu_ref, wd_ref, gf_ref, y_ref, hn_s, acc_s, *, ts, tf, final_norm):
    n_sub = x_ref.shape[0] // ts
    for s in range(n_sub):
        hn_s[s] = _rms(x_ref[s * ts:(s + 1) * ts, :], g_ref[...]).astype(BF16)
    for s in range(n_sub):
        rows = slice(s * ts, (s + 1) * ts)
        for f in range(wu_ref.shape[1] // tf):
            a = jnp.maximum(_dot(hn_s[s], wu_ref[:, f * tf:(f + 1) * tf]), 0.0)
            part = _dot((a * a).astype(BF16), wd_ref[f * tf:(f + 1) * tf, :])
            if f == 0:
                acc_s[s] = part
            else:
                acc_s[s] += part
        y = x_ref[rows, :] + acc_s[s]
        if final_norm:
            y = _rms(y, gf_ref[...])
        y_ref[rows, :] = y


def _ffn(x, g, w_up, w_down, g_final, final_norm):
    m, d = x.shape
    dff = w_up.shape[1]
    n_sub = 2 if m % 16 == 0 else 1
    ts = min(512, m // n_sub)
    tf = min(1024, dff)
    tm = n_sub * ts
    assert m % tm == 0 and dff % tf == 0 and ts % 8 == 0
    kern = functools.partial(_ffn_kernel, ts=ts, tf=tf, final_norm=final_norm)
    resident = pl.Buffered(1)
    return pl.pallas_call(
        kern, grid=(m // tm,),
        in_specs=[
            pl.BlockSpec((tm, d), lambda i: (i, 0)),
            _full((1, d)),
            pl.BlockSpec((d, dff), lambda i: (0, 0), pipeline_mode=resident),
            pl.BlockSpec((dff, d), lambda i: (0, 0), pipeline_mode=resident),
            _full((1, d)),
        ],
        out_specs=pl.BlockSpec((tm, d), lambda i: (i, 0)),
        out_shape=jax.ShapeDtypeStruct((m, d), F32),
        scratch_shapes=[pltpu.VMEM((n_sub, ts, d), BF16), pltpu.VMEM((n_sub, ts, d), F32)],
        compiler_params=_params("arbitrary"), name="ffn",
    )(x, g, w_up, w_down, g_final)


def _query_parts(tokens):
    return 2 if tokens % (2 * LANES) == 0 else 1


def _mla_proj_kernel(x_ref, g_ref, win_ref, gq_ref, gkv_ref, wqn_ref, wqp_ref, wqs_ref, wuk_ref,
                     cos_ref, sin_ref, ckv_ref, kpe_ref, key_ref, vt_ref, qt_ref):
    r = MLA_KV_RANK
    tm = x_ref.shape[1]
    hn = _rms(x_ref[0], g_ref[...]).astype(BF16)
    z = _dot(hn, win_ref[...])
    cq = _rms(z[:, 0:MLA_Q_RANK], gq_ref[...]).astype(BF16)
    ckv = _rms(z[:, MLA_Q_RANK:MLA_Q_RANK + r], gkv_ref[...])
    cos = cos_ref[...]
    sin = sin_ref[...]
    kpe = z[:, MLA_Q_RANK + r:MLA_Q_RANK + r + LANES] * cos + z[:, MLA_Q_RANK + r + LANES:] * sin
    ckv_ref[0] = ckv
    kpe_ref[0] = kpe[:, 0:MLA_ROPE]
    key_ref[0, :, 0:r] = ckv.astype(BF16)
    key_ref[0, :, r:r + LANES] = kpe.astype(BF16)
    vt_ref[0, 0] = ckv.T.astype(BF16)
    qn = _dot(cq, wqn_ref[...])
    qp = _dot(cq, wqp_ref[...])
    qs = _dot(cq, wqs_ref[...])
    n_parts = _query_parts(tm)
    pt = tm // n_parts
    for h in range(MLA_HEADS):
        lanes = slice(h * LANES, (h + 1) * LANES)
        q_lat_t = _dot_nt(wuk_ref[h], qn[:, lanes].astype(BF16)).astype(BF16)
        q_pe_t = (qp[:, lanes] * cos + qs[:, lanes] * sin).T.astype(BF16)
        for part in range(n_parts):
            cols = slice((part * MLA_HEADS + h) * pt, (part * MLA_HEADS + h + 1) * pt)
            qt_ref[0, 0, 0:r, cols] = q_lat_t[:, part * pt:(part + 1) * pt]
            qt_ref[0, 0, r:r + LANES, cols] = q_pe_t[:, part * pt:(part + 1) * pt]


def _mla_proj(x, w, pos):
    nb, t, d = x.shape
    tm = min(MLA_TOKEN_TILE, t)
    assert t % tm == 0
    nt = t // tm
    kw = MLA_KEY_WIDTH
    cos, sin = _rope_tables(pos)
    out_shape = (
        jax.ShapeDtypeStruct((nb, t, MLA_KV_RANK), F32),
        jax.ShapeDtypeStruct((nb, t, MLA_ROPE), F32),
        jax.ShapeDtypeStruct((nb, t, kw), BF16),
        jax.ShapeDtypeStruct((nb, nt, MLA_KV_RANK, tm), BF16),
        jax.ShapeDtypeStruct((nb, nt, kw, MLA_HEADS * tm), BF16),
    )
    in_specs = [
        pl.BlockSpec((1, tm, d), lambda i, j: (i, j, 0)),
        _full((1, d)),
        _full(w["w_in1"].shape),
        _full((1, MLA_Q_RANK)),
        _full((1, MLA_KV_RANK)),
        _full(w["w_q_nope"].shape),
        _full(w["w_q_pe"].shape),
        _full(w["w_q_pe_sw"].shape),
        _full(w["w_uk_h"].shape),
        pl.BlockSpec((tm, LANES), lambda i, j: (j, 0)),
        pl.BlockSpec((tm, LANES), lambda i, j: (j, 0)),
    ]
    out_specs = (
        pl.BlockSpec((1, tm, MLA_KV_RANK), lambda i, j: (i, j, 0)),
        pl.BlockSpec((1, tm, MLA_ROPE), lambda i, j: (i, j, 0)),
        pl.BlockSpec((1, tm, kw), lambda i, j: (i, j, 0)),
        pl.BlockSpec((1, 1, MLA_KV_RANK, tm), lambda i, j: (i, j, 0, 0)),
        pl.BlockSpec((1, 1, kw, MLA_HEADS * tm), lambda i, j: (i, j, 0, 0)),
    )
    return pl.pallas_call(
        _mla_proj_kernel, grid=(nb, nt), in_specs=in_specs, out_specs=out_specs, out_shape=out_shape,
        compiler_params=_params("arbitrary", "arbitrary"), name="mla_proj",
    )(x, w["g_mix1"], w["w_in1"], w["mla_q_norm"], w["mla_kv_norm"], w["w_q_nope"], w["w_q_pe"],
      w["w_q_pe_sw"], w["w_uk_h"], cos, sin)


def _flash_kernel(qt_ref, k_ref, vt_ref, x_ref, wuv_ref, wo_ref, y_ref, acc_s, m_s, l_s, asm_s, *, tq):
    qi = pl.program_id(1)
    tk = tq
    rows = MLA_HEADS * tq
    n_parts = _query_parts(tq)
    pt = tq // n_parts
    m_s[...] = jnp.full_like(m_s, -jnp.inf)
    l_s[...] = jnp.zeros_like(l_s)
    acc_s[...] = jnp.zeros_like(acc_s)

    def block(kb, key_lo, n_keys, first_part, masked):
        start = pl.multiple_of(kb * tk, tk)
        cols = slice(first_part * MLA_HEADS * pt, rows)
        n_cols = rows - cols.start
        s = _dot(k_ref[0, pl.ds(start + key_lo, n_keys), :], qt_ref[0, 0, :, cols])
        if masked:
            key = key_lo + lax.broadcasted_iota(jnp.int32, (n_keys, n_cols), 0)
            col = cols.start + lax.broadcasted_iota(jnp.int32, (n_keys, n_cols), 1)
            part_of_col = lax.shift_right_logical(col, (MLA_HEADS * pt).bit_length() - 1)
            tok = part_of_col * pt + (col & (pt - 1))
            s = jnp.where(key <= tok, s, -jnp.inf)
        m_old = m_s[:, cols]
        m_new = jnp.maximum(m_old, jnp.max(s, axis=0, keepdims=True))
        alpha = jnp.exp2((m_old - m_new) * MLA_SCALE_LOG2E)
        p = jnp.exp2((s - m_new) * MLA_SCALE_LOG2E)
        l_s[:, cols] = alpha * l_s[:, cols] + jnp.sum(p, axis=0, keepdims=True)
        acc_s[:, cols] = alpha * acc_s[:, cols] + _dot(vt_ref[0, kb][:, key_lo:key_lo + n_keys], p.astype(BF16))
        m_s[:, cols] = m_new

    def body(kb, carry):
        block(kb, 0, tk, 0, False)
        return carry

    lax.fori_loop(0, qi, body, 0)
    for part in range(n_parts):
        block(qi, part * pt, pt, part, True)
    o_t = acc_s[...] / l_s[...]
    for h in range(MLA_HEADS):
        for part in range(n_parts):
            c0 = (part * MLA_HEADS + h) * pt
            oh = o_t[:, c0:c0 + pt].astype(BF16)
            asm_s[h * MLA_V:(h + 1) * MLA_V, part * pt:(part + 1) * pt] = _dot(wuv_ref[h], oh).astype(BF16)
    y_ref[0] = x_ref[0] + _dot_tn(asm_s[...], wo_ref[...])


def _mla_attend_prompt(q_t, keys, v_t, x, w):
    b, t, d = x.shape
    nt, tq = v_t.shape[1], v_t.shape[3]
    assert t == nt * tq and tq & (tq - 1) == 0
    rows = MLA_HEADS * tq
    kern = functools.partial(_flash_kernel, tq=tq)
    return pl.pallas_call(
        kern, grid=(b, nt),
        in_specs=[
            pl.BlockSpec((1, 1, MLA_KEY_WIDTH, rows), lambda i, j: (i, j, 0, 0)),
            pl.BlockSpec((1, t, MLA_KEY_WIDTH), lambda i, j: (i, 0, 0)),
            pl.BlockSpec((1, nt, MLA_KV_RANK, tq), lambda i, j: (i, 0, 0, 0)),
            pl.BlockSpec((1, tq, d), lambda i, j: (i, j, 0)),
            _full(w["w_uv_h"].shape),
            _full(w["w_out1"].shape),
        ],
        out_specs=pl.BlockSpec((1, tq, d), lambda i, j: (i, j, 0)),
        out_shape=jax.ShapeDtypeStruct((b, t, d), F32),
        scratch_shapes=[
            pltpu.VMEM((MLA_KV_RANK, rows), F32),
            pltpu.VMEM((1, rows), F32),
            pltpu.VMEM((1, rows), F32),
            pltpu.VMEM((MLA_HEADS * MLA_V, tq), BF16),
        ],
        compiler_params=_params("arbitrary", "arbitrary"), name="mla_flash",
    )(q_t, keys, v_t, x, w["w_uv_h"], w["w_out1"])


def _page_copies(pt_ref, lat_hbm, rope_hbm, lat_buf, rope_buf, sem, seq, slot, start, pages=None):
    def body(p, carry):
        page = pt_ref[seq, p]
        copies = (pltpu.make_async_copy(lat_hbm.at[page], lat_buf.at[slot, p], sem.at[0, slot]),
                  pltpu.make_async_copy(rope_hbm.at[page], rope_buf.at[slot, p], sem.at[1, slot]))
        for c in copies:
            if start:
                c.start(priority=p % 2 if isinstance(p, int) else 0)
            else:
                c.wait()
        return carry

    if pages is None:
        lax.fori_loop(0, lat_buf.shape[1], body, 0)
    else:
        for p in pages:
            body(p, 0)


def _decode_kernel(pt_ref, q_ref, kn_ref, lat_hbm, rope_hbm, o_ref, lat_buf, rope_buf, s_buf, sem, *, group):
    r = MLA_KV_RANK
    n_pages, page = lat_buf.shape[1], lat_buf.shape[2]
    b = pl.program_id(0)
    last = pl.num_programs(0) - 1
    slot = lax.rem(b, 2)
    args = (pt_ref, lat_hbm, rope_hbm, lat_buf, rope_buf, sem)

    @pl.when(b == 0)
    def _():
        _page_copies(*args, 0, 0, True)

    _page_copies(*args, b, slot, False, range(n_pages))
    nxt = jnp.minimum(b + 1, last)

    q = q_ref[0]
    ql = q[:, 0:r]
    qp = q[:, r:r + MLA_ROPE]
    kn = kn_ref[0].astype(F32)
    s_new = jnp.sum(q.astype(F32) * kn, axis=1, keepdims=True)
    lat = lat_buf.at[slot]
    rope = rope_buf.at[slot]
    gk = group * page
    for g in range(n_pages // group):
        lat_g = lat[g * group:(g + 1) * group].reshape(gk, r).astype(BF16)
        s_pe = [_dot(qp, rope[g * group + i].astype(BF16)) for i in range(group)]
        s_buf[:, g * gk:(g + 1) * gk] = _dot_nt(ql, lat_g) + jnp.concatenate(s_pe, axis=1)
        _page_copies(*args, nxt, 1 - slot, True, range(g * group, (g + 1) * group))
    s = s_buf[...]
    m = jnp.maximum(jnp.max(s, axis=1, keepdims=True), s_new)
    p_all = jnp.exp((s - m) * MLA_SCALE)
    p_new = jnp.exp((s_new - m) * MLA_SCALE)
    denom = jnp.sum(p_all, axis=1, keepdims=True) + p_new
    acc = p_new * kn[:, 0:r]
    for g in range(n_pages // group):
        lat_g = lat[g * group:(g + 1) * group].reshape(gk, r).astype(BF16)
        acc = acc + _dot(p_all[:, g * gk:(g + 1) * gk].astype(BF16), lat_g)
    o_ref[0] = acc / denom

    @pl.when(b == last)
    def _():
        _page_copies(*args, last, 1 - slot, False)


def _mla_attend_sample(q, key_new, cache_latent, cache_rope_t, page_table):
    b = q.shape[0]
    n_pages = page_table.shape[1]
    page = cache_latent.shape[1]
    kw = MLA_KEY_WIDTH
    grid_spec = pltpu.PrefetchScalarGridSpec(
        num_scalar_prefetch=1, grid=(b,),
        in_specs=[
            pl.BlockSpec((1, MLA_HEADS, kw), lambda i, pt: (i, 0, 0)),
            pl.BlockSpec((1, 1, kw), lambda i, pt: (i, 0, 0)),
            pl.BlockSpec(memory_space=pl.ANY),
            pl.BlockSpec(memory_space=pl.ANY),
        ],
        out_specs=pl.BlockSpec((1, MLA_HEADS, MLA_KV_RANK), lambda i, pt: (i, 0, 0)),
        scratch_shapes=[
            pltpu.VMEM((2, n_pages, page, MLA_KV_RANK), F32),
            pltpu.VMEM((2, n_pages, MLA_ROPE, page), F32),
            pltpu.VMEM((MLA_HEADS, n_pages * page), F32),
            pltpu.SemaphoreType.DMA((2, 2)),
        ],
    )
    kern = functools.partial(_decode_kernel, group=math.gcd(n_pages, 8))
    return pl.pallas_call(
        kern, grid_spec=grid_spec, out_shape=jax.ShapeDtypeStruct((b, MLA_HEADS, MLA_KV_RANK), F32),
        compiler_params=_params("arbitrary"), name="mla_decode",
    )(page_table, q, key_new, cache_latent, cache_rope_t)


def _attn_out_kernel(o_ref, x_ref, wuv_ref, wo_ref, y_ref, asm_s):
    for h in range(MLA_HEADS):
        oh = o_ref[:, h * MLA_KV_RANK:(h + 1) * MLA_KV_RANK].astype(BF16)
        asm_s[:, h * MLA_V:(h + 1) * MLA_V] = _dot(oh, wuv_ref[h]).astype(BF16)
    y_ref[...] = x_ref[...] + _dot(asm_s[...], wo_ref[...])


def _attn_out_sample(o, x, w):
    m, d = x.shape
    tm = min(128, m)
    assert m % tm == 0
    ow = MLA_HEADS * MLA_KV_RANK
    return pl.pallas_call(
        _attn_out_kernel, grid=(m // tm,),
        in_specs=[pl.BlockSpec((tm, ow), lambda i: (i, 0)), pl.BlockSpec((tm, d), lambda i: (i, 0)),
                  _full(w["w_uv_t"].shape), _full(w["w_out1"].shape)],
        out_specs=pl.BlockSpec((tm, d), lambda i: (i, 0)),
        out_shape=jax.ShapeDtypeStruct((m, d), F32),
        scratch_shapes=[pltpu.VMEM((tm, MLA_HEADS * MLA_V), BF16)],
        compiler_params=_params("arbitrary"), name="attn_out",
    )(o.reshape(m, ow), x, w["w_uv_t"], w["w_out1"])


def _rope_tables(pos):
    half = MLA_ROPE // 2
    inv = ROPE_THETA ** (-jnp.arange(half, dtype=F32) * 2.0 / MLA_ROPE)
    ang = pos.astype(F32)[:, None] * inv[None, :]
    cos = jnp.cos(ang)
    sin = jnp.sin(ang)
    pad = jnp.zeros((pos.shape[0], LANES - MLA_ROPE), F32)
    return jnp.concatenate([cos, cos, pad], axis=1), jnp.concatenate([-sin, sin, pad], axis=1)


def _pad_lanes(a, width):
    return jnp.pad(a, ((0, 0), (0, width - a.shape[1])))


def _prepare_weights(norm_mix, norm_ffn, norm_final, w_in0, mlstm_b_i, mlstm_b_f, mlstm_norm, pool_w,
                     pool_scale, w_out0, w_in1, mla_q_norm, mla_kv_norm, w_q_up, w_uk, w_uv, w_out1,
                     w_up, w_down):
    d = w_in0.shape[0]
    half = MLA_ROPE // 2
    gate_i = w_in0[:, Z_MAIN:Z_MAIN + ML_HEADS]
    gate_f = w_in0[:, Z_MAIN + ML_HEADS:Z_MAIN + 2 * ML_HEADS]
    rope_k = w_in1[:, MLA_Q_RANK + MLA_KV_RANK:]
    rope_k_sw = jnp.concatenate([rope_k[:, half:], rope_k[:, :half]], axis=1)
    q_up = w_q_up.reshape(MLA_Q_RANK, MLA_HEADS, MLA_NOPE + MLA_ROPE)
    q_pe = q_up[:, :, MLA_NOPE:]
    q_pe_sw = jnp.concatenate([q_pe[:, :, half:], q_pe[:, :, :half]], axis=2)
    pad_pe = ((0, 0), (0, 0), (0, LANES - MLA_ROPE))
    col8 = lambda v: jnp.pad(v.astype(F32), (0, 8 - ML_HEADS)).reshape(8, 1)
    return {
        "g_mix0": norm_mix[0].reshape(1, d), "g_mix1": norm_mix[1].reshape(1, d),
        "g_ffn0": norm_ffn[0].reshape(1, d), "g_ffn1": norm_ffn[1].reshape(1, d),
        "g_final": norm_final.reshape(1, d),
        "w_in0_gates": jnp.concatenate(
            [w_in0[:, :Z_MAIN], _pad_lanes(gate_i, LANES), _pad_lanes(gate_f, LANES)], axis=1).astype(BF16),
        "w_gate_i_t": jnp.pad(gate_i.T, ((0, 8 - ML_HEADS), (0, 0))).astype(BF16),
        "w_gate_f_t": jnp.pad(gate_f.T, ((0, 8 - ML_HEADS), (0, 0))).astype(BF16),
        "b_i_col": col8(mlstm_b_i), "b_f_col": col8(mlstm_b_f),
        "b_i_row": mlstm_b_i.reshape(1, ML_HEADS), "b_f_row": mlstm_b_f.reshape(1, ML_HEADS),
        "mlstm_norm": mlstm_norm.reshape(1, ML_WIDTH),
        "pool_w": pool_w.astype(BF16), "pool_scale": pool_scale.reshape(1, POOL_WIDTH),
        "w_out0": w_out0.astype(BF16),
        "w_in1": jnp.concatenate(
            [w_in1[:, :MLA_Q_RANK + MLA_KV_RANK], _pad_lanes(rope_k, LANES), _pad_lanes(rope_k_sw, LANES)],
            axis=1).astype(BF16),
        "mla_q_norm": mla_q_norm.reshape(1, MLA_Q_RANK), "mla_kv_norm": mla_kv_norm.reshape(1, MLA_KV_RANK),
        "w_q_nope": q_up[:, :, :MLA_NOPE].reshape(MLA_Q_RANK, MLA_HEADS * MLA_NOPE).astype(BF16),
        "w_q_pe": jnp.pad(q_pe, pad_pe).reshape(MLA_Q_RANK, MLA_HEADS * LANES).astype(BF16),
        "w_q_pe_sw": jnp.pad(q_pe_sw, pad_pe).reshape(MLA_Q_RANK, MLA_HEADS * LANES).astype(BF16),
        "w_uk_h": jnp.transpose(w_uk, (1, 0, 2)).astype(BF16),
        "w_uv_t": jnp.transpose(w_uv, (1, 0, 2)).astype(BF16),
        "w_uv_h": jnp.transpose(w_uv, (1, 2, 0)).astype(BF16),
        "w_out1": w_out1.astype(BF16),
        "w_up0": w_up[0].astype(BF16), "w_up1": w_up[1].astype(BF16),
        "w_down0": w_down[0].astype(BF16), "w_down1": w_down[1].astype(BF16),
    }


def kernel(x_prompt, x_sample, state_mlstm_C, state_mlstm_n, state_mlstm_m, state_pool, cache_latent, cache_rope_k, page_table, norm_mix, norm_ffn, norm_final, w_in0, mlstm_b_i, mlstm_b_f, mlstm_norm, pool_w, pool_scale, w_out0, w_in1, mla_q_norm, mla_kv_norm, w_q_up, w_uk, w_uv, w_out1, w_up, w_down):
    w = _prepare_weights(norm_mix, norm_ffn, norm_final, w_in0, mlstm_b_i, mlstm_b_f, mlstm_norm, pool_w,
                         pool_scale, w_out0, w_in1, mla_q_norm, mla_kv_norm, w_q_up, w_uk, w_uv, w_out1,
                         w_up, w_down)
    bp, t, d = x_prompt.shape
    bs, ts, _ = x_sample.shape
    assert ts == 1
    past_len = page_table.shape[1] * cache_latent.shape[1]

    x1, c_p, n_p, m_p, buf_p = _mixer0_prompt(x_prompt, w, 0)
    x2 = _ffn(x1.reshape(bp * t, d), w["g_ffn0"], w["w_up0"], w["w_down0"], w["g_final"], False)
    x2 = x2.reshape(bp, t, d)
    ckv_p, kpe_p, keys_p, vt_p, qt_p = _mla_proj(x2, w, jnp.arange(t))
    x3 = _mla_attend_prompt(qt_p, keys_p, vt_p, x2, w)
    y_p = _ffn(x3.reshape(bp * t, d), w["g_ffn1"], w["w_up1"], w["w_down1"], w["g_final"], True)
    y_p = y_p.reshape(bp, t, d)

    xs = x_sample.reshape(bs, d)
    xs1, c_s, n_s, m_s, buf_s = _mixer0_sample(xs, state_mlstm_C, state_mlstm_n, state_mlstm_m, state_pool,
                                               w, past_len)
    xs2 = _ffn(xs1, w["g_ffn0"], w["w_up0"], w["w_down0"], w["g_final"], False)
    ckv_s, kpe_s, keys_s, _, qt_s = _mla_proj(xs2.reshape(1, bs, d), w, jnp.full((bs,), past_len))
    assert qt_s.shape == (1, 1, MLA_KEY_WIDTH, MLA_HEADS * bs)
    q_s = jnp.transpose(qt_s.reshape(MLA_KEY_WIDTH, MLA_HEADS, bs), (2, 1, 0))
    rope_t = jnp.transpose(cache_rope_k, (0, 2, 1))
    o_s = _mla_attend_sample(q_s, keys_s.reshape(bs, 1, MLA_KEY_WIDTH), cache_latent, rope_t, page_table)
    xs3 = _attn_out_sample(o_s, xs2, w)
    y_s = _ffn(xs3, w["g_ffn1"], w["w_up1"], w["w_down1"], w["g_final"], True)

    return (y_p, y_s.reshape(bs, 1, d), c_p, n_p, m_p, buf_p, ckv_p, kpe_p,
            c_s, n_s, m_s, buf_s, ckv_s.reshape(bs, 1, MLA_KV_RANK), kpe_s.reshape(bs, 1, MLA_ROPE))
```

```python
import functools
import math

import jax
import jax.numpy as jnp
from jax import lax
from jax.experimental import pallas as pl
from jax.experimental.pallas import tpu as pltpu

F32 = jnp.float32
BF16 = jnp.bfloat16

EPS = 1e-6
ML_HEADS = 4
ML_HEAD_DIM = 128
ML_WIDTH = ML_HEADS * ML_HEAD_DIM
POOL_WINDOWS = (2, 4, 8, 16)
POOL_GROUP = 128
POOL_WIDTH = len(POOL_WINDOWS) * POOL_GROUP
POOL_BUF = max(POOL_WINDOWS) - 1
POOL_HIST = POOL_BUF + 1
POOL_KEEP = 2 * POOL_HIST
MLA_HEADS = 8
MLA_NOPE = 128
MLA_ROPE = 64
MLA_V = 128
MLA_Q_RANK = 512
MLA_KV_RANK = 256
MLA_SCALE = (MLA_NOPE + MLA_ROPE) ** -0.5
MLA_SCALE_LOG2E = MLA_SCALE * math.log2(math.e)
ROPE_THETA = 10000.0
LANES = 128
MLA_KEY_WIDTH = MLA_KV_RANK + LANES
Z_MAIN = 4 * ML_WIDTH + POOL_WIDTH
MLA_TOKEN_TILE = 512
DECODE_SLOTS = 3

VMEM_LIMIT_BYTES = 48 * 1024 * 1024

_NT = (((1,), (1,)), ((), ()))
_TN = (((0,), (0,)), ((), ()))


def _dot(a, b):
    return jnp.dot(a, b, preferred_element_type=F32)


def _dot_nt(a, b):
    return lax.dot_general(a, b, _NT, preferred_element_type=F32)


def _dot_tn(a, b):
    return lax.dot_general(a, b, _TN, preferred_element_type=F32)


def _rms(x, g):
    return x * lax.rsqrt(jnp.mean(x * x, axis=-1, keepdims=True) + EPS) * g


def _log_sigmoid(x):
    return jnp.minimum(x, 0.0) - jnp.log1p(jnp.exp(-jnp.abs(x)))


def _head_norm(h, g):
    hc = h - jnp.mean(h, axis=-1, keepdims=True)
    return hc * lax.rsqrt(jnp.mean(hc * hc, axis=-1, keepdims=True) + EPS) * g


def _cumsum_lanes(x):
    n = x.shape[-1]
    lane = lax.broadcasted_iota(jnp.int32, x.shape, x.ndim - 1)
    s = 1
    while s < n:
        x = x + jnp.where(lane >= s, pltpu.roll(x, s, axis=x.ndim - 1), 0.0)
        s *= 2
    return x


def _params(*semantics):
    return pltpu.CompilerParams(dimension_semantics=semantics, vmem_limit_bytes=VMEM_LIMIT_BYTES)


def _full(shape):
    return pl.BlockSpec(shape, lambda *_: (0,) * len(shape))


def _in_proj(x, w_in, z_out, qkv_out, gi_out, gf_out):
    g_ref, win_ref, wgi_ref, wgf_ref, bi_ref, bf_ref = w_in
    hn = _rms(x(), g_ref[...]).astype(BF16)
    z = _dot(hn, win_ref[...])
    z_out[...] = z
    qkv_out[:, 0:ML_WIDTH] = z[:, 0:ML_WIDTH].astype(BF16)
    qkv_out[:, ML_WIDTH:2 * ML_WIDTH] = (z[:, ML_WIDTH:2 * ML_WIDTH] * (ML_HEAD_DIM ** -0.5)).astype(BF16)
    qkv_out[:, 2 * ML_WIDTH:3 * ML_WIDTH] = z[:, 2 * ML_WIDTH:3 * ML_WIDTH].astype(BF16)
    gi_out[...] = _dot_nt(wgi_ref[...], hn) + bi_ref[...]
    gf_out[...] = _log_sigmoid(_dot_nt(wgf_ref[...], hn) + bf_ref[...])


def _mixer0_tile(x, cur, nxt, w_in, gn_ref, pw_ref, ps_ref, wout_ref, asm_s, ext_s, lvl_s, c_s, n_s, m_s,
                 *, chunk, pos):
    z_s, qkv_s, gi_s, gf_s = cur
    tt = z_s.shape[0]
    dh = ML_HEAD_DIM
    units = [(c, h) for c in range(tt // chunk) for h in range(ML_HEADS)]
    rows_of = lambda c: slice(c * chunk, (c + 1) * chunk)
    head = lambda part, h: slice(part * ML_WIDTH + h * dh, part * ML_WIDTH + (h + 1) * dh)
    row = lax.broadcasted_iota(jnp.int32, (chunk, chunk), 0)
    col = lax.broadcasted_iota(jnp.int32, (chunk, chunk), 1)
    causal = row >= col
    diag = row == col

    gates = []
    m_prev8 = m_s[:, 0:1]
    for c in range(tt // chunk):
        bcum = _cumsum_lanes(gf_s[:, rows_of(c)])
        a = gi_s[:, rows_of(c)] - bcum
        m_last8 = jnp.maximum(m_prev8, jnp.max(a, axis=1, keepdims=True))
        gates.append((a, bcum, m_prev8, m_last8, jnp.exp(m_prev8 - m_last8)))
        m_prev8 = bcum[:, chunk - 1:chunk] + m_last8
    m_s[...] = jnp.broadcast_to(m_prev8, m_s.shape)

    scores = {}
    for c, h in units:
        scores[c, h] = _dot_nt(qkv_s[rows_of(c), head(0, h)], qkv_s[rows_of(c), head(1, h)])

    _in_proj(nxt[0], w_in, *nxt[1])

    parts = {}
    for c, h in units:
        a, bcum, m_prev8, m_last8, _ = gates[c]
        a_row = a[h:h + 1, :]
        m_prev = m_prev8[h:h + 1, :]
        a_mat = jnp.where(causal, a_row, -jnp.inf)
        m_col = jnp.maximum(jnp.max(a_mat, axis=1, keepdims=True), m_prev)
        b_col = jnp.sum(jnp.where(diag, bcum[h:h + 1, :], 0.0), axis=1, keepdims=True)
        a_col = jnp.sum(jnp.where(diag, a_row, 0.0), axis=1, keepdims=True)
        s = scores[c, h] * jnp.exp(a_mat - m_col)
        vb = qkv_s[rows_of(c), head(2, h)]
        kw = z_s[rows_of(c), head(1, h)] * (dh ** -0.5) * jnp.exp(a_col - m_last8[h:h + 1, :])
        parts[c, h] = (_dot(s.astype(BF16), vb), jnp.sum(s, axis=1, keepdims=True),
                       jnp.exp(m_prev - m_col), jnp.exp(-(b_col + m_col)),
                       _dot_tn(vb, kw.astype(BF16)), jnp.sum(kw, axis=0, keepdims=True))

    for c, h in units:
        num_intra, den_intra, w_inter, e_neg_m, c_add, n_add = parts[c, h]
        w_old = gates[c][4][h:h + 1, :]
        q = z_s[rows_of(c), head(0, h)]
        c_mat = c_s[h]
        n_row = n_s[h:h + 1, :]
        num = num_intra + w_inter * _dot_nt(qkv_s[rows_of(c), head(0, h)], c_mat.astype(BF16))
        den = den_intra + w_inter * jnp.sum(q * n_row, axis=1, keepdims=True)
        hh = num / jnp.maximum(jnp.abs(den), e_neg_m)
        c_s[h] = w_old * c_mat + c_add
        n_s[h:h + 1, :] = w_old * n_row + n_add
        o_gate = jax.nn.sigmoid(z_s[rows_of(c), head(3, h)])
        hm = _head_norm(hh, gn_ref[:, h * dh:(h + 1) * dh]) * o_gate
        asm_s[rows_of(c), h * dh:(h + 1) * dh] = hm.astype(BF16)

    keep = POOL_KEEP
    end = keep + tt
    ext_s[keep:end, :] = z_s[:, 4 * ML_WIDTH:Z_MAIN]
    for g, win in enumerate(POOL_WINDOWS):
        lanes = slice(g * POOL_GROUP, (g + 1) * POOL_GROUP)
        src, src_lanes = ext_s, lanes
        width, lo, level = 1, 0, 0
        while 2 * width < win:
            lo += 8
            dst = lvl_s.at[level % 2]
            dst[lo:end, :] = src[lo:end, src_lanes] + src[lo - width:end - width, src_lanes]
            src, src_lanes = dst, slice(0, POOL_GROUP)
            width, level = 2 * width, level + 1
        wsum = src[keep:end, src_lanes] + src[keep - width:end - width, src_lanes]
        u = ext_s[keep:end, lanes]
        cnt = jnp.minimum(win, pos + 1).astype(F32)
        pooled = wsum / cnt - u
        og = _dot(pooled.astype(BF16), pw_ref[g]) * ps_ref[:, lanes]
        asm_s[:, ML_WIDTH + g * POOL_GROUP:ML_WIDTH + (g + 1) * POOL_GROUP] = og.astype(BF16)
    ext_s[0:keep, :] = ext_s[tt:end, :]

    return x() + _dot(asm_s[...], wout_ref[...])


def _mixer0_kernel(x_ref, xn_ref, g_ref, win_ref, wgi_ref, wgf_ref, bi_ref, bf_ref, gn_ref, pw_ref, ps_ref,
                   wout_ref, y_ref, c_out, n_out, m_out, buf_out,
                   za_s, qa_s, gia_s, gfa_s, zb_s, qb_s, gib_s, gfb_s, asm_s, ext_s, lvl_s, c_s, n_s, m_s,
                   *, tt, chunk, pos0):
    b = pl.program_id(0)
    i = pl.program_id(1)
    w_in = (g_ref, win_ref, wgi_ref, wgf_ref, bi_ref, bf_ref)
    set_a = (za_s, qa_s, gia_s, gfa_s)
    set_b = (zb_s, qb_s, gib_s, gfb_s)

    @pl.when((b == 0) & (i == 0))
    def _():
        _in_proj(lambda: x_ref[0, 0:tt, :], w_in, *set_a)

    @pl.when(i == 0)
    def _():
        c_s[...] = jnp.zeros_like(c_s)
        n_s[...] = jnp.zeros_like(n_s)
        m_s[...] = jnp.zeros_like(m_s)
        ext_s[0:POOL_KEEP, :] = jnp.zeros((POOL_KEEP, POOL_WIDTH), F32)

    tile = functools.partial(_mixer0_tile, w_in=w_in, gn_ref=gn_ref, pw_ref=pw_ref, ps_ref=ps_ref,
                             wout_ref=wout_ref, asm_s=asm_s, ext_s=ext_s, lvl_s=lvl_s, c_s=c_s, n_s=n_s,
                             m_s=m_s, chunk=chunk)
    pos = pos0 + 2 * i * tt + lax.broadcasted_iota(jnp.int32, (tt, 1), 0)
    x0 = lambda: x_ref[0, 0:tt, :]
    x1 = lambda: x_ref[0, tt:2 * tt, :]
    y_ref[0, 0:tt, :] = tile(x0, set_a, (x1, set_b), pos=pos)
    y_ref[0, tt:2 * tt, :] = tile(x1, set_b, (lambda: xn_ref[0], set_a), pos=pos + tt)

    @pl.when(i == pl.num_programs(1) - 1)
    def _():
        c_out[0] = c_s[...]
        n_out[0] = n_s[0:ML_HEADS, :]
        m_out[0] = m_s[...]
        buf_out[0] = ext_s[POOL_KEEP - POOL_HIST:POOL_KEEP, :]


def _mixer0_prompt(x, w, pos0):
    b, t, d = x.shape
    tt = min(256, t // 2)
    chunk = min(256, tt)
    assert t % (2 * tt) == 0 and tt % chunk == 0 and tt >= POOL_KEEP
    n_pairs = t // (2 * tt)
    kern = functools.partial(_mixer0_kernel, tt=tt, chunk=chunk, pos0=pos0)

    def next_tile(i, j):
        nxt = jnp.minimum(i * n_pairs + j + 1, b * n_pairs - 1)
        return (nxt // n_pairs, 2 * (nxt % n_pairs), 0)

    out_shape = (
        jax.ShapeDtypeStruct((b, t, d), F32),
        jax.ShapeDtypeStruct((b, ML_HEADS, ML_HEAD_DIM, ML_HEAD_DIM), F32),
        jax.ShapeDtypeStruct((b, ML_HEADS, ML_HEAD_DIM), F32),
        jax.ShapeDtypeStruct((b, 8, LANES), F32),
        jax.ShapeDtypeStruct((b, POOL_HIST, POOL_WIDTH), F32),
    )
    in_specs = [
        pl.BlockSpec((1, 2 * tt, d), lambda i, j: (i, j, 0)),
        pl.BlockSpec((1, tt, d), next_tile),
        _full((1, d)),
        _full((d, Z_MAIN)),
        _full((8, d)),
        _full((8, d)),
        _full((8, 1)),
        _full((8, 1)),
        _full((1, ML_WIDTH)),
        _full((len(POOL_WINDOWS), POOL_GROUP, POOL_GROUP)),
        _full((1, POOL_WIDTH)),
        _full((ML_WIDTH + POOL_WIDTH, d)),
    ]
    out_specs = (
        pl.BlockSpec((1, 2 * tt, d), lambda i, j: (i, j, 0)),
        pl.BlockSpec((1, ML_HEADS, ML_HEAD_DIM, ML_HEAD_DIM), lambda i, j: (i, 0, 0, 0)),
        pl.BlockSpec((1, ML_HEADS, ML_HEAD_DIM), lambda i, j: (i, 0, 0)),
        pl.BlockSpec((1, 8, LANES), lambda i, j: (i, 0, 0)),
        pl.BlockSpec((1, POOL_HIST, POOL_WIDTH), lambda i, j: (i, 0, 0)),
    )
    proj_set = [pltpu.VMEM((tt, Z_MAIN), F32), pltpu.VMEM((tt, 3 * ML_WIDTH), BF16),
                pltpu.VMEM((8, tt), F32), pltpu.VMEM((8, tt), F32)]
    scratch = proj_set + proj_set + [
        pltpu.VMEM((tt, ML_WIDTH + POOL_WIDTH), BF16),
        pltpu.VMEM((tt + POOL_KEEP, POOL_WIDTH), F32),
        pltpu.VMEM((2, tt + POOL_KEEP, POOL_GROUP), F32),
        pltpu.VMEM((ML_HEADS, ML_HEAD_DIM, ML_HEAD_DIM), F32),
        pltpu.VMEM((8, ML_HEAD_DIM), F32),
        pltpu.VMEM((8, LANES), F32),
    ]
    y, c, n, m, buf = pl.pallas_call(
        kern, grid=(b, n_pairs), in_specs=in_specs, out_specs=out_specs, out_shape=out_shape,
        scratch_shapes=scratch, compiler_params=_params("arbitrary", "arbitrary"), name="mixer0_prompt",
    )(x, x, w["g_mix0"], w["w_in0_gates"], w["w_gate_i_t"], w["w_gate_f_t"], w["b_i_col"], w["b_f_col"],
      w["mlstm_norm"], w["pool_w"], w["pool_scale"], w["w_out0"])
    return y, c, n, m[:, :ML_HEADS, 0], buf[:, 1:, :]


def _norm_matmul_kernel(x_ref, g_ref, w_ref, o_ref):
    o_ref[...] = _dot(_rms(x_ref[...], g_ref[...]).astype(BF16), w_ref[...])


def _norm_matmul(x, g, w_bf16):
    m, d = x.shape
    n = w_bf16.shape[1]
    tm = min(128, m)
    assert m % tm == 0
    return pl.pallas_call(
        _norm_matmul_kernel, grid=(m // tm,),
        in_specs=[pl.BlockSpec((tm, d), lambda i: (i, 0)), _full((1, d)), _full((d, n))],
        out_specs=pl.BlockSpec((tm, n), lambda i: (i, 0)),
        out_shape=jax.ShapeDtypeStruct((m, n), F32),
        compiler_params=_params("arbitrary"), name="norm_matmul",
    )(x, g, w_bf16)


def _mixer0_step_kernel(z_ref, x_ref, c0_ref, n0_ref, m0_ref, buf_ref, bi_ref, bf_ref, gn_ref, pw_ref,
                        ps_ref, wout_ref, y_ref, c_ref, n_ref, m_ref, bufo_ref, cq_s, asm_s, *, bb, cnts):
    dh = ML_HEAD_DIM
    gate_i = Z_MAIN
    gate_f = Z_MAIN + ML_HEADS
    ig = z_ref[:, gate_i:gate_i + ML_HEADS] + bi_ref[...]
    lf = _log_sigmoid(z_ref[:, gate_f:gate_f + ML_HEADS] + bf_ref[...])
    m0 = m0_ref[...]
    m_t = jnp.maximum(lf + m0, ig)
    w_i = jnp.exp(ig - m_t)
    w_f = jnp.exp(lf + m0 - m_t)
    e_m = jnp.exp(-m_t)
    m_ref[...] = m_t

    row8 = lax.broadcasted_iota(jnp.int32, (8, dh), 0) == 0
    for h in range(ML_HEADS):
        q = z_ref[:, h * dh:(h + 1) * dh]
        k = z_ref[:, ML_WIDTH + h * dh:ML_WIDTH + (h + 1) * dh] * (dh ** -0.5)
        v = z_ref[:, 2 * ML_WIDTH + h * dh:2 * ML_WIDTH + (h + 1) * dh]
        wi = w_i[:, h:h + 1]
        wf = w_f[:, h:h + 1]
        kw = k * wi
        for b in range(bb):
            c_mat = c0_ref[b, h]
            q8 = jnp.broadcast_to(q[b:b + 1, :], (8, dh))
            cq_s[b:b + 1, :] = _dot_nt(q8, c_mat)[0:1, :]
            v8 = jnp.where(row8, v[b:b + 1, :], 0.0)
            k8 = jnp.where(row8, kw[b:b + 1, :], 0.0)
            c_ref[b, h] = wf[b:b + 1, :] * c_mat + _dot_tn(v8, k8)
        n_old = n0_ref[:, h, :]
        s = jnp.sum(q * k, axis=1, keepdims=True) * wi
        num = s * v + wf * cq_s[...]
        den = s + wf * jnp.sum(n_old * q, axis=1, keepdims=True)
        hh = num / jnp.maximum(jnp.abs(den), e_m[:, h:h + 1])
        n_ref[:, h, :] = wf * n_old + kw
        o_gate = jax.nn.sigmoid(z_ref[:, 3 * ML_WIDTH + h * dh:3 * ML_WIDTH + (h + 1) * dh])
        hm = _head_norm(hh, gn_ref[:, h * dh:(h + 1) * dh]) * o_gate
        asm_s[:, h * dh:(h + 1) * dh] = hm

    u_all = z_ref[:, 4 * ML_WIDTH:Z_MAIN]
    for g, win in enumerate(POOL_WINDOWS):
        u = u_all[:, g * POOL_GROUP:(g + 1) * POOL_GROUP]
        wsum = u
        for j in range(1, win):
            off = (POOL_BUF - j) * POOL_WIDTH + g * POOL_GROUP
            wsum = wsum + buf_ref[:, off:off + POOL_GROUP]
        pooled = wsum / cnts[g] - u
        og = _dot(pooled.astype(BF16), pw_ref[g]) * ps_ref[:, g * POOL_GROUP:(g + 1) * POOL_GROUP]
        asm_s[:, ML_WIDTH + g * POOL_GROUP:ML_WIDTH + (g + 1) * POOL_GROUP] = og
    keep = (POOL_BUF - 1) * POOL_WIDTH
    bufo_ref[:, 0:keep] = buf_ref[:, POOL_WIDTH:POOL_WIDTH + keep]
    bufo_ref[:, keep:keep + POOL_WIDTH] = u_all

    y_ref[...] = x_ref[...] + _dot(asm_s[...].astype(BF16), wout_ref[...])


def _mixer0_sample(x, c0, n0, m0, buf0, w, pos0):
    b, d = x.shape
    z = _norm_matmul(x, w["g_mix0"], w["w_in0_gates"])
    zw = z.shape[1]
    bb = 8
    assert b % bb == 0
    cnts = tuple(float(min(win, pos0 + 1)) for win in POOL_WINDOWS)
    bufw = POOL_BUF * POOL_WIDTH
    kern = functools.partial(_mixer0_step_kernel, bb=bb, cnts=cnts)
    hd = (ML_HEADS, ML_HEAD_DIM)
    in_specs = [
        pl.BlockSpec((bb, zw), lambda i: (i, 0)),
        pl.BlockSpec((bb, d), lambda i: (i, 0)),
        pl.BlockSpec((bb,) + hd + (ML_HEAD_DIM,), lambda i: (i, 0, 0, 0)),
        pl.BlockSpec((bb,) + hd, lambda i: (i, 0, 0)),
        pl.BlockSpec((bb, ML_HEADS), lambda i: (i, 0)),
        pl.BlockSpec((bb, bufw), lambda i: (i, 0)),
        _full((1, ML_HEADS)),
        _full((1, ML_HEADS)),
        _full((1, ML_WIDTH)),
        _full((len(POOL_WINDOWS), POOL_GROUP, POOL_GROUP)),
        _full((1, POOL_WIDTH)),
        _full((ML_WIDTH + POOL_WIDTH, d)),
    ]
    out_specs = (
        pl.BlockSpec((bb, d), lambda i: (i, 0)),
        pl.BlockSpec((bb,) + hd + (ML_HEAD_DIM,), lambda i: (i, 0, 0, 0)),
        pl.BlockSpec((bb,) + hd, lambda i: (i, 0, 0)),
        pl.BlockSpec((bb, ML_HEADS), lambda i: (i, 0)),
        pl.BlockSpec((bb, bufw), lambda i: (i, 0)),
    )
    out_shape = (
        jax.ShapeDtypeStruct((b, d), F32),
        jax.ShapeDtypeStruct((b,) + hd + (ML_HEAD_DIM,), F32),
        jax.ShapeDtypeStruct((b,) + hd, F32),
        jax.ShapeDtypeStruct((b, ML_HEADS), F32),
        jax.ShapeDtypeStruct((b, bufw), F32),
    )
    scratch = [pltpu.VMEM((bb, ML_HEAD_DIM), F32), pltpu.VMEM((bb, ML_WIDTH + POOL_WIDTH), F32)]
    y, c, n, m, buf = pl.pallas_call(
        kern, grid=(b // bb,), in_specs=in_specs, out_specs=out_specs, out_shape=out_shape,
        scratch_shapes=scratch, compiler_params=_params("arbitrary"), name="mixer0_sample",
    )(z, x, c0, n0, m0, buf0.reshape(b, bufw), w["b_i_row"], w["b_f_row"], w["mlstm_norm"],
      w["pool_w"], w["pool_scale"], w["w_out0"])
    return y, c, n, m, buf.reshape(b, POOL_BUF, POOL_WIDTH)


def _ffn_kernel(x_ref, g_ref, wu_ref, wd_ref, gf_ref, y_ref, hn_s, acc_s, *, ts, tf, final_norm):
    n_sub = x_ref.shape[0] // ts
    for s in range(n_sub):
        hn_s[s] = _rms(x_ref[s * ts:(s + 1) * ts, :], g_ref[...]).astype(BF16)
    for s in range(n_sub):
        rows = slice(s * ts, (s + 1) * ts)
        for f in range(wu_ref.shape[1] // tf):
            a = jnp.maximum(_dot(hn_s[s], wu_ref[:, f * tf:(f + 1) * tf]), 0.0)
            part = _dot((a * a).astype(BF16), wd_ref[f * tf:(f + 1) * tf, :])
            if f == 0:
                acc_s[s] = part
            else:
                acc_s[s] += part
        y = x_ref[rows, :] + acc_s[s]
        if final_norm:
            y = _rms(y, gf_ref[...])
        y_ref[rows, :] = y


def _ffn(x, g, w_up, w_down, layer, g_final, final_norm):
    m, d = x.shape
    dff = w_up.shape[2]
    n_sub = 2 if m % 16 == 0 else 1
    ts = min(512, m // n_sub)
    tf = min(1024, dff)
    tm = n_sub * ts
    assert m % tm == 0 and dff % tf == 0 and ts % 8 == 0
    kern = functools.partial(_ffn_kernel, ts=ts, tf=tf, final_norm=final_norm)
    resident = pl.Buffered(1)
    return pl.pallas_call(
        kern, grid=(m // tm,),
        in_specs=[
            pl.BlockSpec((tm, d), lambda i: (i, 0)),
            _full((1, d)),
            pl.BlockSpec((None, d, dff), lambda i: (layer, 0, 0), pipeline_mode=resident),
            pl.BlockSpec((None, dff, d), lambda i: (layer, 0, 0), pipeline_mode=resident),
            _full((1, d)),
        ],
        out_specs=pl.BlockSpec((tm, d), lambda i: (i, 0)),
        out_shape=jax.ShapeDtypeStruct((m, d), F32),
        scratch_shapes=[pltpu.VMEM((n_sub, ts, d), BF16), pltpu.VMEM((n_sub, ts, d), F32)],
        compiler_params=_params("arbitrary"), name="ffn",
    )(x, g, w_up, w_down, g_final)


def _query_parts(tokens):
    return 2 if tokens % (2 * LANES) == 0 else 1


def _mla_proj_kernel(x_ref, g_ref, win_ref, gq_ref, gkv_ref, wqn_ref, wqp_ref, wqs_ref, wuk_ref,
                     cos_ref, sin_ref, ckv_ref, kpe_ref, key_ref, vt_ref, qt_ref):
    r = MLA_KV_RANK
    tm = x_ref.shape[1]
    hn = _rms(x_ref[0], g_ref[...]).astype(BF16)
    z = _dot(hn, win_ref[...])
    cq = _rms(z[:, 0:MLA_Q_RANK], gq_ref[...]).astype(BF16)
    ckv = _rms(z[:, MLA_Q_RANK:MLA_Q_RANK + r], gkv_ref[...])
    cos = cos_ref[...]
    sin = sin_ref[...]
    kpe = z[:, MLA_Q_RANK + r:MLA_Q_RANK + r + LANES] * cos + z[:, MLA_Q_RANK + r + LANES:] * sin
    ckv_ref[0] = ckv
    kpe_ref[0] = kpe[:, 0:MLA_ROPE]
    key_ref[0, :, 0:r] = ckv.astype(BF16)
    key_ref[0, :, r:r + LANES] = kpe.astype(BF16)
    vt_ref[0, 0] = ckv.T.astype(BF16)
    qn = _dot(cq, wqn_ref[...])
    qp = _dot(cq, wqp_ref[...])
    qs = _dot(cq, wqs_ref[...])
    n_parts = _query_parts(tm)
    pt = tm // n_parts
    for h in range(MLA_HEADS):
        lanes = slice(h * LANES, (h + 1) * LANES)
        q_lat_t = _dot_nt(wuk_ref[h], qn[:, lanes].astype(BF16)).astype(BF16)
        q_pe_t = (qp[:, lanes] * cos + qs[:, lanes] * sin).T.astype(BF16)
        for part in range(n_parts):
            cols = slice((part * MLA_HEADS + h) * pt, (part * MLA_HEADS + h + 1) * pt)
            qt_ref[0, 0, 0:r, cols] = q_lat_t[:, part * pt:(part + 1) * pt]
            qt_ref[0, 0, r:r + LANES, cols] = q_pe_t[:, part * pt:(part + 1) * pt]


def _mla_proj(x, w, pos):
    nb, t, d = x.shape
    tm = min(MLA_TOKEN_TILE, t)
    assert t % tm == 0
    nt = t // tm
    kw = MLA_KEY_WIDTH
    cos, sin = _rope_tables(pos)
    out_shape = (
        jax.ShapeDtypeStruct((nb, t, MLA_KV_RANK), F32),
        jax.ShapeDtypeStruct((nb, t, MLA_ROPE), F32),
        jax.ShapeDtypeStruct((nb, t, kw), BF16),
        jax.ShapeDtypeStruct((nb, nt, MLA_KV_RANK, tm), BF16),
        jax.ShapeDtypeStruct((nb, nt, kw, MLA_HEADS * tm), BF16),
    )
    in_specs = [
        pl.BlockSpec((1, tm, d), lambda i, j: (i, j, 0)),
        _full((1, d)),
        _full(w["w_in1"].shape),
        _full((1, MLA_Q_RANK)),
        _full((1, MLA_KV_RANK)),
        _full(w["w_q_nope"].shape),
        _full(w["w_q_pe"].shape),
        _full(w["w_q_pe_sw"].shape),
        _full(w["w_uk_h"].shape),
        pl.BlockSpec((tm, LANES), lambda i, j: (j, 0)),
        pl.BlockSpec((tm, LANES), lambda i, j: (j, 0)),
    ]
    out_specs = (
        pl.BlockSpec((1, tm, MLA_KV_RANK), lambda i, j: (i, j, 0)),
        pl.BlockSpec((1, tm, MLA_ROPE), lambda i, j: (i, j, 0)),
        pl.BlockSpec((1, tm, kw), lambda i, j: (i, j, 0)),
        pl.BlockSpec((1, 1, MLA_KV_RANK, tm), lambda i, j: (i, j, 0, 0)),
        pl.BlockSpec((1, 1, kw, MLA_HEADS * tm), lambda i, j: (i, j, 0, 0)),
    )
    return pl.pallas_call(
        _mla_proj_kernel, grid=(nb, nt), in_specs=in_specs, out_specs=out_specs, out_shape=out_shape,
        compiler_params=_params("arbitrary", "arbitrary"), name="mla_proj",
    )(x, w["g_mix1"], w["w_in1"], w["mla_q_norm"], w["mla_kv_norm"], w["w_q_nope"], w["w_q_pe"],
      w["w_q_pe_sw"], w["w_uk_h"], cos, sin)


def _flash_kernel(qt_ref, k_ref, vt_ref, x_ref, wuv_ref, wo_ref, y_ref, acc_s, m_s, l_s, asm_s, *, tq):
    qi = pl.program_id(1)
    tk = tq
    rows = MLA_HEADS * tq
    n_parts = _query_parts(tq)
    pt = tq // n_parts
    m_s[...] = jnp.full_like(m_s, -jnp.inf)
    l_s[...] = jnp.zeros_like(l_s)
    acc_s[...] = jnp.zeros_like(acc_s)

    def block(kb, key_lo, n_keys, first_part, masked):
        start = pl.multiple_of(kb * tk, tk)
        cols = slice(first_part * MLA_HEADS * pt, rows)
        n_cols = rows - cols.start
        s = _dot(k_ref[0, pl.ds(start + key_lo, n_keys), :], qt_ref[0, 0, :, cols])
        if masked:
            key = key_lo + lax.broadcasted_iota(jnp.int32, (n_keys, n_cols), 0)
            col = cols.start + lax.broadcasted_iota(jnp.int32, (n_keys, n_cols), 1)
            part_of_col = lax.shift_right_logical(col, (MLA_HEADS * pt).bit_length() - 1)
            tok = part_of_col * pt + (col & (pt - 1))
            s = jnp.where(key <= tok, s, -jnp.inf)
        m_old = m_s[:, cols]
        m_new = jnp.maximum(m_old, jnp.max(s, axis=0, keepdims=True))
        alpha = jnp.exp2((m_old - m_new) * MLA_SCALE_LOG2E)
        p = jnp.exp2((s - m_new) * MLA_SCALE_LOG2E)
        l_s[:, cols] = alpha * l_s[:, cols] + jnp.sum(p, axis=0, keepdims=True)
        acc_s[:, cols] = alpha * acc_s[:, cols] + _dot(vt_ref[0, kb][:, key_lo:key_lo + n_keys], p.astype(BF16))
        m_s[:, cols] = m_new

    def body(kb, carry):
        block(kb, 0, tk, 0, False)
        return carry

    lax.fori_loop(0, qi, body, 0)
    for part in range(n_parts):
        block(qi, part * pt, pt, part, True)
    o_t = acc_s[...] / l_s[...]
    for h in range(MLA_HEADS):
        for part in range(n_parts):
            c0 = (part * MLA_HEADS + h) * pt
            oh = o_t[:, c0:c0 + pt].astype(BF16)
            asm_s[h * MLA_V:(h + 1) * MLA_V, part * pt:(part + 1) * pt] = _dot(wuv_ref[h], oh).astype(BF16)
    y_ref[0] = x_ref[0] + _dot_tn(asm_s[...], wo_ref[...])


def _mla_attend_prompt(q_t, keys, v_t, x, w):
    b, t, d = x.shape
    nt, tq = v_t.shape[1], v_t.shape[3]
    assert t == nt * tq and tq & (tq - 1) == 0
    rows = MLA_HEADS * tq
    kern = functools.partial(_flash_kernel, tq=tq)
    return pl.pallas_call(
        kern, grid=(b, nt),
        in_specs=[
            pl.BlockSpec((1, 1, MLA_KEY_WIDTH, rows), lambda i, j: (i, j, 0, 0)),
            pl.BlockSpec((1, t, MLA_KEY_WIDTH), lambda i, j: (i, 0, 0)),
            pl.BlockSpec((1, nt, MLA_KV_RANK, tq), lambda i, j: (i, 0, 0, 0)),
            pl.BlockSpec((1, tq, d), lambda i, j: (i, j, 0)),
            _full(w["w_uv_h"].shape),
            _full(w["w_out1"].shape),
        ],
        out_specs=pl.BlockSpec((1, tq, d), lambda i, j: (i, j, 0)),
        out_shape=jax.ShapeDtypeStruct((b, t, d), F32),
        scratch_shapes=[
            pltpu.VMEM((MLA_KV_RANK, rows), F32),
            pltpu.VMEM((1, rows), F32),
            pltpu.VMEM((1, rows), F32),
            pltpu.VMEM((MLA_HEADS * MLA_V, tq), BF16),
        ],
        compiler_params=_params("arbitrary", "arbitrary"), name="mla_flash",
    )(q_t, keys, v_t, x, w["w_uv_h"], w["w_out1"])


def _page_copies(pt_ref, lat_hbm, rope_hbm, lat_buf, rope_buf, sem, seq, slot, start, pages=None):
    def body(p, carry):
        page = pt_ref[seq, p]
        copies = (pltpu.make_async_copy(lat_hbm.at[page], lat_buf.at[slot, p], sem.at[0, slot]),
                  pltpu.make_async_copy(rope_hbm.at[page], rope_buf.at[slot, p], sem.at[1, slot]))
        for c in copies:
            if start:
                c.start(priority=p % 2 if isinstance(p, int) else 0)
            else:
                c.wait()
        return carry

    if pages is None:
        lax.fori_loop(0, lat_buf.shape[1], body, 0)
    else:
        for p in pages:
            body(p, 0)


def _decode_kernel(pt_ref, q_ref, kn_ref, lat_hbm, rope_hbm, o_ref, lat_buf, rope_buf, s_buf, sem, *, group):
    r = MLA_KV_RANK
    n_slots, n_pages, page = lat_buf.shape[0], lat_buf.shape[1], lat_buf.shape[2]
    ahead = n_slots - 1
    b = pl.program_id(0)
    last = pl.num_programs(0) - 1
    slot = lax.rem(b, n_slots)
    args = (pt_ref, lat_hbm, rope_hbm, lat_buf, rope_buf, sem)

    @pl.when(b == 0)
    def _():
        for s0 in range(ahead):
            _page_copies(*args, jnp.minimum(s0, last), s0, True)

    _page_copies(*args, b, slot, False, range(n_pages))
    nxt = jnp.minimum(b + ahead, last)
    nxt_slot = lax.rem(b + ahead, n_slots)

    q = q_ref[0]
    ql = q[:, 0:r]
    qp = q[:, r:r + MLA_ROPE]
    kn = kn_ref[0].astype(F32)
    s_new = jnp.sum(q.astype(F32) * kn, axis=1, keepdims=True)
    lat = lat_buf.at[slot]
    rope = rope_buf.at[slot]
    gk = group * page
    for g in range(n_pages // group):
        lat_g = lat[g * group:(g + 1) * group].reshape(gk, r).astype(BF16)
        s_pe = [_dot(qp, rope[g * group + i].astype(BF16)) for i in range(group)]
        s_buf[:, g * gk:(g + 1) * gk] = _dot_nt(ql, lat_g) + jnp.concatenate(s_pe, axis=1)
        _page_copies(*args, nxt, nxt_slot, True, range(g * group, (g + 1) * group))
    s = s_buf[...]
    m = jnp.maximum(jnp.max(s, axis=1, keepdims=True), s_new)
    p_all = jnp.exp((s - m) * MLA_SCALE)
    p_new = jnp.exp((s_new - m) * MLA_SCALE)
    denom = jnp.sum(p_all, axis=1, keepdims=True) + p_new
    acc = p_new * kn[:, 0:r]
    for g in range(n_pages // group):
        lat_g = lat[g * group:(g + 1) * group].reshape(gk, r).astype(BF16)
        acc = acc + _dot(p_all[:, g * gk:(g + 1) * gk].astype(BF16), lat_g)
    o_ref[0] = acc / denom

    @pl.when(b == last)
    def _():
        for j in range(1, n_slots):
            _page_copies(*args, last, lax.rem(last + j, n_slots), False)


def _mla_attend_sample(q, key_new, cache_latent, cache_rope_t, page_table):
    b = q.shape[0]
    n_pages = page_table.shape[1]
    page = cache_latent.shape[1]
    kw = MLA_KEY_WIDTH
    grid_spec = pltpu.PrefetchScalarGridSpec(
        num_scalar_prefetch=1, grid=(b,),
        in_specs=[
            pl.BlockSpec((1, MLA_HEADS, kw), lambda i, pt: (i, 0, 0)),
            pl.BlockSpec((1, 1, kw), lambda i, pt: (i, 0, 0)),
            pl.BlockSpec(memory_space=pl.ANY),
            pl.BlockSpec(memory_space=pl.ANY),
        ],
        out_specs=pl.BlockSpec((1, MLA_HEADS, MLA_KV_RANK), lambda i, pt: (i, 0, 0)),
        scratch_shapes=[
            pltpu.VMEM((DECODE_SLOTS, n_pages, page, MLA_KV_RANK), F32),
            pltpu.VMEM((DECODE_SLOTS, n_pages, MLA_ROPE, page), F32),
            pltpu.VMEM((MLA_HEADS, n_pages * page), F32),
            pltpu.SemaphoreType.DMA((2, DECODE_SLOTS)),
        ],
    )
    kern = functools.partial(_decode_kernel, group=math.gcd(n_pages, 8))
    return pl.pallas_call(
        kern, grid_spec=grid_spec, out_shape=jax.ShapeDtypeStruct((b, MLA_HEADS, MLA_KV_RANK), F32),
        compiler_params=_params("arbitrary"), name="mla_decode",
    )(page_table, q, key_new, cache_latent, cache_rope_t)


def _attn_out_kernel(o_ref, x_ref, wuv_ref, wo_ref, y_ref, asm_s):
    for h in range(MLA_HEADS):
        oh = o_ref[:, h * MLA_KV_RANK:(h + 1) * MLA_KV_RANK].astype(BF16)
        asm_s[:, h * MLA_V:(h + 1) * MLA_V] = _dot(oh, wuv_ref[h]).astype(BF16)
    y_ref[...] = x_ref[...] + _dot(asm_s[...], wo_ref[...])


def _attn_out_sample(o, x, w):
    m, d = x.shape
    tm = min(128, m)
    assert m % tm == 0
    ow = MLA_HEADS * MLA_KV_RANK
    return pl.pallas_call(
        _attn_out_kernel, grid=(m // tm,),
        in_specs=[pl.BlockSpec((tm, ow), lambda i: (i, 0)), pl.BlockSpec((tm, d), lambda i: (i, 0)),
                  _full(w["w_uv_t"].shape), _full(w["w_out1"].shape)],
        out_specs=pl.BlockSpec((tm, d), lambda i: (i, 0)),
        out_shape=jax.ShapeDtypeStruct((m, d), F32),
        scratch_shapes=[pltpu.VMEM((tm, MLA_HEADS * MLA_V), BF16)],
        compiler_params=_params("arbitrary"), name="attn_out",
    )(o.reshape(m, ow), x, w["w_uv_t"], w["w_out1"])


def _rope_tables(pos):
    half = MLA_ROPE // 2
    inv = ROPE_THETA ** (-jnp.arange(half, dtype=F32) * 2.0 / MLA_ROPE)
    ang = pos.astype(F32)[:, None] * inv[None, :]
    cos = jnp.cos(ang)
    sin = jnp.sin(ang)
    pad = jnp.zeros((pos.shape[0], LANES - MLA_ROPE), F32)
    return jnp.concatenate([cos, cos, pad], axis=1), jnp.concatenate([-sin, sin, pad], axis=1)


def _pad_lanes(a, width):
    return jnp.pad(a, ((0, 0), (0, width - a.shape[1])))


def _prepare_weights(norm_mix, norm_ffn, norm_final, w_in0, mlstm_b_i, mlstm_b_f, mlstm_norm, pool_w,
                     pool_scale, w_out0, w_in1, mla_q_norm, mla_kv_norm, w_q_up, w_uk, w_uv, w_out1,
                     w_up, w_down):
    d = w_in0.shape[0]
    half = MLA_ROPE // 2
    gate_i = w_in0[:, Z_MAIN:Z_MAIN + ML_HEADS]
    gate_f = w_in0[:, Z_MAIN + ML_HEADS:Z_MAIN + 2 * ML_HEADS]
    rope_k = w_in1[:, MLA_Q_RANK + MLA_KV_RANK:]
    rope_k_sw = jnp.concatenate([rope_k[:, half:], rope_k[:, :half]], axis=1)
    q_up = w_q_up.reshape(MLA_Q_RANK, MLA_HEADS, MLA_NOPE + MLA_ROPE)
    q_pe = q_up[:, :, MLA_NOPE:]
    q_pe_sw = jnp.concatenate([q_pe[:, :, half:], q_pe[:, :, :half]], axis=2)
    pad_pe = ((0, 0), (0, 0), (0, LANES - MLA_ROPE))
    col8 = lambda v: jnp.pad(v.astype(F32), (0, 8 - ML_HEADS)).reshape(8, 1)
    return {
        "g_mix0": norm_mix[0].reshape(1, d), "g_mix1": norm_mix[1].reshape(1, d),
        "g_ffn0": norm_ffn[0].reshape(1, d), "g_ffn1": norm_ffn[1].reshape(1, d),
        "g_final": norm_final.reshape(1, d),
        "w_in0_gates": _pad_lanes(w_in0, Z_MAIN + LANES).astype(BF16),
        "w_gate_i_t": jnp.pad(gate_i.T, ((0, 8 - ML_HEADS), (0, 0))).astype(BF16),
        "w_gate_f_t": jnp.pad(gate_f.T, ((0, 8 - ML_HEADS), (0, 0))).astype(BF16),
        "b_i_col": col8(mlstm_b_i), "b_f_col": col8(mlstm_b_f),
        "b_i_row": mlstm_b_i.reshape(1, ML_HEADS), "b_f_row": mlstm_b_f.reshape(1, ML_HEADS),
        "mlstm_norm": mlstm_norm.reshape(1, ML_WIDTH),
        "pool_w": pool_w.astype(BF16), "pool_scale": pool_scale.reshape(1, POOL_WIDTH),
        "w_out0": w_out0.astype(BF16),
        "w_in1": jnp.concatenate(
            [w_in1[:, :MLA_Q_RANK + MLA_KV_RANK], _pad_lanes(rope_k, LANES), _pad_lanes(rope_k_sw, LANES)],
            axis=1).astype(BF16),
        "mla_q_norm": mla_q_norm.reshape(1, MLA_Q_RANK), "mla_kv_norm": mla_kv_norm.reshape(1, MLA_KV_RANK),
        "w_q_nope": q_up[:, :, :MLA_NOPE].reshape(MLA_Q_RANK, MLA_HEADS * MLA_NOPE).astype(BF16),
        "w_q_pe": jnp.pad(q_pe, pad_pe).reshape(MLA_Q_RANK, MLA_HEADS * LANES).astype(BF16),
        "w_q_pe_sw": jnp.pad(q_pe_sw, pad_pe).reshape(MLA_Q_RANK, MLA_HEADS * LANES).astype(BF16),
        "w_uk_h": jnp.transpose(w_uk, (1, 0, 2)).astype(BF16),
        "w_uv_t": jnp.transpose(w_uv, (1, 0, 2)).astype(BF16),
        "w_uv_h": jnp.transpose(w_uv, (1, 2, 0)).astype(BF16),
        "w_out1": w_out1.astype(BF16),
        "w_up": w_up.astype(BF16), "w_down": w_down.astype(BF16),
    }


def kernel(x_prompt, x_sample, state_mlstm_C, state_mlstm_n, state_mlstm_m, state_pool, cache_latent, cache_rope_k, page_table, norm_mix, norm_ffn, norm_final, w_in0, mlstm_b_i, mlstm_b_f, mlstm_norm, pool_w, pool_scale, w_out0, w_in1, mla_q_norm, mla_kv_norm, w_q_up, w_uk, w_uv, w_out1, w_up, w_down):
    w = _prepare_weights(norm_mix, norm_ffn, norm_final, w_in0, mlstm_b_i, mlstm_b_f, mlstm_norm, pool_w,
                         pool_scale, w_out0, w_in1, mla_q_norm, mla_kv_norm, w_q_up, w_uk, w_uv, w_out1,
                         w_up, w_down)
    bp, t, d = x_prompt.shape
    bs, ts, _ = x_sample.shape
    assert ts == 1
    past_len = page_table.shape[1] * cache_latent.shape[1]

    x1, c_p, n_p, m_p, buf_p = _mixer0_prompt(x_prompt, w, 0)
    x2 = _ffn(x1.reshape(bp * t, d), w["g_ffn0"], w["w_up"], w["w_down"], 0, w["g_final"], False)
    x2 = x2.reshape(bp, t, d)
    ckv_p, kpe_p, keys_p, vt_p, qt_p = _mla_proj(x2, w, jnp.arange(t))
    x3 = _mla_attend_prompt(qt_p, keys_p, vt_p, x2, w)
    y_p = _ffn(x3.reshape(bp * t, d), w["g_ffn1"], w["w_up"], w["w_down"], 1, w["g_final"], True)
    y_p = y_p.reshape(bp, t, d)

    xs = x_sample.reshape(bs, d)
    xs1, c_s, n_s, m_s, buf_s = _mixer0_sample(xs, state_mlstm_C, state_mlstm_n, state_mlstm_m, state_pool,
                                               w, past_len)
    xs2 = _ffn(xs1, w["g_ffn0"], w["w_up"], w["w_down"], 0, w["g_final"], False)
    ckv_s, kpe_s, keys_s, _, qt_s = _mla_proj(xs2.reshape(1, bs, d), w, jnp.full((bs,), past_len))
    assert qt_s.shape == (1, 1, MLA_KEY_WIDTH, MLA_HEADS * bs)
    q_s = jnp.transpose(qt_s.reshape(MLA_KEY_WIDTH, MLA_HEADS, bs), (2, 1, 0))
    rope_t = jnp.transpose(cache_rope_k, (0, 2, 1))
    o_s = _mla_attend_sample(q_s, keys_s.reshape(bs, 1, MLA_KEY_WIDTH), cache_latent, rope_t, page_table)
    xs3 = _attn_out_sample(o_s, xs2, w)
    y_s = _ffn(xs3, w["g_ffn1"], w["w_up"], w["w_down"], 1, w["g_final"], True)

    return (y_p, y_s.reshape(bs, 1, d), c_p, n_p, m_p, buf_p, ckv_p, kpe_p,
            c_s, n_s, m_s, buf_s, ckv_s.reshape(bs, 1, MLA_KV_RANK), kpe_s.reshape(bs, 1, MLA_ROPE))
```

```python
import functools
import math

import jax
import jax.numpy as jnp
from jax import lax
from jax.experimental import pallas as pl
from jax.experimental.pallas import tpu as pltpu

F32 = jnp.float32
BF16 = jnp.bfloat16

EPS = 1e-6
ML_HEADS = 4
ML_HEAD_DIM = 128
ML_WIDTH = ML_HEADS * ML_HEAD_DIM
POOL_WINDOWS = (2, 4, 8, 16)
POOL_GROUP = 128
POOL_WIDTH = len(POOL_WINDOWS) * POOL_GROUP
POOL_BUF = max(POOL_WINDOWS) - 1
POOL_HIST = POOL_BUF + 1
POOL_KEEP = 2 * POOL_HIST
MLA_HEADS = 8
MLA_NOPE = 128
MLA_ROPE = 64
MLA_V = 128
MLA_Q_RANK = 512
MLA_KV_RANK = 256
MLA_SCALE = (MLA_NOPE + MLA_ROPE) ** -0.5
MLA_SCALE_LOG2E = MLA_SCALE * math.log2(math.e)
ROPE_THETA = 10000.0
LANES = 128
MLA_KEY_WIDTH = MLA_KV_RANK + LANES
Z_MAIN = 4 * ML_WIDTH + POOL_WIDTH
MLA_TOKEN_TILE = 512
DECODE_SLOTS = 3

VMEM_LIMIT_BYTES = 48 * 1024 * 1024
MIXER0_VMEM_LIMIT_BYTES = 56 * 1024 * 1024

_NT = (((1,), (1,)), ((), ()))
_TN = (((0,), (0,)), ((), ()))


def _dot(a, b):
    return jnp.dot(a, b, preferred_element_type=F32)


def _dot_nt(a, b):
    return lax.dot_general(a, b, _NT, preferred_element_type=F32)


def _dot_tn(a, b):
    return lax.dot_general(a, b, _TN, preferred_element_type=F32)


def _rms(x, g):
    return x * lax.rsqrt(jnp.mean(x * x, axis=-1, keepdims=True) + EPS) * g


def _log_sigmoid(x):
    return jnp.minimum(x, 0.0) - jnp.log1p(jnp.exp(-jnp.abs(x)))


def _head_norm(h, g):
    hc = h - jnp.mean(h, axis=-1, keepdims=True)
    return hc * lax.rsqrt(jnp.mean(hc * hc, axis=-1, keepdims=True) + EPS) * g


def _cumsum_lanes(x):
    n = x.shape[-1]
    lane = lax.broadcasted_iota(jnp.int32, x.shape, x.ndim - 1)
    s = 1
    while s < n:
        x = x + jnp.where(lane >= s, pltpu.roll(x, s, axis=x.ndim - 1), 0.0)
        s *= 2
    return x


def _params(*semantics, vmem_limit_bytes=VMEM_LIMIT_BYTES):
    return pltpu.CompilerParams(dimension_semantics=semantics, vmem_limit_bytes=vmem_limit_bytes)


def _full(shape):
    return pl.BlockSpec(shape, lambda *_: (0,) * len(shape))


def _in_proj(x, w_in, z_out, qkv_out, gi_out, gf_out):
    g_ref, win_ref, wgi_ref, wgf_ref, bi_ref, bf_ref = w_in
    hn = _rms(x(), g_ref[...]).astype(BF16)
    z = _dot(hn, win_ref[...])
    z_out[...] = z
    qkv_out[:, 0:ML_WIDTH] = z[:, 0:ML_WIDTH].astype(BF16)
    qkv_out[:, ML_WIDTH:2 * ML_WIDTH] = (z[:, ML_WIDTH:2 * ML_WIDTH] * (ML_HEAD_DIM ** -0.5)).astype(BF16)
    qkv_out[:, 2 * ML_WIDTH:3 * ML_WIDTH] = z[:, 2 * ML_WIDTH:3 * ML_WIDTH].astype(BF16)
    gi_out[...] = _dot_nt(wgi_ref[...], hn) + bi_ref[...]
    gf_out[...] = _log_sigmoid(_dot_nt(wgf_ref[...], hn) + bf_ref[...])


def _mixer0_tile(x, cur, nxt, w_in, gn_ref, pw_ref, ps_ref, wout_ref, asm_s, ext_s, lvl_s, c_s, n_s, m_s,
                 *, chunk, pos):
    z_s, qkv_s, gi_s, gf_s = cur
    tt = z_s.shape[0]
    dh = ML_HEAD_DIM
    units = [(c, h) for c in range(tt // chunk) for h in range(ML_HEADS)]
    rows_of = lambda c: slice(c * chunk, (c + 1) * chunk)
    head = lambda part, h: slice(part * ML_WIDTH + h * dh, part * ML_WIDTH + (h + 1) * dh)
    row = lax.broadcasted_iota(jnp.int32, (chunk, chunk), 0)
    col = lax.broadcasted_iota(jnp.int32, (chunk, chunk), 1)
    causal = row >= col
    diag = row == col

    gates = []
    m_prev8 = m_s[:, 0:1]
    for c in range(tt // chunk):
        bcum = _cumsum_lanes(gf_s[:, rows_of(c)])
        a = gi_s[:, rows_of(c)] - bcum
        m_last8 = jnp.maximum(m_prev8, jnp.max(a, axis=1, keepdims=True))
        gates.append((a, bcum, m_prev8, m_last8, jnp.exp(m_prev8 - m_last8)))
        m_prev8 = bcum[:, chunk - 1:chunk] + m_last8
    m_s[...] = jnp.broadcast_to(m_prev8, m_s.shape)

    scores = {}
    for c, h in units:
        scores[c, h] = _dot_nt(qkv_s[rows_of(c), head(0, h)], qkv_s[rows_of(c), head(1, h)])

    _in_proj(nxt[0], w_in, *nxt[1])

    parts = {}
    for c, h in units:
        a, bcum, m_prev8, m_last8, _ = gates[c]
        a_row = a[h:h + 1, :]
        m_prev = m_prev8[h:h + 1, :]
        a_mat = jnp.where(causal, a_row, -jnp.inf)
        m_col = jnp.maximum(jnp.max(a_mat, axis=1, keepdims=True), m_prev)
        b_col = jnp.sum(jnp.where(diag, bcum[h:h + 1, :], 0.0), axis=1, keepdims=True)
        a_col = jnp.sum(jnp.where(diag, a_row, 0.0), axis=1, keepdims=True)
        s = scores[c, h] * jnp.exp(a_mat - m_col)
        vb = qkv_s[rows_of(c), head(2, h)]
        kw = z_s[rows_of(c), head(1, h)] * (dh ** -0.5) * jnp.exp(a_col - m_last8[h:h + 1, :])
        parts[c, h] = (_dot(s.astype(BF16), vb), jnp.sum(s, axis=1, keepdims=True),
                       jnp.exp(m_prev - m_col), jnp.exp(-(b_col + m_col)),
                       _dot_tn(vb, kw.astype(BF16)), jnp.sum(kw, axis=0, keepdims=True))

    for c, h in units:
        num_intra, den_intra, w_inter, e_neg_m, c_add, n_add = parts[c, h]
        w_old = gates[c][4][h:h + 1, :]
        q = z_s[rows_of(c), head(0, h)]
        c_mat = c_s[h]
        n_row = n_s[h:h + 1, :]
        num = num_intra + w_inter * _dot_nt(qkv_s[rows_of(c), head(0, h)], c_mat.astype(BF16))
        den = den_intra + w_inter * jnp.sum(q * n_row, axis=1, keepdims=True)
        hh = num / jnp.maximum(jnp.abs(den), e_neg_m)
        c_s[h] = w_old * c_mat + c_add
        n_s[h:h + 1, :] = w_old * n_row + n_add
        o_gate = jax.nn.sigmoid(z_s[rows_of(c), head(3, h)])
        hm = _head_norm(hh, gn_ref[:, h * dh:(h + 1) * dh]) * o_gate
        asm_s[rows_of(c), h * dh:(h + 1) * dh] = hm.astype(BF16)

    keep = POOL_KEEP
    end = keep + tt
    ext_s[keep:end, :] = z_s[:, 4 * ML_WIDTH:Z_MAIN]
    for g, win in enumerate(POOL_WINDOWS):
        lanes = slice(g * POOL_GROUP, (g + 1) * POOL_GROUP)
        src, src_lanes = ext_s, lanes
        width, lo, level = 1, 0, 0
        while 2 * width < win:
            lo += 8
            dst = lvl_s.at[level % 2]
            dst[lo:end, :] = src[lo:end, src_lanes] + src[lo - width:end - width, src_lanes]
            src, src_lanes = dst, slice(0, POOL_GROUP)
            width, level = 2 * width, level + 1
        wsum = src[keep:end, src_lanes] + src[keep - width:end - width, src_lanes]
        u = ext_s[keep:end, lanes]
        cnt = jnp.minimum(win, pos + 1).astype(F32)
        pooled = wsum / cnt - u
        og = _dot(pooled.astype(BF16), pw_ref[g]) * ps_ref[:, lanes]
        asm_s[:, ML_WIDTH + g * POOL_GROUP:ML_WIDTH + (g + 1) * POOL_GROUP] = og.astype(BF16)
    ext_s[0:keep, :] = ext_s[tt:end, :]

    return x() + _dot(asm_s[...], wout_ref[...])


def _mixer0_kernel(x_ref, xn_ref, g_ref, win_ref, wgi_ref, wgf_ref, bi_ref, bf_ref, gn_ref, pw_ref, ps_ref,
                   wout_ref, y_ref, c_out, n_out, m_out, buf_out,
                   za_s, qa_s, gia_s, gfa_s, zb_s, qb_s, gib_s, gfb_s, asm_s, ext_s, lvl_s, c_s, n_s, m_s,
                   *, tt, chunk, pos0):
    b = pl.program_id(0)
    i = pl.program_id(1)
    w_in = (g_ref, win_ref, wgi_ref, wgf_ref, bi_ref, bf_ref)
    set_a = (za_s, qa_s, gia_s, gfa_s)
    set_b = (zb_s, qb_s, gib_s, gfb_s)

    @pl.when((b == 0) & (i == 0))
    def _():
        _in_proj(lambda: x_ref[0, 0:tt, :], w_in, *set_a)

    @pl.when(i == 0)
    def _():
        c_s[...] = jnp.zeros_like(c_s)
        n_s[...] = jnp.zeros_like(n_s)
        m_s[...] = jnp.zeros_like(m_s)
        ext_s[0:POOL_KEEP, :] = jnp.zeros((POOL_KEEP, POOL_WIDTH), F32)

    tile = functools.partial(_mixer0_tile, w_in=w_in, gn_ref=gn_ref, pw_ref=pw_ref, ps_ref=ps_ref,
                             wout_ref=wout_ref, asm_s=asm_s, ext_s=ext_s, lvl_s=lvl_s, c_s=c_s, n_s=n_s,
                             m_s=m_s, chunk=chunk)
    pos = pos0 + 2 * i * tt + lax.broadcasted_iota(jnp.int32, (tt, 1), 0)
    x0 = lambda: x_ref[0, 0:tt, :]
    x1 = lambda: x_ref[0, tt:2 * tt, :]
    y_ref[0, 0:tt, :] = tile(x0, set_a, (x1, set_b), pos=pos)
    y_ref[0, tt:2 * tt, :] = tile(x1, set_b, (lambda: xn_ref[0], set_a), pos=pos + tt)

    @pl.when(i == pl.num_programs(1) - 1)
    def _():
        c_out[0] = c_s[...]
        n_out[0] = n_s[0:ML_HEADS, :]
        m_out[0] = m_s[...]
        buf_out[0] = ext_s[POOL_KEEP - POOL_HIST:POOL_KEEP, :]


def _mixer0_prompt(x, w, pos0):
    b, t, d = x.shape
    tt = min(512, t // 2)
    chunk = min(256, tt)
    assert t % (2 * tt) == 0 and tt % chunk == 0 and tt >= POOL_KEEP
    n_pairs = t // (2 * tt)
    kern = functools.partial(_mixer0_kernel, tt=tt, chunk=chunk, pos0=pos0)

    def next_tile(i, j):
        nxt = jnp.minimum(i * n_pairs + j + 1, b * n_pairs - 1)
        return (nxt // n_pairs, 2 * (nxt % n_pairs), 0)

    out_shape = (
        jax.ShapeDtypeStruct((b, t, d), F32),
        jax.ShapeDtypeStruct((b, ML_HEADS, ML_HEAD_DIM, ML_HEAD_DIM), F32),
        jax.ShapeDtypeStruct((b, ML_HEADS, ML_HEAD_DIM), F32),
        jax.ShapeDtypeStruct((b, 8, LANES), F32),
        jax.ShapeDtypeStruct((b, POOL_HIST, POOL_WIDTH), F32),
    )
    in_specs = [
        pl.BlockSpec((1, 2 * tt, d), lambda i, j: (i, j, 0)),
        pl.BlockSpec((1, tt, d), next_tile),
        _full((1, d)),
        pl.BlockSpec((d, Z_MAIN), lambda i, j: (0, 0), pipeline_mode=pl.Buffered(1)),
        _full((8, d)),
        _full((8, d)),
        _full((8, 1)),
        _full((8, 1)),
        _full((1, ML_WIDTH)),
        _full((len(POOL_WINDOWS), POOL_GROUP, POOL_GROUP)),
        _full((1, POOL_WIDTH)),
        pl.BlockSpec((ML_WIDTH + POOL_WIDTH, d), lambda i, j: (0, 0), pipeline_mode=pl.Buffered(1)),
    ]
    out_specs = (
        pl.BlockSpec((1, 2 * tt, d), lambda i, j: (i, j, 0)),
        pl.BlockSpec((1, ML_HEADS, ML_HEAD_DIM, ML_HEAD_DIM), lambda i, j: (i, 0, 0, 0)),
        pl.BlockSpec((1, ML_HEADS, ML_HEAD_DIM), lambda i, j: (i, 0, 0)),
        pl.BlockSpec((1, 8, LANES), lambda i, j: (i, 0, 0)),
        pl.BlockSpec((1, POOL_HIST, POOL_WIDTH), lambda i, j: (i, 0, 0)),
    )
    proj_set = [pltpu.VMEM((tt, Z_MAIN), F32), pltpu.VMEM((tt, 3 * ML_WIDTH), BF16),
                pltpu.VMEM((8, tt), F32), pltpu.VMEM((8, tt), F32)]
    scratch = proj_set + proj_set + [
        pltpu.VMEM((tt, ML_WIDTH + POOL_WIDTH), BF16),
        pltpu.VMEM((tt + POOL_KEEP, POOL_WIDTH), F32),
        pltpu.VMEM((2, tt + POOL_KEEP, POOL_GROUP), F32),
        pltpu.VMEM((ML_HEADS, ML_HEAD_DIM, ML_HEAD_DIM), F32),
        pltpu.VMEM((8, ML_HEAD_DIM), F32),
        pltpu.VMEM((8, LANES), F32),
    ]
    y, c, n, m, buf = pl.pallas_call(
        kern, grid=(b, n_pairs), in_specs=in_specs, out_specs=out_specs, out_shape=out_shape,
        scratch_shapes=scratch, name="mixer0_prompt",
        compiler_params=_params("arbitrary", "arbitrary", vmem_limit_bytes=MIXER0_VMEM_LIMIT_BYTES),
    )(x, x, w["g_mix0"], w["w_in0_gates"], w["w_gate_i_t"], w["w_gate_f_t"], w["b_i_col"], w["b_f_col"],
      w["mlstm_norm"], w["pool_w"], w["pool_scale"], w["w_out0"])
    return y, c, n, m[:, :ML_HEADS, 0], buf[:, 1:, :]


def _norm_matmul_kernel(x_ref, g_ref, w_ref, o_ref):
    o_ref[...] = _dot(_rms(x_ref[...], g_ref[...]).astype(BF16), w_ref[...])


def _norm_matmul(x, g, w_bf16):
    m, d = x.shape
    n = w_bf16.shape[1]
    tm = min(128, m)
    assert m % tm == 0
    return pl.pallas_call(
        _norm_matmul_kernel, grid=(m // tm,),
        in_specs=[pl.BlockSpec((tm, d), lambda i: (i, 0)), _full((1, d)), _full((d, n))],
        out_specs=pl.BlockSpec((tm, n), lambda i: (i, 0)),
        out_shape=jax.ShapeDtypeStruct((m, n), F32),
        compiler_params=_params("arbitrary"), name="norm_matmul",
    )(x, g, w_bf16)


def _mixer0_step_kernel(z_ref, x_ref, c0_ref, n0_ref, m0_ref, buf_ref, bi_ref, bf_ref, gn_ref, pw_ref,
                        ps_ref, wout_ref, y_ref, c_ref, n_ref, m_ref, bufo_ref, cq_s, asm_s, *, bb, cnts):
    dh = ML_HEAD_DIM
    gate_i = Z_MAIN
    gate_f = Z_MAIN + ML_HEADS
    ig = z_ref[:, gate_i:gate_i + ML_HEADS] + bi_ref[...]
    lf = _log_sigmoid(z_ref[:, gate_f:gate_f + ML_HEADS] + bf_ref[...])
    m0 = m0_ref[...]
    m_t = jnp.maximum(lf + m0, ig)
    w_i = jnp.exp(ig - m_t)
    w_f = jnp.exp(lf + m0 - m_t)
    e_m = jnp.exp(-m_t)
    m_ref[...] = m_t

    row8 = lax.broadcasted_iota(jnp.int32, (8, dh), 0) == 0
    for h in range(ML_HEADS):
        q = z_ref[:, h * dh:(h + 1) * dh]
        k = z_ref[:, ML_WIDTH + h * dh:ML_WIDTH + (h + 1) * dh] * (dh ** -0.5)
        v = z_ref[:, 2 * ML_WIDTH + h * dh:2 * ML_WIDTH + (h + 1) * dh]
        wi = w_i[:, h:h + 1]
        wf = w_f[:, h:h + 1]
        kw = k * wi
        for b in range(bb):
            c_mat = c0_ref[b, h]
            q8 = jnp.broadcast_to(q[b:b + 1, :], (8, dh))
            cq_s[b:b + 1, :] = _dot_nt(q8, c_mat)[0:1, :]
            v8 = jnp.where(row8, v[b:b + 1, :], 0.0)
            k8 = jnp.where(row8, kw[b:b + 1, :], 0.0)
            c_ref[b, h] = wf[b:b + 1, :] * c_mat + _dot_tn(v8, k8)
        n_old = n0_ref[:, h, :]
        s = jnp.sum(q * k, axis=1, keepdims=True) * wi
        num = s * v + wf * cq_s[...]
        den = s + wf * jnp.sum(n_old * q, axis=1, keepdims=True)
        hh = num / jnp.maximum(jnp.abs(den), e_m[:, h:h + 1])
        n_ref[:, h, :] = wf * n_old + kw
        o_gate = jax.nn.sigmoid(z_ref[:, 3 * ML_WIDTH + h * dh:3 * ML_WIDTH + (h + 1) * dh])
        hm = _head_norm(hh, gn_ref[:, h * dh:(h + 1) * dh]) * o_gate
        asm_s[:, h * dh:(h + 1) * dh] = hm

    u_all = z_ref[:, 4 * ML_WIDTH:Z_MAIN]
    for g, win in enumerate(POOL_WINDOWS):
        u = u_all[:, g * POOL_GROUP:(g + 1) * POOL_GROUP]
        wsum = u
        for j in range(1, win):
            off = (POOL_BUF - j) * POOL_WIDTH + g * POOL_GROUP
            wsum = wsum + buf_ref[:, off:off + POOL_GROUP]
        pooled = wsum / cnts[g] - u
        og = _dot(pooled.astype(BF16), pw_ref[g]) * ps_ref[:, g * POOL_GROUP:(g + 1) * POOL_GROUP]
        asm_s[:, ML_WIDTH + g * POOL_GROUP:ML_WIDTH + (g + 1) * POOL_GROUP] = og
    keep = (POOL_BUF - 1) * POOL_WIDTH
    bufo_ref[:, 0:keep] = buf_ref[:, POOL_WIDTH:POOL_WIDTH + keep]
    bufo_ref[:, keep:keep + POOL_WIDTH] = u_all

    y_ref[...] = x_ref[...] + _dot(asm_s[...].astype(BF16), wout_ref[...])


def _mixer0_sample(x, c0, n0, m0, buf0, w, pos0):
    b, d = x.shape
    z = _norm_matmul(x, w["g_mix0"], w["w_in0_gates"])
    zw = z.shape[1]
    bb = 8
    assert b % bb == 0
    cnts = tuple(float(min(win, pos0 + 1)) for win in POOL_WINDOWS)
    bufw = POOL_BUF * POOL_WIDTH
    kern = functools.partial(_mixer0_step_kernel, bb=bb, cnts=cnts)
    hd = (ML_HEADS, ML_HEAD_DIM)
    in_specs = [
        pl.BlockSpec((bb, zw), lambda i: (i, 0)),
        pl.BlockSpec((bb, d), lambda i: (i, 0)),
        pl.BlockSpec((bb,) + hd + (ML_HEAD_DIM,), lambda i: (i, 0, 0, 0)),
        pl.BlockSpec((bb,) + hd, lambda i: (i, 0, 0)),
        pl.BlockSpec((bb, ML_HEADS), lambda i: (i, 0)),
        pl.BlockSpec((bb, bufw), lambda i: (i, 0)),
        _full((1, ML_HEADS)),
        _full((1, ML_HEADS)),
        _full((1, ML_WIDTH)),
        _full((len(POOL_WINDOWS), POOL_GROUP, POOL_GROUP)),
        _full((1, POOL_WIDTH)),
        _full((ML_WIDTH + POOL_WIDTH, d)),
    ]
    out_specs = (
        pl.BlockSpec((bb, d), lambda i: (i, 0)),
        pl.BlockSpec((bb,) + hd + (ML_HEAD_DIM,), lambda i: (i, 0, 0, 0)),
        pl.BlockSpec((bb,) + hd, lambda i: (i, 0, 0)),
        pl.BlockSpec((bb, ML_HEADS), lambda i: (i, 0)),
        pl.BlockSpec((bb, bufw), lambda i: (i, 0)),
    )
    out_shape = (
        jax.ShapeDtypeStruct((b, d), F32),
        jax.ShapeDtypeStruct((b,) + hd + (ML_HEAD_DIM,), F32),
        jax.ShapeDtypeStruct((b,) + hd, F32),
        jax.ShapeDtypeStruct((b, ML_HEADS), F32),
        jax.ShapeDtypeStruct((b, bufw), F32),
    )
    scratch = [pltpu.VMEM((bb, ML_HEAD_DIM), F32), pltpu.VMEM((bb, ML_WIDTH + POOL_WIDTH), F32)]
    y, c, n, m, buf = pl.pallas_call(
        kern, grid=(b // bb,), in_specs=in_specs, out_specs=out_specs, out_shape=out_shape,
        scratch_shapes=scratch, compiler_params=_params("arbitrary"), name="mixer0_sample",
    )(z, x, c0, n0, m0, buf0.reshape(b, bufw), w["b_i_row"], w["b_f_row"], w["mlstm_norm"],
      w["pool_w"], w["pool_scale"], w["w_out0"])
    return y, c, n, m, buf.reshape(b, POOL_BUF, POOL_WIDTH)


def _ffn_kernel(x_ref, g_ref, wu_ref, wd_ref, gf_ref, y_ref, hn_s, acc_s, *, ts, tf, final_norm):
    n_sub = x_ref.shape[0] // ts
    for s in range(n_sub):
        hn_s[s] = _rms(x_ref[s * ts:(s + 1) * ts, :], g_ref[...]).astype(BF16)
    for s in range(n_sub):
        rows = slice(s * ts, (s + 1) * ts)
        for f in range(wu_ref.shape[1] // tf):
            a = jnp.maximum(_dot(hn_s[s], wu_ref[:, f * tf:(f + 1) * tf]), 0.0)
            part = _dot((a * a).astype(BF16), wd_ref[f * tf:(f + 1) * tf, :])
            if f == 0:
                acc_s[s] = part
            else:
                acc_s[s] += part
        y = x_ref[rows, :] + acc_s[s]
        if final_norm:
            y = _rms(y, gf_ref[...])
        y_ref[rows, :] = y


def _ffn(x, g, w_up, w_down, layer, g_final, final_norm):
    m, d = x.shape
    dff = w_up.shape[2]
    n_sub = 2 if m % 16 == 0 else 1
    ts = min(512, m // n_sub)
    tf = min(1024, dff)
    tm = n_sub * ts
    assert m % tm == 0 and dff % tf == 0 and ts % 8 == 0
    kern = functools.partial(_ffn_kernel, ts=ts, tf=tf, final_norm=final_norm)
    resident = pl.Buffered(1)
    return pl.pallas_call(
        kern, grid=(m // tm,),
        in_specs=[
            pl.BlockSpec((tm, d), lambda i: (i, 0)),
            _full((1, d)),
            pl.BlockSpec((None, d, dff), lambda i: (layer, 0, 0), pipeline_mode=resident),
            pl.BlockSpec((None, dff, d), lambda i: (layer, 0, 0), pipeline_mode=resident),
            _full((1, d)),
        ],
        out_specs=pl.BlockSpec((tm, d), lambda i: (i, 0)),
        out_shape=jax.ShapeDtypeStruct((m, d), F32),
        scratch_shapes=[pltpu.VMEM((n_sub, ts, d), BF16), pltpu.VMEM((n_sub, ts, d), F32)],
        compiler_params=_params("arbitrary"), name="ffn",
    )(x, g, w_up, w_down, g_final)


def _query_parts(tokens):
    return 2 if tokens % (2 * LANES) == 0 else 1


def _mla_proj_kernel(x_ref, g_ref, win_ref, gq_ref, gkv_ref, wqn_ref, wqp_ref, wqs_ref, wuk_ref,
                     cos_ref, sin_ref, ckv_ref, kpe_ref, key_ref, vt_ref, qt_ref):
    r = MLA_KV_RANK
    tm = x_ref.shape[1]
    hn = _rms(x_ref[0], g_ref[...]).astype(BF16)
    z = _dot(hn, win_ref[...])
    cq = _rms(z[:, 0:MLA_Q_RANK], gq_ref[...]).astype(BF16)
    ckv = _rms(z[:, MLA_Q_RANK:MLA_Q_RANK + r], gkv_ref[...])
    cos = cos_ref[...]
    sin = sin_ref[...]
    kpe = z[:, MLA_Q_RANK + r:MLA_Q_RANK + r + LANES] * cos + z[:, MLA_Q_RANK + r + LANES:] * sin
    ckv_ref[0] = ckv
    kpe_ref[0] = kpe[:, 0:MLA_ROPE]
    key_ref[0, :, 0:r] = ckv.astype(BF16)
    key_ref[0, :, r:r + LANES] = kpe.astype(BF16)
    vt_ref[0, 0] = ckv.T.astype(BF16)
    qn = _dot(cq, wqn_ref[...])
    qp = _dot(cq, wqp_ref[...])
    qs = _dot(cq, wqs_ref[...])
    n_parts = _query_parts(tm)
    pt = tm // n_parts
    for h in range(MLA_HEADS):
        lanes = slice(h * LANES, (h + 1) * LANES)
        q_lat_t = _dot_nt(wuk_ref[h], qn[:, lanes].astype(BF16)).astype(BF16)
        q_pe_t = (qp[:, lanes] * cos + qs[:, lanes] * sin).T.astype(BF16)
        for part in range(n_parts):
            cols = slice((part * MLA_HEADS + h) * pt, (part * MLA_HEADS + h + 1) * pt)
            qt_ref[0, 0, 0:r, cols] = q_lat_t[:, part * pt:(part + 1) * pt]
            qt_ref[0, 0, r:r + LANES, cols] = q_pe_t[:, part * pt:(part + 1) * pt]


def _mla_proj(x, w, pos):
    nb, t, d = x.shape
    tm = min(MLA_TOKEN_TILE, t)
    assert t % tm == 0
    nt = t // tm
    kw = MLA_KEY_WIDTH
    cos, sin = _rope_tables(pos)
    out_shape = (
        jax.ShapeDtypeStruct((nb, t, MLA_KV_RANK), F32),
        jax.ShapeDtypeStruct((nb, t, MLA_ROPE), F32),
        jax.ShapeDtypeStruct((nb, t, kw), BF16),
        jax.ShapeDtypeStruct((nb, nt, MLA_KV_RANK, tm), BF16),
        jax.ShapeDtypeStruct((nb, nt, kw, MLA_HEADS * tm), BF16),
    )
    in_specs = [
        pl.BlockSpec((1, tm, d), lambda i, j: (i, j, 0)),
        _full((1, d)),
        _full(w["w_in1"].shape),
        _full((1, MLA_Q_RANK)),
        _full((1, MLA_KV_RANK)),
        _full(w["w_q_nope"].shape),
        _full(w["w_q_pe"].shape),
        _full(w["w_q_pe_sw"].shape),
        _full(w["w_uk_h"].shape),
        pl.BlockSpec((tm, LANES), lambda i, j: (j, 0)),
        pl.BlockSpec((tm, LANES), lambda i, j: (j, 0)),
    ]
    out_specs = (
        pl.BlockSpec((1, tm, MLA_KV_RANK), lambda i, j: (i, j, 0)),
        pl.BlockSpec((1, tm, MLA_ROPE), lambda i, j: (i, j, 0)),
        pl.BlockSpec((1, tm, kw), lambda i, j: (i, j, 0)),
        pl.BlockSpec((1, 1, MLA_KV_RANK, tm), lambda i, j: (i, j, 0, 0)),
        pl.BlockSpec((1, 1, kw, MLA_HEADS * tm), lambda i, j: (i, j, 0, 0)),
    )
    return pl.pallas_call(
        _mla_proj_kernel, grid=(nb, nt), in_specs=in_specs, out_specs=out_specs, out_shape=out_shape,
        compiler_params=_params("arbitrary", "arbitrary"), name="mla_proj",
    )(x, w["g_mix1"], w["w_in1"], w["mla_q_norm"], w["mla_kv_norm"], w["w_q_nope"], w["w_q_pe"],
      w["w_q_pe_sw"], w["w_uk_h"], cos, sin)


def _flash_kernel(qt_ref, k_ref, vt_ref, x_ref, wuv_ref, wo_ref, y_ref, acc_s, m_s, l_s, asm_s, *, tq):
    qi = pl.program_id(1)
    tk = tq
    rows = MLA_HEADS * tq
    n_parts = _query_parts(tq)
    pt = tq // n_parts
    m_s[...] = jnp.full_like(m_s, -jnp.inf)
    l_s[...] = jnp.zeros_like(l_s)
    acc_s[...] = jnp.zeros_like(acc_s)

    def block(kb, key_lo, n_keys, first_part, masked):
        start = pl.multiple_of(kb * tk, tk)
        cols = slice(first_part * MLA_HEADS * pt, rows)
        n_cols = rows - cols.start
        s = _dot(k_ref[0, pl.ds(start + key_lo, n_keys), :], qt_ref[0, 0, :, cols])
        if masked:
            key = key_lo + lax.broadcasted_iota(jnp.int32, (n_keys, n_cols), 0)
            col = cols.start + lax.broadcasted_iota(jnp.int32, (n_keys, n_cols), 1)
            part_of_col = lax.shift_right_logical(col, (MLA_HEADS * pt).bit_length() - 1)
            tok = part_of_col * pt + (col & (pt - 1))
            s = jnp.where(key <= tok, s, -jnp.inf)
        m_old = m_s[:, cols]
        m_new = jnp.maximum(m_old, jnp.max(s, axis=0, keepdims=True))
        alpha = jnp.exp2((m_old - m_new) * MLA_SCALE_LOG2E)
        p = jnp.exp2((s - m_new) * MLA_SCALE_LOG2E)
        l_s[:, cols] = alpha * l_s[:, cols] + jnp.sum(p, axis=0, keepdims=True)
        acc_s[:, cols] = alpha * acc_s[:, cols] + _dot(vt_ref[0, kb][:, key_lo:key_lo + n_keys], p.astype(BF16))
        m_s[:, cols] = m_new

    def body(kb, carry):
        block(kb, 0, tk, 0, False)
        return carry

    lax.fori_loop(0, qi, body, 0)
    for part in range(n_parts):
        block(qi, part * pt, pt, part, True)
    o_t = acc_s[...] / l_s[...]
    for h in range(MLA_HEADS):
        for part in range(n_parts):
            c0 = (part * MLA_HEADS + h) * pt
            oh = o_t[:, c0:c0 + pt].astype(BF16)
            asm_s[h * MLA_V:(h + 1) * MLA_V, part * pt:(part + 1) * pt] = _dot(wuv_ref[h], oh).astype(BF16)
    y_ref[0] = x_ref[0] + _dot_tn(asm_s[...], wo_ref[...])


def _mla_attend_prompt(q_t, keys, v_t, x, w):
    b, t, d = x.shape
    nt, tq = v_t.shape[1], v_t.shape[3]
    assert t == nt * tq and tq & (tq - 1) == 0
    rows = MLA_HEADS * tq
    kern = functools.partial(_flash_kernel, tq=tq)
    return pl.pallas_call(
        kern, grid=(b, nt),
        in_specs=[
            pl.BlockSpec((1, 1, MLA_KEY_WIDTH, rows), lambda i, j: (i, j, 0, 0)),
            pl.BlockSpec((1, t, MLA_KEY_WIDTH), lambda i, j: (i, 0, 0)),
            pl.BlockSpec((1, nt, MLA_KV_RANK, tq), lambda i, j: (i, 0, 0, 0)),
            pl.BlockSpec((1, tq, d), lambda i, j: (i, j, 0)),
            _full(w["w_uv_h"].shape),
            _full(w["w_out1"].shape),
        ],
        out_specs=pl.BlockSpec((1, tq, d), lambda i, j: (i, j, 0)),
        out_shape=jax.ShapeDtypeStruct((b, t, d), F32),
        scratch_shapes=[
            pltpu.VMEM((MLA_KV_RANK, rows), F32),
            pltpu.VMEM((1, rows), F32),
            pltpu.VMEM((1, rows), F32),
            pltpu.VMEM((MLA_HEADS * MLA_V, tq), BF16),
        ],
        compiler_params=_params("arbitrary", "arbitrary"), name="mla_flash",
    )(q_t, keys, v_t, x, w["w_uv_h"], w["w_out1"])


def _page_copies(pt_ref, lat_hbm, rope_hbm, lat_buf, rope_buf, sem, seq, slot, start, pages=None):
    def body(p, carry):
        page = pt_ref[seq, p]
        copies = (pltpu.make_async_copy(lat_hbm.at[page], lat_buf.at[slot, p], sem.at[0, slot]),
                  pltpu.make_async_copy(rope_hbm.at[page], rope_buf.at[slot, p], sem.at[1, slot]))
        for c in copies:
            if start:
                c.start(priority=p % 2 if isinstance(p, int) else 0)
            else:
                c.wait()
        return carry

    if pages is None:
        lax.fori_loop(0, lat_buf.shape[1], body, 0)
    else:
        for p in pages:
            body(p, 0)


def _decode_kernel(pt_ref, q_ref, kn_ref, lat_hbm, rope_hbm, o_ref, lat_buf, rope_buf, s_buf, sem, *, group):
    r = MLA_KV_RANK
    n_slots, n_pages, page = lat_buf.shape[0], lat_buf.shape[1], lat_buf.shape[2]
    ahead = n_slots - 1
    b = pl.program_id(0)
    last = pl.num_programs(0) - 1
    slot = lax.rem(b, n_slots)
    args = (pt_ref, lat_hbm, rope_hbm, lat_buf, rope_buf, sem)

    @pl.when(b == 0)
    def _():
        for s0 in range(ahead):
            _page_copies(*args, jnp.minimum(s0, last), s0, True)

    _page_copies(*args, b, slot, False, range(n_pages))
    nxt = jnp.minimum(b + ahead, last)
    nxt_slot = lax.rem(b + ahead, n_slots)

    q = q_ref[0]
    ql = q[:, 0:r]
    qp = q[:, r:r + MLA_ROPE]
    kn = kn_ref[0].astype(F32)
    s_new = jnp.sum(q.astype(F32) * kn, axis=1, keepdims=True)
    lat = lat_buf.at[slot]
    rope = rope_buf.at[slot]
    gk = group * page
    for g in range(n_pages // group):
        lat_g = lat[g * group:(g + 1) * group].reshape(gk, r).astype(BF16)
        s_pe = [_dot(qp, rope[g * group + i].astype(BF16)) for i in range(group)]
        s_buf[:, g * gk:(g + 1) * gk] = _dot_nt(ql, lat_g) + jnp.concatenate(s_pe, axis=1)
        _page_copies(*args, nxt, nxt_slot, True, range(g * group, (g + 1) * group))
    s = s_buf[...]
    m = jnp.maximum(jnp.max(s, axis=1, keepdims=True), s_new)
    p_all = jnp.exp((s - m) * MLA_SCALE)
    p_new = jnp.exp((s_new - m) * MLA_SCALE)
    denom = jnp.sum(p_all, axis=1, keepdims=True) + p_new
    acc = p_new * kn[:, 0:r]
    for g in range(n_pages // group):
        lat_g = lat[g * group:(g + 1) * group].reshape(gk, r).astype(BF16)
        acc = acc + _dot(p_all[:, g * gk:(g + 1) * gk].astype(BF16), lat_g)
    o_ref[0] = acc / denom

    @pl.when(b == last)
    def _():
        for j in range(1, n_slots):
            _page_copies(*args, last, lax.rem(last + j, n_slots), False)


def _mla_attend_sample(q, key_new, cache_latent, cache_rope_t, page_table):
    b = q.shape[0]
    n_pages = page_table.shape[1]
    page = cache_latent.shape[1]
    kw = MLA_KEY_WIDTH
    grid_spec = pltpu.PrefetchScalarGridSpec(
        num_scalar_prefetch=1, grid=(b,),
        in_specs=[
            pl.BlockSpec((1, MLA_HEADS, kw), lambda i, pt: (i, 0, 0)),
            pl.BlockSpec((1, 1, kw), lambda i, pt: (i, 0, 0)),
            pl.BlockSpec(memory_space=pl.ANY),
            pl.BlockSpec(memory_space=pl.ANY),
        ],
        out_specs=pl.BlockSpec((1, MLA_HEADS, MLA_KV_RANK), lambda i, pt: (i, 0, 0)),
        scratch_shapes=[
            pltpu.VMEM((DECODE_SLOTS, n_pages, page, MLA_KV_RANK), F32),
            pltpu.VMEM((DECODE_SLOTS, n_pages, MLA_ROPE, page), F32),
            pltpu.VMEM((MLA_HEADS, n_pages * page), F32),
            pltpu.SemaphoreType.DMA((2, DECODE_SLOTS)),
        ],
    )
    kern = functools.partial(_decode_kernel, group=math.gcd(n_pages, 8))
    return pl.pallas_call(
        kern, grid_spec=grid_spec, out_shape=jax.ShapeDtypeStruct((b, MLA_HEADS, MLA_KV_RANK), F32),
        compiler_params=_params("arbitrary"), name="mla_decode",
    )(page_table, q, key_new, cache_latent, cache_rope_t)


def _attn_out_kernel(o_ref, x_ref, wuv_ref, wo_ref, y_ref, asm_s):
    for h in range(MLA_HEADS):
        oh = o_ref[:, h * MLA_KV_RANK:(h + 1) * MLA_KV_RANK].astype(BF16)
        asm_s[:, h * MLA_V:(h + 1) * MLA_V] = _dot(oh, wuv_ref[h]).astype(BF16)
    y_ref[...] = x_ref[...] + _dot(asm_s[...], wo_ref[...])


def _attn_out_sample(o, x, w):
    m, d = x.shape
    tm = min(128, m)
    assert m % tm == 0
    ow = MLA_HEADS * MLA_KV_RANK
    return pl.pallas_call(
        _attn_out_kernel, grid=(m // tm,),
        in_specs=[pl.BlockSpec((tm, ow), lambda i: (i, 0)), pl.BlockSpec((tm, d), lambda i: (i, 0)),
                  _full(w["w_uv_t"].shape), _full(w["w_out1"].shape)],
        out_specs=pl.BlockSpec((tm, d), lambda i: (i, 0)),
        out_shape=jax.ShapeDtypeStruct((m, d), F32),
        scratch_shapes=[pltpu.VMEM((tm, MLA_HEADS * MLA_V), BF16)],
        compiler_params=_params("arbitrary"), name="attn_out",
    )(o.reshape(m, ow), x, w["w_uv_t"], w["w_out1"])


def _rope_tables(pos):
    half = MLA_ROPE // 2
    inv = ROPE_THETA ** (-jnp.arange(half, dtype=F32) * 2.0 / MLA_ROPE)
    ang = pos.astype(F32)[:, None] * inv[None, :]
    cos = jnp.cos(ang)
    sin = jnp.sin(ang)
    pad = jnp.zeros((pos.shape[0], LANES - MLA_ROPE), F32)
    return jnp.concatenate([cos, cos, pad], axis=1), jnp.concatenate([-sin, sin, pad], axis=1)


def _pad_lanes(a, width):
    return jnp.pad(a, ((0, 0), (0, width - a.shape[1])))


def _prepare_weights(norm_mix, norm_ffn, norm_final, w_in0, mlstm_b_i, mlstm_b_f, mlstm_norm, pool_w,
                     pool_scale, w_out0, w_in1, mla_q_norm, mla_kv_norm, w_q_up, w_uk, w_uv, w_out1,
                     w_up, w_down):
    d = w_in0.shape[0]
    half = MLA_ROPE // 2
    gate_i = w_in0[:, Z_MAIN:Z_MAIN + ML_HEADS]
    gate_f = w_in0[:, Z_MAIN + ML_HEADS:Z_MAIN + 2 * ML_HEADS]
    rope_k = w_in1[:, MLA_Q_RANK + MLA_KV_RANK:]
    rope_k_sw = jnp.concatenate([rope_k[:, half:], rope_k[:, :half]], axis=1)
    q_up = w_q_up.reshape(MLA_Q_RANK, MLA_HEADS, MLA_NOPE + MLA_ROPE)
    q_pe = q_up[:, :, MLA_NOPE:]
    q_pe_sw = jnp.concatenate([q_pe[:, :, half:], q_pe[:, :, :half]], axis=2)
    pad_pe = ((0, 0), (0, 0), (0, LANES - MLA_ROPE))
    col8 = lambda v: jnp.pad(v.astype(F32), (0, 8 - ML_HEADS)).reshape(8, 1)
    return {
        "g_mix0": norm_mix[0].reshape(1, d), "g_mix1": norm_mix[1].reshape(1, d),
        "g_ffn0": norm_ffn[0].reshape(1, d), "g_ffn1": norm_ffn[1].reshape(1, d),
        "g_final": norm_final.reshape(1, d),
        "w_in0_gates": _pad_lanes(w_in0, Z_MAIN + LANES).astype(BF16),
        "w_gate_i_t": jnp.pad(gate_i.T, ((0, 8 - ML_HEADS), (0, 0))).astype(BF16),
        "w_gate_f_t": jnp.pad(gate_f.T, ((0, 8 - ML_HEADS), (0, 0))).astype(BF16),
        "b_i_col": col8(mlstm_b_i), "b_f_col": col8(mlstm_b_f),
        "b_i_row": mlstm_b_i.reshape(1, ML_HEADS), "b_f_row": mlstm_b_f.reshape(1, ML_HEADS),
        "mlstm_norm": mlstm_norm.reshape(1, ML_WIDTH),
        "pool_w": pool_w.astype(BF16), "pool_scale": pool_scale.reshape(1, POOL_WIDTH),
        "w_out0": w_out0.astype(BF16),
        "w_in1": jnp.concatenate(
            [w_in1[:, :MLA_Q_RANK + MLA_KV_RANK], _pad_lanes(rope_k, LANES), _pad_lanes(rope_k_sw, LANES)],
            axis=1).astype(BF16),
        "mla_q_norm": mla_q_norm.reshape(1, MLA_Q_RANK), "mla_kv_norm": mla_kv_norm.reshape(1, MLA_KV_RANK),
        "w_q_nope": q_up[:, :, :MLA_NOPE].reshape(MLA_Q_RANK, MLA_HEADS * MLA_NOPE).astype(BF16),
        "w_q_pe": jnp.pad(q_pe, pad_pe).reshape(MLA_Q_RANK, MLA_HEADS * LANES).astype(BF16),
        "w_q_pe_sw": jnp.pad(q_pe_sw, pad_pe).reshape(MLA_Q_RANK, MLA_HEADS * LANES).astype(BF16),
        "w_uk_h": jnp.transpose(w_uk, (1, 0, 2)).astype(BF16),
        "w_uv_t": jnp.transpose(w_uv, (1, 0, 2)).astype(BF16),
        "w_uv_h": jnp.transpose(w_uv, (1, 2, 0)).astype(BF16),
        "w_out1": w_out1.astype(BF16),
        "w_up": w_up.astype(BF16), "w_down": w_down.astype(BF16),
    }


def kernel(x_prompt, x_sample, state_mlstm_C, state_mlstm_n, state_mlstm_m, state_pool, cache_latent, cache_rope_k, page_table, norm_mix, norm_ffn, norm_final, w_in0, mlstm_b_i, mlstm_b_f, mlstm_norm, pool_w, pool_scale, w_out0, w_in1, mla_q_norm, mla_kv_norm, w_q_up, w_uk, w_uv, w_out1, w_up, w_down):
    w = _prepare_weights(norm_mix, norm_ffn, norm_final, w_in0, mlstm_b_i, mlstm_b_f, mlstm_norm, pool_w,
                         pool_scale, w_out0, w_in1, mla_q_norm, mla_kv_norm, w_q_up, w_uk, w_uv, w_out1,
                         w_up, w_down)
    bp, t, d = x_prompt.shape
    bs, ts, _ = x_sample.shape
    assert ts == 1
    past_len = page_table.shape[1] * cache_latent.shape[1]

    x1, c_p, n_p, m_p, buf_p = _mixer0_prompt(x_prompt, w, 0)
    x2 = _ffn(x1.reshape(bp * t, d), w["g_ffn0"], w["w_up"], w["w_down"], 0, w["g_final"], False)
    x2 = x2.reshape(bp, t, d)
    ckv_p, kpe_p, keys_p, vt_p, qt_p = _mla_proj(x2, w, jnp.arange(t))
    x3 = _mla_attend_prompt(qt_p, keys_p, vt_p, x2, w)
    y_p = _ffn(x3.reshape(bp * t, d), w["g_ffn1"], w["w_up"], w["w_down"], 1, w["g_final"], True)
    y_p = y_p.reshape(bp, t, d)

    xs = x_sample.reshape(bs, d)
    xs1, c_s, n_s, m_s, buf_s = _mixer0_sample(xs, state_mlstm_C, state_mlstm_n, state_mlstm_m, state_pool,
                                               w, past_len)
    xs2 = _ffn(xs1, w["g_ffn0"], w["w_up"], w["w_down"], 0, w["g_final"], False)
    ckv_s, kpe_s, keys_s, _, qt_s = _mla_proj(xs2.reshape(1, bs, d), w, jnp.full((bs,), past_len))
    assert qt_s.shape == (1, 1, MLA_KEY_WIDTH, MLA_HEADS * bs)
    q_s = jnp.transpose(qt_s.reshape(MLA_KEY_WIDTH, MLA_HEADS, bs), (2, 1, 0))
    rope_t = jnp.transpose(cache_rope_k, (0, 2, 1))
    o_s = _mla_attend_sample(q_s, keys_s.reshape(bs, 1, MLA_KEY_WIDTH), cache_latent, rope_t, page_table)
    xs3 = _attn_out_sample(o_s, xs2, w)
    y_s = _ffn(xs3, w["g_ffn1"], w["w_up"], w["w_down"], 1, w["g_final"], True)

    return (y_p, y_s.reshape(bs, 1, d), c_p, n_p, m_p, buf_p, ckv_p, kpe_p,
            c_s, n_s, m_s, buf_s, ckv_s.reshape(bs, 1, MLA_KV_RANK), kpe_s.reshape(bs, 1, MLA_ROPE))
```

```python
import functools
import math

import jax
import jax.numpy as jnp
from jax import lax
from jax.experimental import pallas as pl
from jax.experimental.pallas import tpu as pltpu

F32 = jnp.float32
BF16 = jnp.bfloat16

EPS = 1e-6
ML_HEADS = 4
ML_HEAD_DIM = 128
ML_WIDTH = ML_HEADS * ML_HEAD_DIM
POOL_WINDOWS = (2, 4, 8, 16)
POOL_GROUP = 128
POOL_WIDTH = len(POOL_WINDOWS) * POOL_GROUP
POOL_BUF = max(POOL_WINDOWS) - 1
POOL_HIST = POOL_BUF + 1
POOL_KEEP = 2 * POOL_HIST
MLA_HEADS = 8
MLA_NOPE = 128
MLA_ROPE = 64
MLA_V = 128
MLA_Q_RANK = 512
MLA_KV_RANK = 256
MLA_SCALE = (MLA_NOPE + MLA_ROPE) ** -0.5
MLA_SCALE_LOG2E = MLA_SCALE * math.log2(math.e)
ROPE_THETA = 10000.0
LANES = 128
MLA_KEY_WIDTH = MLA_KV_RANK + LANES
Z_MAIN = 4 * ML_WIDTH + POOL_WIDTH
MLA_TOKEN_TILE = 512
DECODE_SLOTS = 3

VMEM_LIMIT_BYTES = 48 * 1024 * 1024
MIXER0_VMEM_LIMIT_BYTES = 56 * 1024 * 1024

_NT = (((1,), (1,)), ((), ()))
_TN = (((0,), (0,)), ((), ()))


def _dot(a, b):
    return jnp.dot(a, b, preferred_element_type=F32)


def _dot_nt(a, b):
    return lax.dot_general(a, b, _NT, preferred_element_type=F32)


def _dot_tn(a, b):
    return lax.dot_general(a, b, _TN, preferred_element_type=F32)


def _rms(x, g):
    return x * lax.rsqrt(jnp.mean(x * x, axis=-1, keepdims=True) + EPS) * g


def _log_sigmoid(x):
    return jnp.minimum(x, 0.0) - jnp.log1p(jnp.exp(-jnp.abs(x)))


def _head_norm(h, g):
    hc = h - jnp.mean(h, axis=-1, keepdims=True)
    return hc * lax.rsqrt(jnp.mean(hc * hc, axis=-1, keepdims=True) + EPS) * g


def _cumsum_lanes(x):
    n = x.shape[-1]
    lane = lax.broadcasted_iota(jnp.int32, x.shape, x.ndim - 1)
    s = 1
    while s < n:
        x = x + jnp.where(lane >= s, pltpu.roll(x, s, axis=x.ndim - 1), 0.0)
        s *= 2
    return x


def _params(*semantics, vmem_limit_bytes=VMEM_LIMIT_BYTES):
    return pltpu.CompilerParams(dimension_semantics=semantics, vmem_limit_bytes=vmem_limit_bytes)


def _full(shape):
    return pl.BlockSpec(shape, lambda *_: (0,) * len(shape))


def _in_proj(x, w_in, z_out, qkv_out, gi_out, gf_out):
    g_ref, win_ref, wgi_ref, wgf_ref, bi_ref, bf_ref = w_in
    hn = _rms(x(), g_ref[...]).astype(BF16)
    z = _dot(hn, win_ref[...])
    z_out[...] = z
    qkv_out[:, 0:ML_WIDTH] = z[:, 0:ML_WIDTH].astype(BF16)
    qkv_out[:, ML_WIDTH:2 * ML_WIDTH] = (z[:, ML_WIDTH:2 * ML_WIDTH] * (ML_HEAD_DIM ** -0.5)).astype(BF16)
    qkv_out[:, 2 * ML_WIDTH:3 * ML_WIDTH] = z[:, 2 * ML_WIDTH:3 * ML_WIDTH].astype(BF16)
    gi_out[...] = _dot_nt(wgi_ref[...], hn) + bi_ref[...]
    gf_out[...] = _log_sigmoid(_dot_nt(wgf_ref[...], hn) + bf_ref[...])


def _mixer0_tile(x, cur, nxt, w_in, gn_ref, pw_ref, ps_ref, wout_ref, asm_s, ext_s, lvl_s, c_s, n_s, m_s,
                 *, chunk, pos):
    z_s, qkv_s, gi_s, gf_s = cur
    tt = z_s.shape[0]
    dh = ML_HEAD_DIM
    units = [(c, h) for c in range(tt // chunk) for h in range(ML_HEADS)]
    rows_of = lambda c: slice(c * chunk, (c + 1) * chunk)
    head = lambda part, h: slice(part * ML_WIDTH + h * dh, part * ML_WIDTH + (h + 1) * dh)
    row = lax.broadcasted_iota(jnp.int32, (chunk, chunk), 0)
    col = lax.broadcasted_iota(jnp.int32, (chunk, chunk), 1)
    causal = row >= col
    diag = row == col

    gates = []
    m_prev8 = m_s[:, 0:1]
    for c in range(tt // chunk):
        bcum = _cumsum_lanes(gf_s[:, rows_of(c)])
        a = gi_s[:, rows_of(c)] - bcum
        m_last8 = jnp.maximum(m_prev8, jnp.max(a, axis=1, keepdims=True))
        gates.append((a, bcum, m_prev8, m_last8, jnp.exp(m_prev8 - m_last8)))
        m_prev8 = bcum[:, chunk - 1:chunk] + m_last8
    m_s[...] = jnp.broadcast_to(m_prev8, m_s.shape)

    scores = {}
    for c, h in units:
        scores[c, h] = _dot_nt(qkv_s[rows_of(c), head(0, h)], qkv_s[rows_of(c), head(1, h)])

    _in_proj(nxt[0], w_in, *nxt[1])

    parts = {}
    for c, h in units:
        a, bcum, m_prev8, m_last8, _ = gates[c]
        a_row = a[h:h + 1, :]
        m_prev = m_prev8[h:h + 1, :]
        a_mat = jnp.where(causal, a_row, -jnp.inf)
        m_col = jnp.maximum(jnp.max(a_mat, axis=1, keepdims=True), m_prev)
        b_col = jnp.sum(jnp.where(diag, bcum[h:h + 1, :], 0.0), axis=1, keepdims=True)
        a_col = jnp.sum(jnp.where(diag, a_row, 0.0), axis=1, keepdims=True)
        s = scores[c, h] * jnp.exp(a_mat - m_col)
        vb = qkv_s[rows_of(c), head(2, h)]
        kw = z_s[rows_of(c), head(1, h)] * (dh ** -0.5) * jnp.exp(a_col - m_last8[h:h + 1, :])
        parts[c, h] = (_dot(s.astype(BF16), vb), jnp.sum(s, axis=1, keepdims=True),
                       jnp.exp(m_prev - m_col), jnp.exp(-(b_col + m_col)),
                       _dot_tn(vb, kw.astype(BF16)), jnp.sum(kw, axis=0, keepdims=True))

    for c, h in units:
        num_intra, den_intra, w_inter, e_neg_m, c_add, n_add = parts[c, h]
        w_old = gates[c][4][h:h + 1, :]
        q = z_s[rows_of(c), head(0, h)]
        c_mat = c_s[h]
        n_row = n_s[h:h + 1, :]
        num = num_intra + w_inter * _dot_nt(qkv_s[rows_of(c), head(0, h)], c_mat.astype(BF16))
        den = den_intra + w_inter * jnp.sum(q * n_row, axis=1, keepdims=True)
        hh = num / jnp.maximum(jnp.abs(den), e_neg_m)
        c_s[h] = w_old * c_mat + c_add
        n_s[h:h + 1, :] = w_old * n_row + n_add
        o_gate = jax.nn.sigmoid(z_s[rows_of(c), head(3, h)])
        hm = _head_norm(hh, gn_ref[:, h * dh:(h + 1) * dh]) * o_gate
        asm_s[rows_of(c), h * dh:(h + 1) * dh] = hm.astype(BF16)

    keep = POOL_KEEP
    end = keep + tt
    ext_s[keep:end, :] = z_s[:, 4 * ML_WIDTH:Z_MAIN]
    for g, win in enumerate(POOL_WINDOWS):
        lanes = slice(g * POOL_GROUP, (g + 1) * POOL_GROUP)
        src, src_lanes = ext_s, lanes
        width, lo, level = 1, 0, 0
        while 2 * width < win:
            lo += 8
            dst = lvl_s.at[level % 2]
            dst[lo:end, :] = src[lo:end, src_lanes] + src[lo - width:end - width, src_lanes]
            src, src_lanes = dst, slice(0, POOL_GROUP)
            width, level = 2 * width, level + 1
        wsum = src[keep:end, src_lanes] + src[keep - width:end - width, src_lanes]
        u = ext_s[keep:end, lanes]
        cnt = jnp.minimum(win, pos + 1).astype(F32)
        pooled = wsum / cnt - u
        og = _dot(pooled.astype(BF16), pw_ref[g]) * ps_ref[:, lanes]
        asm_s[:, ML_WIDTH + g * POOL_GROUP:ML_WIDTH + (g + 1) * POOL_GROUP] = og.astype(BF16)
    ext_s[0:keep, :] = ext_s[tt:end, :]

    return x() + _dot(asm_s[...], wout_ref[...])


def _mixer0_kernel(x_ref, xn_ref, g_ref, win_ref, wgi_ref, wgf_ref, bi_ref, bf_ref, gn_ref, pw_ref, ps_ref,
                   wout_ref, y_ref, c_out, n_out, m_out, buf_out,
                   za_s, qa_s, gia_s, gfa_s, zb_s, qb_s, gib_s, gfb_s, asm_s, ext_s, lvl_s, c_s, n_s, m_s,
                   *, tt, chunk, pos0):
    b = pl.program_id(0)
    i = pl.program_id(1)
    w_in = (g_ref, win_ref, wgi_ref, wgf_ref, bi_ref, bf_ref)
    set_a = (za_s, qa_s, gia_s, gfa_s)
    set_b = (zb_s, qb_s, gib_s, gfb_s)

    @pl.when((b == 0) & (i == 0))
    def _():
        _in_proj(lambda: x_ref[0, 0:tt, :], w_in, *set_a)

    @pl.when(i == 0)
    def _():
        c_s[...] = jnp.zeros_like(c_s)
        n_s[...] = jnp.zeros_like(n_s)
        m_s[...] = jnp.zeros_like(m_s)
        ext_s[0:POOL_KEEP, :] = jnp.zeros((POOL_KEEP, POOL_WIDTH), F32)

    tile = functools.partial(_mixer0_tile, w_in=w_in, gn_ref=gn_ref, pw_ref=pw_ref, ps_ref=ps_ref,
                             wout_ref=wout_ref, asm_s=asm_s, ext_s=ext_s, lvl_s=lvl_s, c_s=c_s, n_s=n_s,
                             m_s=m_s, chunk=chunk)
    pos = pos0 + 2 * i * tt + lax.broadcasted_iota(jnp.int32, (tt, 1), 0)
    x0 = lambda: x_ref[0, 0:tt, :]
    x1 = lambda: x_ref[0, tt:2 * tt, :]
    y_ref[0, 0:tt, :] = tile(x0, set_a, (x1, set_b), pos=pos)
    y_ref[0, tt:2 * tt, :] = tile(x1, set_b, (lambda: xn_ref[0], set_a), pos=pos + tt)

    @pl.when(i == pl.num_programs(1) - 1)
    def _():
        c_out[0] = c_s[...]
        n_out[0] = n_s[0:ML_HEADS, :]
        m_out[0] = m_s[...]
        buf_out[0] = ext_s[POOL_KEEP - POOL_HIST:POOL_KEEP, :]


def _mixer0_prompt(x, w, pos0):
    b, t, d = x.shape
    tt = min(512, t // 2)
    chunk = min(256, tt)
    assert t % (2 * tt) == 0 and tt % chunk == 0 and tt >= POOL_KEEP
    n_pairs = t // (2 * tt)
    kern = functools.partial(_mixer0_kernel, tt=tt, chunk=chunk, pos0=pos0)

    def next_tile(i, j):
        nxt = jnp.minimum(i * n_pairs + j + 1, b * n_pairs - 1)
        return (nxt // n_pairs, 2 * (nxt % n_pairs), 0)

    out_shape = (
        jax.ShapeDtypeStruct((b, t, d), F32),
        jax.ShapeDtypeStruct((b, ML_HEADS, ML_HEAD_DIM, ML_HEAD_DIM), F32),
        jax.ShapeDtypeStruct((b, ML_HEADS, ML_HEAD_DIM), F32),
        jax.ShapeDtypeStruct((b, 8, LANES), F32),
        jax.ShapeDtypeStruct((b, POOL_HIST, POOL_WIDTH), F32),
    )
    in_specs = [
        pl.BlockSpec((1, 2 * tt, d), lambda i, j: (i, j, 0)),
        pl.BlockSpec((1, tt, d), next_tile),
        _full((1, d)),
        pl.BlockSpec((d, Z_MAIN), lambda i, j: (0, 0), pipeline_mode=pl.Buffered(1)),
        _full((8, d)),
        _full((8, d)),
        _full((8, 1)),
        _full((8, 1)),
        _full((1, ML_WIDTH)),
        _full((len(POOL_WINDOWS), POOL_GROUP, POOL_GROUP)),
        _full((1, POOL_WIDTH)),
        pl.BlockSpec((ML_WIDTH + POOL_WIDTH, d), lambda i, j: (0, 0), pipeline_mode=pl.Buffered(1)),
    ]
    out_specs = (
        pl.BlockSpec((1, 2 * tt, d), lambda i, j: (i, j, 0)),
        pl.BlockSpec((1, ML_HEADS, ML_HEAD_DIM, ML_HEAD_DIM), lambda i, j: (i, 0, 0, 0)),
        pl.BlockSpec((1, ML_HEADS, ML_HEAD_DIM), lambda i, j: (i, 0, 0)),
        pl.BlockSpec((1, 8, LANES), lambda i, j: (i, 0, 0)),
        pl.BlockSpec((1, POOL_HIST, POOL_WIDTH), lambda i, j: (i, 0, 0)),
    )
    proj_set = [pltpu.VMEM((tt, Z_MAIN), F32), pltpu.VMEM((tt, 3 * ML_WIDTH), BF16),
                pltpu.VMEM((8, tt), F32), pltpu.VMEM((8, tt), F32)]
    scratch = proj_set + proj_set + [
        pltpu.VMEM((tt, ML_WIDTH + POOL_WIDTH), BF16),
        pltpu.VMEM((tt + POOL_KEEP, POOL_WIDTH), F32),
        pltpu.VMEM((2, tt + POOL_KEEP, POOL_GROUP), F32),
        pltpu.VMEM((ML_HEADS, ML_HEAD_DIM, ML_HEAD_DIM), F32),
        pltpu.VMEM((8, ML_HEAD_DIM), F32),
        pltpu.VMEM((8, LANES), F32),
    ]
    y, c, n, m, buf = pl.pallas_call(
        kern, grid=(b, n_pairs), in_specs=in_specs, out_specs=out_specs, out_shape=out_shape,
        scratch_shapes=scratch, name="mixer0_prompt",
        compiler_params=_params("arbitrary", "arbitrary", vmem_limit_bytes=MIXER0_VMEM_LIMIT_BYTES),
    )(x, x, w["g_mix0"], w["w_in0_gates"], w["w_gate_i_t"], w["w_gate_f_t"], w["b_i_col"], w["b_f_col"],
      w["mlstm_norm"], w["pool_w"], w["pool_scale"], w["w_out0"])
    return y, c, n, m[:, :ML_HEADS, 0], buf[:, 1:, :]


def _norm_matmul_kernel(x_ref, g_ref, w_ref, o_ref):
    o_ref[...] = _dot(_rms(x_ref[...], g_ref[...]).astype(BF16), w_ref[...])


def _norm_matmul(x, g, w_bf16):
    m, d = x.shape
    n = w_bf16.shape[1]
    tm = min(128, m)
    assert m % tm == 0
    return pl.pallas_call(
        _norm_matmul_kernel, grid=(m // tm,),
        in_specs=[pl.BlockSpec((tm, d), lambda i: (i, 0)), _full((1, d)), _full((d, n))],
        out_specs=pl.BlockSpec((tm, n), lambda i: (i, 0)),
        out_shape=jax.ShapeDtypeStruct((m, n), F32),
        compiler_params=_params("arbitrary"), name="norm_matmul",
    )(x, g, w_bf16)


def _mixer0_step_kernel(z_ref, x_ref, c0_ref, n0_ref, m0_ref, buf_ref, bi_ref, bf_ref, gn_ref, pw_ref,
                        ps_ref, wout_ref, y_ref, c_ref, n_ref, m_ref, bufo_ref, cq_s, asm_s, *, bb, cnts):
    dh = ML_HEAD_DIM
    gate_i = Z_MAIN
    gate_f = Z_MAIN + ML_HEADS
    ig = z_ref[:, gate_i:gate_i + ML_HEADS] + bi_ref[...]
    lf = _log_sigmoid(z_ref[:, gate_f:gate_f + ML_HEADS] + bf_ref[...])
    m0 = m0_ref[...]
    m_t = jnp.maximum(lf + m0, ig)
    w_i = jnp.exp(ig - m_t)
    w_f = jnp.exp(lf + m0 - m_t)
    e_m = jnp.exp(-m_t)
    m_ref[...] = m_t

    row8 = lax.broadcasted_iota(jnp.int32, (8, dh), 0) == 0
    for h in range(ML_HEADS):
        q = z_ref[:, h * dh:(h + 1) * dh]
        k = z_ref[:, ML_WIDTH + h * dh:ML_WIDTH + (h + 1) * dh] * (dh ** -0.5)
        v = z_ref[:, 2 * ML_WIDTH + h * dh:2 * ML_WIDTH + (h + 1) * dh]
        wi = w_i[:, h:h + 1]
        wf = w_f[:, h:h + 1]
        kw = k * wi
        for b in range(bb):
            c_mat = c0_ref[b, h]
            q8 = jnp.broadcast_to(q[b:b + 1, :], (8, dh))
            cq_s[b:b + 1, :] = _dot_nt(q8, c_mat)[0:1, :]
            v8 = jnp.where(row8, v[b:b + 1, :], 0.0)
            k8 = jnp.where(row8, kw[b:b + 1, :], 0.0)
            c_ref[b, h] = wf[b:b + 1, :] * c_mat + _dot_tn(v8, k8)
        n_old = n0_ref[:, h, :]
        s = jnp.sum(q * k, axis=1, keepdims=True) * wi
        num = s * v + wf * cq_s[...]
        den = s + wf * jnp.sum(n_old * q, axis=1, keepdims=True)
        hh = num / jnp.maximum(jnp.abs(den), e_m[:, h:h + 1])
        n_ref[:, h, :] = wf * n_old + kw
        o_gate = jax.nn.sigmoid(z_ref[:, 3 * ML_WIDTH + h * dh:3 * ML_WIDTH + (h + 1) * dh])
        hm = _head_norm(hh, gn_ref[:, h * dh:(h + 1) * dh]) * o_gate
        asm_s[:, h * dh:(h + 1) * dh] = hm

    u_all = z_ref[:, 4 * ML_WIDTH:Z_MAIN]
    for g, win in enumerate(POOL_WINDOWS):
        u = u_all[:, g * POOL_GROUP:(g + 1) * POOL_GROUP]
        wsum = u
        for j in range(1, win):
            off = (POOL_BUF - j) * POOL_WIDTH + g * POOL_GROUP
            wsum = wsum + buf_ref[:, off:off + POOL_GROUP]
        pooled = wsum / cnts[g] - u
        og = _dot(pooled.astype(BF16), pw_ref[g]) * ps_ref[:, g * POOL_GROUP:(g + 1) * POOL_GROUP]
        asm_s[:, ML_WIDTH + g * POOL_GROUP:ML_WIDTH + (g + 1) * POOL_GROUP] = og
    keep = (POOL_BUF - 1) * POOL_WIDTH
    bufo_ref[:, 0:keep] = buf_ref[:, POOL_WIDTH:POOL_WIDTH + keep]
    bufo_ref[:, keep:keep + POOL_WIDTH] = u_all

    y_ref[...] = x_ref[...] + _dot(asm_s[...].astype(BF16), wout_ref[...])


def _mixer0_sample(x, c0, n0, m0, buf0, w, pos0):
    b, d = x.shape
    z = _norm_matmul(x, w["g_mix0"], w["w_in0_gates"])
    zw = z.shape[1]
    bb = 16 if b % 16 == 0 else 8
    assert b % bb == 0
    cnts = tuple(float(min(win, pos0 + 1)) for win in POOL_WINDOWS)
    bufw = POOL_BUF * POOL_WIDTH
    kern = functools.partial(_mixer0_step_kernel, bb=bb, cnts=cnts)
    hd = (ML_HEADS, ML_HEAD_DIM)
    in_specs = [
        pl.BlockSpec((bb, zw), lambda i: (i, 0)),
        pl.BlockSpec((bb, d), lambda i: (i, 0)),
        pl.BlockSpec((bb,) + hd + (ML_HEAD_DIM,), lambda i: (i, 0, 0, 0)),
        pl.BlockSpec((bb,) + hd, lambda i: (i, 0, 0)),
        pl.BlockSpec((bb, ML_HEADS), lambda i: (i, 0)),
        pl.BlockSpec((bb, bufw), lambda i: (i, 0)),
        _full((1, ML_HEADS)),
        _full((1, ML_HEADS)),
        _full((1, ML_WIDTH)),
        _full((len(POOL_WINDOWS), POOL_GROUP, POOL_GROUP)),
        _full((1, POOL_WIDTH)),
        _full((ML_WIDTH + POOL_WIDTH, d)),
    ]
    out_specs = (
        pl.BlockSpec((bb, d), lambda i: (i, 0)),
        pl.BlockSpec((bb,) + hd + (ML_HEAD_DIM,), lambda i: (i, 0, 0, 0)),
        pl.BlockSpec((bb,) + hd, lambda i: (i, 0, 0)),
        pl.BlockSpec((bb, ML_HEADS), lambda i: (i, 0)),
        pl.BlockSpec((bb, bufw), lambda i: (i, 0)),
    )
    out_shape = (
        jax.ShapeDtypeStruct((b, d), F32),
        jax.ShapeDtypeStruct((b,) + hd + (ML_HEAD_DIM,), F32),
        jax.ShapeDtypeStruct((b,) + hd, F32),
        jax.ShapeDtypeStruct((b, ML_HEADS), F32),
        jax.ShapeDtypeStruct((b, bufw), F32),
    )
    scratch = [pltpu.VMEM((bb, ML_HEAD_DIM), F32), pltpu.VMEM((bb, ML_WIDTH + POOL_WIDTH), F32)]
    y, c, n, m, buf = pl.pallas_call(
        kern, grid=(b // bb,), in_specs=in_specs, out_specs=out_specs, out_shape=out_shape,
        scratch_shapes=scratch, compiler_params=_params("arbitrary"), name="mixer0_sample",
    )(z, x, c0, n0, m0, buf0.reshape(b, bufw), w["b_i_row"], w["b_f_row"], w["mlstm_norm"],
      w["pool_w"], w["pool_scale"], w["w_out0"])
    return y, c, n, m, buf.reshape(b, POOL_BUF, POOL_WIDTH)


def _ffn_kernel(x_ref, g_ref, wu_ref, wd_ref, gf_ref, y_ref, hn_s, acc_s, *, ts, tf, final_norm, f_steps):
    n_sub = x_ref.shape[0] // ts
    j = pl.program_id(1)

    def norm():
        for s in range(n_sub):
            hn_s[s] = _rms(x_ref[s * ts:(s + 1) * ts, :], g_ref[...]).astype(BF16)

    def mlp(s, assign_first):
        for f in range(wu_ref.shape[1] // tf):
            a = jnp.maximum(_dot(hn_s[s], wu_ref[:, f * tf:(f + 1) * tf]), 0.0)
            part = _dot((a * a).astype(BF16), wd_ref[f * tf:(f + 1) * tf, :])
            if f == 0 and assign_first:
                acc_s[s] = part
            else:
                acc_s[s] += part

    def finish(s):
        rows = slice(s * ts, (s + 1) * ts)
        y = x_ref[rows, :] + acc_s[s]
        if final_norm:
            y = _rms(y, gf_ref[...])
        y_ref[rows, :] = y

    if f_steps == 1:
        norm()
        for s in range(n_sub):
            mlp(s, True)
            finish(s)
    else:
        @pl.when(j == 0)
        def _():
            norm()
            acc_s[...] = jnp.zeros_like(acc_s)

        for s in range(n_sub):
            mlp(s, False)

        @pl.when(j == f_steps - 1)
        def _():
            for s in range(n_sub):
                finish(s)


def _ffn(x, g, w_up, w_down, layer, g_final, final_norm):
    m, d = x.shape
    dff = w_up.shape[2]
    n_sub = 2 if m % 16 == 0 else 1
    ts = min(512, m // n_sub)
    tf = min(1024, dff)
    tm = n_sub * ts
    assert m % tm == 0 and dff % tf == 0 and ts % 8 == 0
    f_steps = dff // tf if m == tm else 1
    fw = dff // f_steps
    kern = functools.partial(_ffn_kernel, ts=ts, tf=tf, final_norm=final_norm, f_steps=f_steps)
    mode = pl.Buffered(1) if f_steps == 1 else None
    return pl.pallas_call(
        kern, grid=(m // tm, f_steps),
        in_specs=[
            pl.BlockSpec((tm, d), lambda i, j: (i, 0)),
            _full((1, d)),
            pl.BlockSpec((None, d, fw), lambda i, j: (layer, 0, j), pipeline_mode=mode),
            pl.BlockSpec((None, fw, d), lambda i, j: (layer, j, 0), pipeline_mode=mode),
            _full((1, d)),
        ],
        out_specs=pl.BlockSpec((tm, d), lambda i, j: (i, 0)),
        out_shape=jax.ShapeDtypeStruct((m, d), F32),
        scratch_shapes=[pltpu.VMEM((n_sub, ts, d), BF16), pltpu.VMEM((n_sub, ts, d), F32)],
        compiler_params=_params("arbitrary", "arbitrary"), name="ffn",
    )(x, g, w_up, w_down, g_final)


def _query_parts(tokens):
    return 2 if tokens % (2 * LANES) == 0 else 1


def _mla_proj_kernel(x_ref, g_ref, win_ref, gq_ref, gkv_ref, wqn_ref, wqp_ref, wqs_ref, wuk_ref,
                     cos_ref, sin_ref, ckv_ref, kpe_ref, key_ref, vt_ref, qt_ref):
    r = MLA_KV_RANK
    tm = x_ref.shape[1]
    hn = _rms(x_ref[0], g_ref[...]).astype(BF16)
    z = _dot(hn, win_ref[...])
    cq = _rms(z[:, 0:MLA_Q_RANK], gq_ref[...]).astype(BF16)
    ckv = _rms(z[:, MLA_Q_RANK:MLA_Q_RANK + r], gkv_ref[...])
    cos = cos_ref[...]
    sin = sin_ref[...]
    kpe = z[:, MLA_Q_RANK + r:MLA_Q_RANK + r + LANES] * cos + z[:, MLA_Q_RANK + r + LANES:] * sin
    ckv_ref[0] = ckv
    kpe_ref[0] = kpe[:, 0:MLA_ROPE]
    key_ref[0, :, 0:r] = ckv.astype(BF16)
    key_ref[0, :, r:r + LANES] = kpe.astype(BF16)
    vt_ref[0, 0] = ckv.T.astype(BF16)
    qn = _dot(cq, wqn_ref[...])
    qp = _dot(cq, wqp_ref[...])
    qs = _dot(cq, wqs_ref[...])
    n_parts = _query_parts(tm)
    pt = tm // n_parts
    for h in range(MLA_HEADS):
        lanes = slice(h * LANES, (h + 1) * LANES)
        q_lat_t = _dot_nt(wuk_ref[h], qn[:, lanes].astype(BF16)).astype(BF16)
        q_pe_t = (qp[:, lanes] * cos + qs[:, lanes] * sin).T.astype(BF16)
        for part in range(n_parts):
            cols = slice((part * MLA_HEADS + h) * pt, (part * MLA_HEADS + h + 1) * pt)
            qt_ref[0, 0, 0:r, cols] = q_lat_t[:, part * pt:(part + 1) * pt]
            qt_ref[0, 0, r:r + LANES, cols] = q_pe_t[:, part * pt:(part + 1) * pt]


def _mla_proj(x, w, pos):
    nb, t, d = x.shape
    tm = min(MLA_TOKEN_TILE, t)
    assert t % tm == 0
    nt = t // tm
    kw = MLA_KEY_WIDTH
    cos, sin = _rope_tables(pos)
    out_shape = (
        jax.ShapeDtypeStruct((nb, t, MLA_KV_RANK), F32),
        jax.ShapeDtypeStruct((nb, t, MLA_ROPE), F32),
        jax.ShapeDtypeStruct((nb, t, kw), BF16),
        jax.ShapeDtypeStruct((nb, nt, MLA_KV_RANK, tm), BF16),
        jax.ShapeDtypeStruct((nb, nt, kw, MLA_HEADS * tm), BF16),
    )
    in_specs = [
        pl.BlockSpec((1, tm, d), lambda i, j: (i, j, 0)),
        _full((1, d)),
        _full(w["w_in1"].shape),
        _full((1, MLA_Q_RANK)),
        _full((1, MLA_KV_RANK)),
        _full(w["w_q_nope"].shape),
        _full(w["w_q_pe"].shape),
        _full(w["w_q_pe_sw"].shape),
        _full(w["w_uk_h"].shape),
        pl.BlockSpec((tm, LANES), lambda i, j: (j, 0)),
        pl.BlockSpec((tm, LANES), lambda i, j: (j, 0)),
    ]
    out_specs = (
        pl.BlockSpec((1, tm, MLA_KV_RANK), lambda i, j: (i, j, 0)),
        pl.BlockSpec((1, tm, MLA_ROPE), lambda i, j: (i, j, 0)),
        pl.BlockSpec((1, tm, kw), lambda i, j: (i, j, 0)),
        pl.BlockSpec((1, 1, MLA_KV_RANK, tm), lambda i, j: (i, j, 0, 0)),
        pl.BlockSpec((1, 1, kw, MLA_HEADS * tm), lambda i, j: (i, j, 0, 0)),
    )
    return pl.pallas_call(
        _mla_proj_kernel, grid=(nb, nt), in_specs=in_specs, out_specs=out_specs, out_shape=out_shape,
        compiler_params=_params("arbitrary", "arbitrary"), name="mla_proj",
    )(x, w["g_mix1"], w["w_in1"], w["mla_q_norm"], w["mla_kv_norm"], w["w_q_nope"], w["w_q_pe"],
      w["w_q_pe_sw"], w["w_uk_h"], cos, sin)


def _flash_kernel(qt_ref, k_ref, vt_ref, x_ref, wuv_ref, wo_ref, y_ref, acc_s, m_s, l_s, asm_s, *, tq):
    qi = pl.program_id(1)
    tk = tq
    rows = MLA_HEADS * tq
    n_parts = _query_parts(tq)
    pt = tq // n_parts
    m_s[...] = jnp.full_like(m_s, -jnp.inf)
    l_s[...] = jnp.zeros_like(l_s)
    acc_s[...] = jnp.zeros_like(acc_s)

    def block(kb, key_lo, n_keys, first_part, masked):
        start = pl.multiple_of(kb * tk, tk)
        cols = slice(first_part * MLA_HEADS * pt, rows)
        n_cols = rows - cols.start
        s = _dot(k_ref[0, pl.ds(start + key_lo, n_keys), :], qt_ref[0, 0, :, cols])
        if masked:
            key = key_lo + lax.broadcasted_iota(jnp.int32, (n_keys, n_cols), 0)
            col = cols.start + lax.broadcasted_iota(jnp.int32, (n_keys, n_cols), 1)
            part_of_col = lax.shift_right_logical(col, (MLA_HEADS * pt).bit_length() - 1)
            tok = part_of_col * pt + (col & (pt - 1))
            s = jnp.where(key <= tok, s, -jnp.inf)
        m_old = m_s[:, cols]
        m_new = jnp.maximum(m_old, jnp.max(s, axis=0, keepdims=True))
        alpha = jnp.exp2((m_old - m_new) * MLA_SCALE_LOG2E)
        p = jnp.exp2((s - m_new) * MLA_SCALE_LOG2E)
        l_s[:, cols] = alpha * l_s[:, cols] + jnp.sum(p, axis=0, keepdims=True)
        acc_s[:, cols] = alpha * acc_s[:, cols] + _dot(vt_ref[0, kb][:, key_lo:key_lo + n_keys], p.astype(BF16))
        m_s[:, cols] = m_new

    def body(kb, carry):
        block(kb, 0, tk, 0, False)
        return carry

    lax.fori_loop(0, qi, body, 0)
    for part in range(n_parts):
        block(qi, part * pt, pt, part, True)
    o_t = acc_s[...] / l_s[...]
    for h in range(MLA_HEADS):
        for part in range(n_parts):
            c0 = (part * MLA_HEADS + h) * pt
            oh = o_t[:, c0:c0 + pt].astype(BF16)
            asm_s[h * MLA_V:(h + 1) * MLA_V, part * pt:(part + 1) * pt] = _dot(wuv_ref[h], oh).astype(BF16)
    y_ref[0] = x_ref[0] + _dot_tn(asm_s[...], wo_ref[...])


def _mla_attend_prompt(q_t, keys, v_t, x, w):
    b, t, d = x.shape
    nt, tq = v_t.shape[1], v_t.shape[3]
    assert t == nt * tq and tq & (tq - 1) == 0
    rows = MLA_HEADS * tq
    kern = functools.partial(_flash_kernel, tq=tq)
    return pl.pallas_call(
        kern, grid=(b, nt),
        in_specs=[
            pl.BlockSpec((1, 1, MLA_KEY_WIDTH, rows), lambda i, j: (i, j, 0, 0)),
            pl.BlockSpec((1, t, MLA_KEY_WIDTH), lambda i, j: (i, 0, 0)),
            pl.BlockSpec((1, nt, MLA_KV_RANK, tq), lambda i, j: (i, 0, 0, 0)),
            pl.BlockSpec((1, tq, d), lambda i, j: (i, j, 0)),
            _full(w["w_uv_h"].shape),
            _full(w["w_out1"].shape),
        ],
        out_specs=pl.BlockSpec((1, tq, d), lambda i, j: (i, j, 0)),
        out_shape=jax.ShapeDtypeStruct((b, t, d), F32),
        scratch_shapes=[
            pltpu.VMEM((MLA_KV_RANK, rows), F32),
            pltpu.VMEM((1, rows), F32),
            pltpu.VMEM((1, rows), F32),
            pltpu.VMEM((MLA_HEADS * MLA_V, tq), BF16),
        ],
        compiler_params=_params("arbitrary", "arbitrary"), name="mla_flash",
    )(q_t, keys, v_t, x, w["w_uv_h"], w["w_out1"])


def _page_copies(pt_ref, lat_hbm, rope_hbm, lat_buf, rope_buf, sem, seq, slot, start, pages=None):
    def body(p, carry):
        page = pt_ref[seq, p]
        copies = (pltpu.make_async_copy(lat_hbm.at[page], lat_buf.at[slot, p], sem.at[0, slot]),
                  pltpu.make_async_copy(rope_hbm.at[page], rope_buf.at[slot, p], sem.at[1, slot]))
        for c in copies:
            if start:
                c.start(priority=p % 2 if isinstance(p, int) else 0)
            else:
                c.wait()
        return carry

    if pages is None:
        lax.fori_loop(0, lat_buf.shape[1], body, 0)
    else:
        for p in pages:
            body(p, 0)


def _decode_kernel(pt_ref, q_ref, kn_ref, lat_hbm, rope_hbm, o_ref, lat_buf, rope_buf, s_buf, sem, *, group):
    r = MLA_KV_RANK
    n_slots, n_pages, page = lat_buf.shape[0], lat_buf.shape[1], lat_buf.shape[2]
    ahead = n_slots - 1
    b = pl.program_id(0)
    last = pl.num_programs(0) - 1
    slot = lax.rem(b, n_slots)
    args = (pt_ref, lat_hbm, rope_hbm, lat_buf, rope_buf, sem)

    @pl.when(b == 0)
    def _():
        for s0 in range(ahead):
            _page_copies(*args, jnp.minimum(s0, last), s0, True)

    _page_copies(*args, b, slot, False, range(n_pages))
    nxt = jnp.minimum(b + ahead, last)
    nxt_slot = lax.rem(b + ahead, n_slots)

    q = q_ref[0]
    ql = q[:, 0:r]
    qp = q[:, r:r + MLA_ROPE]
    kn = kn_ref[0].astype(F32)
    s_new = jnp.sum(q.astype(F32) * kn, axis=1, keepdims=True)
    lat = lat_buf.at[slot]
    rope = rope_buf.at[slot]
    gk = group * page
    for g in range(n_pages // group):
        lat_g = lat[g * group:(g + 1) * group].reshape(gk, r).astype(BF16)
        s_pe = [_dot(qp, rope[g * group + i].astype(BF16)) for i in range(group)]
        s_buf[:, g * gk:(g + 1) * gk] = _dot_nt(ql, lat_g) + jnp.concatenate(s_pe, axis=1)
        _page_copies(*args, nxt, nxt_slot, True, range(g * group, (g + 1) * group))
    s = s_buf[...]
    m = jnp.maximum(jnp.max(s, axis=1, keepdims=True), s_new)
    p_all = jnp.exp((s - m) * MLA_SCALE)
    p_new = jnp.exp((s_new - m) * MLA_SCALE)
    denom = jnp.sum(p_all, axis=1, keepdims=True) + p_new
    acc = p_new * kn[:, 0:r]
    for g in range(n_pages // group):
        lat_g = lat[g * group:(g + 1) * group].reshape(gk, r).astype(BF16)
        acc = acc + _dot(p_all[:, g * gk:(g + 1) * gk].astype(BF16), lat_g)
    o_ref[0] = acc / denom

    @pl.when(b == last)
    def _():
        for j in range(1, n_slots):
            _page_copies(*args, last, lax.rem(last + j, n_slots), False)


def _mla_attend_sample(q, key_new, cache_latent, cache_rope_t, page_table):
    b = q.shape[0]
    n_pages = page_table.shape[1]
    page = cache_latent.shape[1]
    kw = MLA_KEY_WIDTH
    grid_spec = pltpu.PrefetchScalarGridSpec(
        num_scalar_prefetch=1, grid=(b,),
        in_specs=[
            pl.BlockSpec((1, MLA_HEADS, kw), lambda i, pt: (i, 0, 0)),
            pl.BlockSpec((1, 1, kw), lambda i, pt: (i, 0, 0)),
            pl.BlockSpec(memory_space=pl.ANY),
            pl.BlockSpec(memory_space=pl.ANY),
        ],
        out_specs=pl.BlockSpec((1, MLA_HEADS, MLA_KV_RANK), lambda i, pt: (i, 0, 0)),
        scratch_shapes=[
            pltpu.VMEM((DECODE_SLOTS, n_pages, page, MLA_KV_RANK), F32),
            pltpu.VMEM((DECODE_SLOTS, n_pages, MLA_ROPE, page), F32),
            pltpu.VMEM((MLA_HEADS, n_pages * page), F32),
            pltpu.SemaphoreType.DMA((2, DECODE_SLOTS)),
        ],
    )
    kern = functools.partial(_decode_kernel, group=math.gcd(n_pages, 8))
    return pl.pallas_call(
        kern, grid_spec=grid_spec, out_shape=jax.ShapeDtypeStruct((b, MLA_HEADS, MLA_KV_RANK), F32),
        compiler_params=_params("arbitrary"), name="mla_decode",
    )(page_table, q, key_new, cache_latent, cache_rope_t)


def _attn_out_kernel(o_ref, x_ref, wuv_ref, wo_ref, y_ref, asm_s):
    for h in range(MLA_HEADS):
        oh = o_ref[:, h * MLA_KV_RANK:(h + 1) * MLA_KV_RANK].astype(BF16)
        asm_s[:, h * MLA_V:(h + 1) * MLA_V] = _dot(oh, wuv_ref[h]).astype(BF16)
    y_ref[...] = x_ref[...] + _dot(asm_s[...], wo_ref[...])


def _attn_out_sample(o, x, w):
    m, d = x.shape
    tm = min(128, m)
    assert m % tm == 0
    ow = MLA_HEADS * MLA_KV_RANK
    return pl.pallas_call(
        _attn_out_kernel, grid=(m // tm,),
        in_specs=[pl.BlockSpec((tm, ow), lambda i: (i, 0)), pl.BlockSpec((tm, d), lambda i: (i, 0)),
                  _full(w["w_uv_t"].shape), _full(w["w_out1"].shape)],
        out_specs=pl.BlockSpec((tm, d), lambda i: (i, 0)),
        out_shape=jax.ShapeDtypeStruct((m, d), F32),
        scratch_shapes=[pltpu.VMEM((tm, MLA_HEADS * MLA_V), BF16)],
        compiler_params=_params("arbitrary"), name="attn_out",
    )(o.reshape(m, ow), x, w["w_uv_t"], w["w_out1"])


def _rope_tables(pos):
    half = MLA_ROPE // 2
    inv = ROPE_THETA ** (-jnp.arange(half, dtype=F32) * 2.0 / MLA_ROPE)
    ang = pos.astype(F32)[:, None] * inv[None, :]
    cos = jnp.cos(ang)
    sin = jnp.sin(ang)
    pad = jnp.zeros((pos.shape[0], LANES - MLA_ROPE), F32)
    return jnp.concatenate([cos, cos, pad], axis=1), jnp.concatenate([-sin, sin, pad], axis=1)


def _pad_lanes(a, width):
    return jnp.pad(a, ((0, 0), (0, width - a.shape[1])))


def _prepare_weights(norm_mix, norm_ffn, norm_final, w_in0, mlstm_b_i, mlstm_b_f, mlstm_norm, pool_w,
                     pool_scale, w_out0, w_in1, mla_q_norm, mla_kv_norm, w_q_up, w_uk, w_uv, w_out1,
                     w_up, w_down):
    d = w_in0.shape[0]
    half = MLA_ROPE // 2
    gate_i = w_in0[:, Z_MAIN:Z_MAIN + ML_HEADS]
    gate_f = w_in0[:, Z_MAIN + ML_HEADS:Z_MAIN + 2 * ML_HEADS]
    rope_k = w_in1[:, MLA_Q_RANK + MLA_KV_RANK:]
    rope_k_sw = jnp.concatenate([rope_k[:, half:], rope_k[:, :half]], axis=1)
    q_up = w_q_up.reshape(MLA_Q_RANK, MLA_HEADS, MLA_NOPE + MLA_ROPE)
    q_pe = q_up[:, :, MLA_NOPE:]
    q_pe_sw = jnp.concatenate([q_pe[:, :, half:], q_pe[:, :, :half]], axis=2)
    pad_pe = ((0, 0), (0, 0), (0, LANES - MLA_ROPE))
    col8 = lambda v: jnp.pad(v.astype(F32), (0, 8 - ML_HEADS)).reshape(8, 1)
    return {
        "g_mix0": norm_mix[0].reshape(1, d), "g_mix1": norm_mix[1].reshape(1, d),
        "g_ffn0": norm_ffn[0].reshape(1, d), "g_ffn1": norm_ffn[1].reshape(1, d),
        "g_final": norm_final.reshape(1, d),
        "w_in0_gates": _pad_lanes(w_in0, Z_MAIN + LANES).astype(BF16),
        "w_gate_i_t": jnp.pad(gate_i.T, ((0, 8 - ML_HEADS), (0, 0))).astype(BF16),
        "w_gate_f_t": jnp.pad(gate_f.T, ((0, 8 - ML_HEADS), (0, 0))).astype(BF16),
        "b_i_col": col8(mlstm_b_i), "b_f_col": col8(mlstm_b_f),
        "b_i_row": mlstm_b_i.reshape(1, ML_HEADS), "b_f_row": mlstm_b_f.reshape(1, ML_HEADS),
        "mlstm_norm": mlstm_norm.reshape(1, ML_WIDTH),
        "pool_w": pool_w.astype(BF16), "pool_scale": pool_scale.reshape(1, POOL_WIDTH),
        "w_out0": w_out0.astype(BF16),
        "w_in1": jnp.concatenate(
            [w_in1[:, :MLA_Q_RANK + MLA_KV_RANK], _pad_lanes(rope_k, LANES), _pad_lanes(rope_k_sw, LANES)],
            axis=1).astype(BF16),
        "mla_q_norm": mla_q_norm.reshape(1, MLA_Q_RANK), "mla_kv_norm": mla_kv_norm.reshape(1, MLA_KV_RANK),
        "w_q_nope": q_up[:, :, :MLA_NOPE].reshape(MLA_Q_RANK, MLA_HEADS * MLA_NOPE).astype(BF16),
        "w_q_pe": jnp.pad(q_pe, pad_pe).reshape(MLA_Q_RANK, MLA_HEADS * LANES).astype(BF16),
        "w_q_pe_sw": jnp.pad(q_pe_sw, pad_pe).reshape(MLA_Q_RANK, MLA_HEADS * LANES).astype(BF16),
        "w_uk_h": jnp.transpose(w_uk, (1, 0, 2)).astype(BF16),
        "w_uv_t": jnp.transpose(w_uv, (1, 0, 2)).astype(BF16),
        "w_uv_h": jnp.transpose(w_uv, (1, 2, 0)).astype(BF16),
        "w_out1": w_out1.astype(BF16),
        "w_up": w_up.astype(BF16), "w_down": w_down.astype(BF16),
    }


def kernel(x_prompt, x_sample, state_mlstm_C, state_mlstm_n, state_mlstm_m, state_pool, cache_latent, cache_rope_k, page_table, norm_mix, norm_ffn, norm_final, w_in0, mlstm_b_i, mlstm_b_f, mlstm_norm, pool_w, pool_scale, w_out0, w_in1, mla_q_norm, mla_kv_norm, w_q_up, w_uk, w_uv, w_out1, w_up, w_down):
    w = _prepare_weights(norm_mix, norm_ffn, norm_final, w_in0, mlstm_b_i, mlstm_b_f, mlstm_norm, pool_w,
                         pool_scale, w_out0, w_in1, mla_q_norm, mla_kv_norm, w_q_up, w_uk, w_uv, w_out1,
                         w_up, w_down)
    bp, t, d = x_prompt.shape
    bs, ts, _ = x_sample.shape
    assert ts == 1
    past_len = page_table.shape[1] * cache_latent.shape[1]

    x1, c_p, n_p, m_p, buf_p = _mixer0_prompt(x_prompt, w, 0)
    x2 = _ffn(x1.reshape(bp * t, d), w["g_ffn0"], w["w_up"], w["w_down"], 0, w["g_final"], False)
    x2 = x2.reshape(bp, t, d)
    ckv_p, kpe_p, keys_p, vt_p, qt_p = _mla_proj(x2, w, jnp.arange(t))
    x3 = _mla_attend_prompt(qt_p, keys_p, vt_p, x2, w)
    y_p = _ffn(x3.reshape(bp * t, d), w["g_ffn1"], w["w_up"], w["w_down"], 1, w["g_final"], True)
    y_p = y_p.reshape(bp, t, d)

    xs = x_sample.reshape(bs, d)
    xs1, c_s, n_s, m_s, buf_s = _mixer0_sample(xs, state_mlstm_C, state_mlstm_n, state_mlstm_m, state_pool,
                                               w, past_len)
    xs2 = _ffn(xs1, w["g_ffn0"], w["w_up"], w["w_down"], 0, w["g_final"], False)
    ckv_s, kpe_s, keys_s, _, qt_s = _mla_proj(xs2.reshape(1, bs, d), w, jnp.full((bs,), past_len))
    assert qt_s.shape == (1, 1, MLA_KEY_WIDTH, MLA_HEADS * bs)
    q_s = jnp.transpose(qt_s.reshape(MLA_KEY_WIDTH, MLA_HEADS, bs), (2, 1, 0))
    rope_t = jnp.transpose(cache_rope_k, (0, 2, 1))
    o_s = _mla_attend_sample(q_s, keys_s.reshape(bs, 1, MLA_KEY_WIDTH), cache_latent, rope_t, page_table)
    xs3 = _attn_out_sample(o_s, xs2, w)
    y_s = _ffn(xs3, w["g_ffn1"], w["w_up"], w["w_down"], 1, w["g_final"], True)

    return (y_p, y_s.reshape(bs, 1, d), c_p, n_p, m_p, buf_p, ckv_p, kpe_p,
            c_s, n_s, m_s, buf_s, ckv_s.reshape(bs, 1, MLA_KV_RANK), kpe_s.reshape(bs, 1, MLA_ROPE))
```

```python
import functools
import math

import jax
import jax.numpy as jnp
from jax import lax
from jax.experimental import pallas as pl
from jax.experimental.pallas import tpu as pltpu

F32 = jnp.float32
BF16 = jnp.bfloat16

EPS = 1e-6
ML_HEADS = 4
ML_HEAD_DIM = 128
ML_WIDTH = ML_HEADS * ML_HEAD_DIM
POOL_WINDOWS = (2, 4, 8, 16)
POOL_GROUP = 128
POOL_WIDTH = len(POOL_WINDOWS) * POOL_GROUP
POOL_BUF = max(POOL_WINDOWS) - 1
POOL_HIST = POOL_BUF + 1
POOL_KEEP = 2 * POOL_HIST
MLA_HEADS = 8
MLA_NOPE = 128
MLA_ROPE = 64
MLA_V = 128
MLA_Q_RANK = 512
MLA_KV_RANK = 256
MLA_SCALE = (MLA_NOPE + MLA_ROPE) ** -0.5
MLA_SCALE_LOG2E = MLA_SCALE * math.log2(math.e)
ROPE_THETA = 10000.0
LANES = 128
MLA_KEY_WIDTH = MLA_KV_RANK + LANES
Z_MAIN = 4 * ML_WIDTH + POOL_WIDTH
MLA_TOKEN_TILE = 512
DECODE_SLOTS = 3

VMEM_LIMIT_BYTES = 48 * 1024 * 1024
MIXER0_VMEM_LIMIT_BYTES = 56 * 1024 * 1024

_NT = (((1,), (1,)), ((), ()))
_TN = (((0,), (0,)), ((), ()))


def _dot(a, b):
    return jnp.dot(a, b, preferred_element_type=F32)


def _dot_nt(a, b):
    return lax.dot_general(a, b, _NT, preferred_element_type=F32)


def _dot_tn(a, b):
    return lax.dot_general(a, b, _TN, preferred_element_type=F32)


def _rms(x, g):
    return x * lax.rsqrt(jnp.mean(x * x, axis=-1, keepdims=True) + EPS) * g


def _log_sigmoid(x):
    return jnp.minimum(x, 0.0) - jnp.log1p(jnp.exp(-jnp.abs(x)))


def _head_norm(h, g):
    hc = h - jnp.mean(h, axis=-1, keepdims=True)
    return hc * lax.rsqrt(jnp.mean(hc * hc, axis=-1, keepdims=True) + EPS) * g


def _cumsum_lanes(x):
    n = x.shape[-1]
    lane = lax.broadcasted_iota(jnp.int32, x.shape, x.ndim - 1)
    s = 1
    while s < n:
        x = x + jnp.where(lane >= s, pltpu.roll(x, s, axis=x.ndim - 1), 0.0)
        s *= 2
    return x


def _params(*semantics, vmem_limit_bytes=VMEM_LIMIT_BYTES):
    return pltpu.CompilerParams(dimension_semantics=semantics, vmem_limit_bytes=vmem_limit_bytes)


def _full(shape):
    return pl.BlockSpec(shape, lambda *_: (0,) * len(shape))


def _in_proj(x, w_in, z_out, qkv_out, gi_out, gf_out):
    g_ref, win_ref, wgi_ref, wgf_ref, bi_ref, bf_ref = w_in
    hn = _rms(x(), g_ref[...]).astype(BF16)
    z = _dot(hn, win_ref[...])
    z_out[...] = z
    qkv_out[:, 0:ML_WIDTH] = z[:, 0:ML_WIDTH].astype(BF16)
    qkv_out[:, ML_WIDTH:2 * ML_WIDTH] = (z[:, ML_WIDTH:2 * ML_WIDTH] * (ML_HEAD_DIM ** -0.5)).astype(BF16)
    qkv_out[:, 2 * ML_WIDTH:3 * ML_WIDTH] = z[:, 2 * ML_WIDTH:3 * ML_WIDTH].astype(BF16)
    gi_out[...] = _dot_nt(wgi_ref[...], hn) + bi_ref[...]
    gf_out[...] = _log_sigmoid(_dot_nt(wgf_ref[...], hn) + bf_ref[...])


def _mixer0_tile(x, cur, nxt, w_in, gn_ref, pw_ref, ps_ref, wout_ref, asm_s, ext_s, lvl_s, c_s, n_s, m_s,
                 *, chunk, pos):
    z_s, qkv_s, gi_s, gf_s = cur
    tt = z_s.shape[0]
    dh = ML_HEAD_DIM
    units = [(c, h) for c in range(tt // chunk) for h in range(ML_HEADS)]
    rows_of = lambda c: slice(c * chunk, (c + 1) * chunk)
    head = lambda part, h: slice(part * ML_WIDTH + h * dh, part * ML_WIDTH + (h + 1) * dh)
    row = lax.broadcasted_iota(jnp.int32, (chunk, chunk), 0)
    col = lax.broadcasted_iota(jnp.int32, (chunk, chunk), 1)
    causal = row >= col
    diag = row == col

    gates = []
    m_prev8 = m_s[:, 0:1]
    for c in range(tt // chunk):
        bcum = _cumsum_lanes(gf_s[:, rows_of(c)])
        a = gi_s[:, rows_of(c)] - bcum
        m_last8 = jnp.maximum(m_prev8, jnp.max(a, axis=1, keepdims=True))
        gates.append((a, bcum, m_prev8, m_last8, jnp.exp(m_prev8 - m_last8)))
        m_prev8 = bcum[:, chunk - 1:chunk] + m_last8
    m_s[...] = jnp.broadcast_to(m_prev8, m_s.shape)

    scores = {}
    for c, h in units:
        scores[c, h] = _dot_nt(qkv_s[rows_of(c), head(0, h)], qkv_s[rows_of(c), head(1, h)])

    _in_proj(nxt[0], w_in, *nxt[1])

    parts = {}
    for c, h in units:
        a, bcum, m_prev8, m_last8, _ = gates[c]
        a_row = a[h:h + 1, :]
        m_prev = m_prev8[h:h + 1, :]
        a_mat = jnp.where(causal, a_row, -jnp.inf)
        m_col = jnp.maximum(jnp.max(a_mat, axis=1, keepdims=True), m_prev)
        b_col = jnp.sum(jnp.where(diag, bcum[h:h + 1, :], 0.0), axis=1, keepdims=True)
        a_col = jnp.sum(jnp.where(diag, a_row, 0.0), axis=1, keepdims=True)
        s = scores[c, h] * jnp.exp(a_mat - m_col)
        vb = qkv_s[rows_of(c), head(2, h)]
        kw = z_s[rows_of(c), head(1, h)] * (dh ** -0.5) * jnp.exp(a_col - m_last8[h:h + 1, :])
        parts[c, h] = (_dot(s.astype(BF16), vb), jnp.sum(s, axis=1, keepdims=True),
                       jnp.exp(m_prev - m_col), jnp.exp(-(b_col + m_col)),
                       _dot_tn(vb, kw.astype(BF16)), jnp.sum(kw, axis=0, keepdims=True))

    for c, h in units:
        num_intra, den_intra, w_inter, e_neg_m, c_add, n_add = parts[c, h]
        w_old = gates[c][4][h:h + 1, :]
        q = z_s[rows_of(c), head(0, h)]
        c_mat = c_s[h]
        n_row = n_s[h:h + 1, :]
        num = num_intra + w_inter * _dot_nt(qkv_s[rows_of(c), head(0, h)], c_mat.astype(BF16))
        den = den_intra + w_inter * jnp.sum(q * n_row, axis=1, keepdims=True)
        hh = num / jnp.maximum(jnp.abs(den), e_neg_m)
        c_s[h] = w_old * c_mat + c_add
        n_s[h:h + 1, :] = w_old * n_row + n_add
        o_gate = jax.nn.sigmoid(z_s[rows_of(c), head(3, h)])
        hm = _head_norm(hh, gn_ref[:, h * dh:(h + 1) * dh]) * o_gate
        asm_s[rows_of(c), h * dh:(h + 1) * dh] = hm.astype(BF16)

    keep = POOL_KEEP
    end = keep + tt
    ext_s[keep:end, :] = z_s[:, 4 * ML_WIDTH:Z_MAIN]
    for g, win in enumerate(POOL_WINDOWS):
        lanes = slice(g * POOL_GROUP, (g + 1) * POOL_GROUP)
        src, src_lanes = ext_s, lanes
        width, lo, level = 1, 0, 0
        while 2 * width < win:
            lo += 8
            dst = lvl_s.at[level % 2]
            dst[lo:end, :] = src[lo:end, src_lanes] + src[lo - width:end - width, src_lanes]
            src, src_lanes = dst, slice(0, POOL_GROUP)
            width, level = 2 * width, level + 1
        wsum = src[keep:end, src_lanes] + src[keep - width:end - width, src_lanes]
        u = ext_s[keep:end, lanes]
        cnt = jnp.minimum(win, pos + 1).astype(F32)
        pooled = wsum / cnt - u
        og = _dot(pooled.astype(BF16), pw_ref[g]) * ps_ref[:, lanes]
        asm_s[:, ML_WIDTH + g * POOL_GROUP:ML_WIDTH + (g + 1) * POOL_GROUP] = og.astype(BF16)
    ext_s[0:keep, :] = ext_s[tt:end, :]

    return x() + _dot(asm_s[...], wout_ref[...])


def _mixer0_kernel(x_ref, xn_ref, g_ref, win_ref, wgi_ref, wgf_ref, bi_ref, bf_ref, gn_ref, pw_ref, ps_ref,
                   wout_ref, y_ref, c_out, n_out, m_out, buf_out,
                   za_s, qa_s, gia_s, gfa_s, zb_s, qb_s, gib_s, gfb_s, asm_s, ext_s, lvl_s, c_s, n_s, m_s,
                   *, tt, chunk, pos0):
    b = pl.program_id(0)
    i = pl.program_id(1)
    w_in = (g_ref, win_ref, wgi_ref, wgf_ref, bi_ref, bf_ref)
    set_a = (za_s, qa_s, gia_s, gfa_s)
    set_b = (zb_s, qb_s, gib_s, gfb_s)

    @pl.when((b == 0) & (i == 0))
    def _():
        _in_proj(lambda: x_ref[0, 0:tt, :], w_in, *set_a)

    @pl.when(i == 0)
    def _():
        c_s[...] = jnp.zeros_like(c_s)
        n_s[...] = jnp.zeros_like(n_s)
        m_s[...] = jnp.zeros_like(m_s)
        ext_s[0:POOL_KEEP, :] = jnp.zeros((POOL_KEEP, POOL_WIDTH), F32)

    tile = functools.partial(_mixer0_tile, w_in=w_in, gn_ref=gn_ref, pw_ref=pw_ref, ps_ref=ps_ref,
                             wout_ref=wout_ref, asm_s=asm_s, ext_s=ext_s, lvl_s=lvl_s, c_s=c_s, n_s=n_s,
                             m_s=m_s, chunk=chunk)
    pos = pos0 + 2 * i * tt + lax.broadcasted_iota(jnp.int32, (tt, 1), 0)
    x0 = lambda: x_ref[0, 0:tt, :]
    x1 = lambda: x_ref[0, tt:2 * tt, :]
    y_ref[0, 0:tt, :] = tile(x0, set_a, (x1, set_b), pos=pos)
    y_ref[0, tt:2 * tt, :] = tile(x1, set_b, (lambda: xn_ref[0], set_a), pos=pos + tt)

    @pl.when(i == pl.num_programs(1) - 1)
    def _():
        c_out[0] = c_s[...]
        n_out[0] = n_s[0:ML_HEADS, :]
        m_out[0] = m_s[...]
        buf_out[0] = ext_s[POOL_KEEP - POOL_HIST:POOL_KEEP, :]


def _mixer0_prompt(x, w, pos0):
    b, t, d = x.shape
    tt = min(512, t // 2)
    chunk = min(256, tt)
    assert t % (2 * tt) == 0 and tt % chunk == 0 and tt >= POOL_KEEP
    n_pairs = t // (2 * tt)
    kern = functools.partial(_mixer0_kernel, tt=tt, chunk=chunk, pos0=pos0)

    def next_tile(i, j):
        nxt = jnp.minimum(i * n_pairs + j + 1, b * n_pairs - 1)
        return (nxt // n_pairs, 2 * (nxt % n_pairs), 0)

    out_shape = (
        jax.ShapeDtypeStruct((b, t, d), F32),
        jax.ShapeDtypeStruct((b, ML_HEADS, ML_HEAD_DIM, ML_HEAD_DIM), F32),
        jax.ShapeDtypeStruct((b, ML_HEADS, ML_HEAD_DIM), F32),
        jax.ShapeDtypeStruct((b, 8, LANES), F32),
        jax.ShapeDtypeStruct((b, POOL_HIST, POOL_WIDTH), F32),
    )
    in_specs = [
        pl.BlockSpec((1, 2 * tt, d), lambda i, j: (i, j, 0)),
        pl.BlockSpec((1, tt, d), next_tile),
        _full((1, d)),
        pl.BlockSpec((d, Z_MAIN), lambda i, j: (0, 0), pipeline_mode=pl.Buffered(1)),
        _full((8, d)),
        _full((8, d)),
        _full((8, 1)),
        _full((8, 1)),
        _full((1, ML_WIDTH)),
        _full((len(POOL_WINDOWS), POOL_GROUP, POOL_GROUP)),
        _full((1, POOL_WIDTH)),
        pl.BlockSpec((ML_WIDTH + POOL_WIDTH, d), lambda i, j: (0, 0), pipeline_mode=pl.Buffered(1)),
    ]
    out_specs = (
        pl.BlockSpec((1, 2 * tt, d), lambda i, j: (i, j, 0)),
        pl.BlockSpec((1, ML_HEADS, ML_HEAD_DIM, ML_HEAD_DIM), lambda i, j: (i, 0, 0, 0)),
        pl.BlockSpec((1, ML_HEADS, ML_HEAD_DIM), lambda i, j: (i, 0, 0)),
        pl.BlockSpec((1, 8, LANES), lambda i, j: (i, 0, 0)),
        pl.BlockSpec((1, POOL_HIST, POOL_WIDTH), lambda i, j: (i, 0, 0)),
    )
    proj_set = [pltpu.VMEM((tt, Z_MAIN), F32), pltpu.VMEM((tt, 3 * ML_WIDTH), BF16),
                pltpu.VMEM((8, tt), F32), pltpu.VMEM((8, tt), F32)]
    scratch = proj_set + proj_set + [
        pltpu.VMEM((tt, ML_WIDTH + POOL_WIDTH), BF16),
        pltpu.VMEM((tt + POOL_KEEP, POOL_WIDTH), F32),
        pltpu.VMEM((2, tt + POOL_KEEP, POOL_GROUP), F32),
        pltpu.VMEM((ML_HEADS, ML_HEAD_DIM, ML_HEAD_DIM), F32),
        pltpu.VMEM((8, ML_HEAD_DIM), F32),
        pltpu.VMEM((8, LANES), F32),
    ]
    y, c, n, m, buf = pl.pallas_call(
        kern, grid=(b, n_pairs), in_specs=in_specs, out_specs=out_specs, out_shape=out_shape,
        scratch_shapes=scratch, name="mixer0_prompt",
        compiler_params=_params("arbitrary", "arbitrary", vmem_limit_bytes=MIXER0_VMEM_LIMIT_BYTES),
    )(x, x, w["g_mix0"], w["w_in0_gates"], w["w_gate_i_t"], w["w_gate_f_t"], w["b_i_col"], w["b_f_col"],
      w["mlstm_norm"], w["pool_w"], w["pool_scale"], w["w_out0"])
    return y, c, n, m[:, :ML_HEADS, 0], buf[:, 1:, :]


def _norm_matmul_kernel(x_ref, g_ref, w_ref, o_ref):
    o_ref[...] = _dot(_rms(x_ref[...], g_ref[...]).astype(BF16), w_ref[...])


def _norm_matmul(x, g, w_bf16):
    m, d = x.shape
    n = w_bf16.shape[1]
    tm = min(128, m)
    assert m % tm == 0
    return pl.pallas_call(
        _norm_matmul_kernel, grid=(m // tm,),
        in_specs=[pl.BlockSpec((tm, d), lambda i: (i, 0)), _full((1, d)), _full((d, n))],
        out_specs=pl.BlockSpec((tm, n), lambda i: (i, 0)),
        out_shape=jax.ShapeDtypeStruct((m, n), F32),
        compiler_params=_params("arbitrary"), name="norm_matmul",
    )(x, g, w_bf16)


def _mixer0_step_kernel(z_ref, x_ref, c0_ref, n0_ref, m0_ref, buf_ref, bi_ref, bf_ref, gn_ref, pw_ref,
                        ps_ref, wout_ref, y_ref, c_ref, n_ref, m_ref, bufo_ref, cq_s, asm_s, *, bb, cnts):
    dh = ML_HEAD_DIM
    gate_i = Z_MAIN
    gate_f = Z_MAIN + ML_HEADS
    ig = z_ref[:, gate_i:gate_i + ML_HEADS] + bi_ref[...]
    lf = _log_sigmoid(z_ref[:, gate_f:gate_f + ML_HEADS] + bf_ref[...])
    m0 = m0_ref[...]
    m_t = jnp.maximum(lf + m0, ig)
    w_i = jnp.exp(ig - m_t)
    w_f = jnp.exp(lf + m0 - m_t)
    e_m = jnp.exp(-m_t)
    m_ref[...] = m_t

    row8 = lax.broadcasted_iota(jnp.int32, (8, dh), 0) == 0
    for h in range(ML_HEADS):
        q = z_ref[:, h * dh:(h + 1) * dh]
        k = z_ref[:, ML_WIDTH + h * dh:ML_WIDTH + (h + 1) * dh] * (dh ** -0.5)
        v = z_ref[:, 2 * ML_WIDTH + h * dh:2 * ML_WIDTH + (h + 1) * dh]
        wi = w_i[:, h:h + 1]
        wf = w_f[:, h:h + 1]
        kw = k * wi
        for b in range(bb):
            c_mat = c0_ref[b, h]
            q8 = jnp.broadcast_to(q[b:b + 1, :], (8, dh))
            cq_s[b:b + 1, :] = _dot_nt(q8, c_mat)[0:1, :]
            v8 = jnp.where(row8, v[b:b + 1, :], 0.0)
            k8 = jnp.where(row8, kw[b:b + 1, :], 0.0)
            c_ref[b, h] = wf[b:b + 1, :] * c_mat + _dot_tn(v8, k8)
        n_old = n0_ref[:, h, :]
        s = jnp.sum(q * k, axis=1, keepdims=True) * wi
        num = s * v + wf * cq_s[...]
        den = s + wf * jnp.sum(n_old * q, axis=1, keepdims=True)
        hh = num / jnp.maximum(jnp.abs(den), e_m[:, h:h + 1])
        n_ref[:, h, :] = wf * n_old + kw
        o_gate = jax.nn.sigmoid(z_ref[:, 3 * ML_WIDTH + h * dh:3 * ML_WIDTH + (h + 1) * dh])
        hm = _head_norm(hh, gn_ref[:, h * dh:(h + 1) * dh]) * o_gate
        asm_s[:, h * dh:(h + 1) * dh] = hm

    u_all = z_ref[:, 4 * ML_WIDTH:Z_MAIN]
    for g, win in enumerate(POOL_WINDOWS):
        u = u_all[:, g * POOL_GROUP:(g + 1) * POOL_GROUP]
        wsum = u
        for j in range(1, win):
            off = (POOL_BUF - j) * POOL_WIDTH + g * POOL_GROUP
            wsum = wsum + buf_ref[:, off:off + POOL_GROUP]
        pooled = wsum / cnts[g] - u
        og = _dot(pooled.astype(BF16), pw_ref[g]) * ps_ref[:, g * POOL_GROUP:(g + 1) * POOL_GROUP]
        asm_s[:, ML_WIDTH + g * POOL_GROUP:ML_WIDTH + (g + 1) * POOL_GROUP] = og
    keep = (POOL_BUF - 1) * POOL_WIDTH
    bufo_ref[:, 0:keep] = buf_ref[:, POOL_WIDTH:POOL_WIDTH + keep]
    bufo_ref[:, keep:keep + POOL_WIDTH] = u_all

    y_ref[...] = x_ref[...] + _dot(asm_s[...].astype(BF16), wout_ref[...])


def _mixer0_sample(x, c0, n0, m0, buf0, w, pos0):
    b, d = x.shape
    z = _norm_matmul(x, w["g_mix0"], w["w_in0_gates"])
    zw = z.shape[1]
    bb = 16 if b % 16 == 0 else 8
    assert b % bb == 0
    cnts = tuple(float(min(win, pos0 + 1)) for win in POOL_WINDOWS)
    bufw = POOL_BUF * POOL_WIDTH
    kern = functools.partial(_mixer0_step_kernel, bb=bb, cnts=cnts)
    hd = (ML_HEADS, ML_HEAD_DIM)
    in_specs = [
        pl.BlockSpec((bb, zw), lambda i: (i, 0)),
        pl.BlockSpec((bb, d), lambda i: (i, 0)),
        pl.BlockSpec((bb,) + hd + (ML_HEAD_DIM,), lambda i: (i, 0, 0, 0)),
        pl.BlockSpec((bb,) + hd, lambda i: (i, 0, 0)),
        pl.BlockSpec((bb, ML_HEADS), lambda i: (i, 0)),
        pl.BlockSpec((bb, bufw), lambda i: (i, 0)),
        _full((1, ML_HEADS)),
        _full((1, ML_HEADS)),
        _full((1, ML_WIDTH)),
        _full((len(POOL_WINDOWS), POOL_GROUP, POOL_GROUP)),
        _full((1, POOL_WIDTH)),
        _full((ML_WIDTH + POOL_WIDTH, d)),
    ]
    out_specs = (
        pl.BlockSpec((bb, d), lambda i: (i, 0)),
        pl.BlockSpec((bb,) + hd + (ML_HEAD_DIM,), lambda i: (i, 0, 0, 0)),
        pl.BlockSpec((bb,) + hd, lambda i: (i, 0, 0)),
        pl.BlockSpec((bb, ML_HEADS), lambda i: (i, 0)),
        pl.BlockSpec((bb, bufw), lambda i: (i, 0)),
    )
    out_shape = (
        jax.ShapeDtypeStruct((b, d), F32),
        jax.ShapeDtypeStruct((b,) + hd + (ML_HEAD_DIM,), F32),
        jax.ShapeDtypeStruct((b,) + hd, F32),
        jax.ShapeDtypeStruct((b, ML_HEADS), F32),
        jax.ShapeDtypeStruct((b, bufw), F32),
    )
    scratch = [pltpu.VMEM((bb, ML_HEAD_DIM), F32), pltpu.VMEM((bb, ML_WIDTH + POOL_WIDTH), F32)]
    y, c, n, m, buf = pl.pallas_call(
        kern, grid=(b // bb,), in_specs=in_specs, out_specs=out_specs, out_shape=out_shape,
        scratch_shapes=scratch, compiler_params=_params("arbitrary"), name="mixer0_sample",
    )(z, x, c0, n0, m0, buf0.reshape(b, bufw), w["b_i_row"], w["b_f_row"], w["mlstm_norm"],
      w["pool_w"], w["pool_scale"], w["w_out0"])
    return y, c, n, m, buf.reshape(b, POOL_BUF, POOL_WIDTH)


def _ffn_kernel(x_ref, g_ref, wu_ref, wd_ref, gf_ref, y_ref, hn_s, acc_s, *, ts, tf, final_norm, f_steps):
    n_sub = x_ref.shape[0] // ts
    j = pl.program_id(1)

    def norm():
        for s in range(n_sub):
            hn_s[s] = _rms(x_ref[s * ts:(s + 1) * ts, :], g_ref[...]).astype(BF16)

    def mlp(s, assign_first):
        for f in range(wu_ref.shape[1] // tf):
            a = jnp.maximum(_dot(hn_s[s], wu_ref[:, f * tf:(f + 1) * tf]), 0.0)
            part = _dot((a * a).astype(BF16), wd_ref[f * tf:(f + 1) * tf, :])
            if f == 0 and assign_first:
                acc_s[s] = part
            else:
                acc_s[s] += part

    def finish(s):
        rows = slice(s * ts, (s + 1) * ts)
        y = x_ref[rows, :] + acc_s[s]
        if final_norm:
            y = _rms(y, gf_ref[...])
        y_ref[rows, :] = y

    if f_steps == 1:
        norm()
        for s in range(n_sub):
            mlp(s, True)
            finish(s)
    else:
        @pl.when(j == 0)
        def _():
            norm()
            acc_s[...] = jnp.zeros_like(acc_s)

        for s in range(n_sub):
            mlp(s, False)

        @pl.when(j == f_steps - 1)
        def _():
            for s in range(n_sub):
                finish(s)


def _ffn(x, g, w_up, w_down, layer, g_final, final_norm):
    m, d = x.shape
    dff = w_up.shape[2]
    n_sub = 2 if m % 16 == 0 else 1
    ts = min(512, m // n_sub)
    tf = min(1024, dff)
    tm = n_sub * ts
    assert m % tm == 0 and dff % tf == 0 and ts % 8 == 0
    f_steps = dff // tf if m == tm else 1
    fw = dff // f_steps
    kern = functools.partial(_ffn_kernel, ts=ts, tf=tf, final_norm=final_norm, f_steps=f_steps)
    mode = pl.Buffered(1) if f_steps == 1 else None
    return pl.pallas_call(
        kern, grid=(m // tm, f_steps),
        in_specs=[
            pl.BlockSpec((tm, d), lambda i, j: (i, 0)),
            _full((1, d)),
            pl.BlockSpec((None, d, fw), lambda i, j: (layer, 0, j), pipeline_mode=mode),
            pl.BlockSpec((None, fw, d), lambda i, j: (layer, j, 0), pipeline_mode=mode),
            _full((1, d)),
        ],
        out_specs=pl.BlockSpec((tm, d), lambda i, j: (i, 0)),
        out_shape=jax.ShapeDtypeStruct((m, d), F32),
        scratch_shapes=[pltpu.VMEM((n_sub, ts, d), BF16), pltpu.VMEM((n_sub, ts, d), F32)],
        compiler_params=_params("arbitrary", "arbitrary"), name="ffn",
    )(x, g, w_up, w_down, g_final)


def _query_parts(tokens):
    return 2 if tokens % (2 * LANES) == 0 else 1


def _mla_proj_kernel(x_ref, g_ref, win_ref, gq_ref, gkv_ref, wqn_ref, wqp_ref, wuk_ref,
                     cos_ref, sin_ref, cos2_ref, sin2_ref, ckv_ref, kpe_ref, key_ref, vt_ref, qt_ref):
    r = MLA_KV_RANK
    tm = x_ref.shape[1]
    hn = _rms(x_ref[0], g_ref[...]).astype(BF16)
    z = _dot(hn, win_ref[...])
    cq = _rms(z[:, 0:MLA_Q_RANK], gq_ref[...]).astype(BF16)
    ckv = _rms(z[:, MLA_Q_RANK:MLA_Q_RANK + r], gkv_ref[...])
    cos = cos_ref[...]
    sin = sin_ref[...]
    kpe = z[:, MLA_Q_RANK + r:MLA_Q_RANK + r + LANES] * cos + z[:, MLA_Q_RANK + r + LANES:] * sin
    ckv_ref[0] = ckv
    kpe_ref[0] = kpe[:, 0:MLA_ROPE]
    key_ref[0, :, 0:r] = ckv.astype(BF16)
    key_ref[0, :, r:r + LANES] = kpe.astype(BF16)
    vt_ref[0, 0] = ckv.T.astype(BF16)
    qn = _dot(cq, wqn_ref[...])
    qpe = _dot(cq, wqp_ref[...])
    n_parts = _query_parts(tm)
    pt = tm // n_parts
    hw = MLA_HEADS * MLA_ROPE
    qt_ref[0, 0, r + MLA_ROPE:r + LANES, :] = jnp.zeros((LANES - MLA_ROPE, MLA_HEADS * tm), BF16)

    def store(h, rows, value):
        for part in range(n_parts):
            cols = slice((part * MLA_HEADS + h) * pt, (part * MLA_HEADS + h + 1) * pt)
            qt_ref[0, 0, rows, cols] = value[:, part * pt:(part + 1) * pt]

    for h in range(MLA_HEADS):
        lanes = slice(h * LANES, (h + 1) * LANES)
        store(h, slice(0, r), _dot_nt(wuk_ref[h], qn[:, lanes].astype(BF16)).astype(BF16))
    for pair in range(MLA_HEADS // 2):
        lanes = slice(pair * LANES, (pair + 1) * LANES)
        rot = qpe[:, lanes] * cos2_ref[...] + qpe[:, hw + pair * LANES:hw + (pair + 1) * LANES] * sin2_ref[...]
        rot_t = rot.T.astype(BF16)
        for i in range(2):
            store(2 * pair + i, slice(r, r + MLA_ROPE), rot_t[i * MLA_ROPE:(i + 1) * MLA_ROPE, :])


def _mla_proj(x, w, pos):
    nb, t, d = x.shape
    tm = min(MLA_TOKEN_TILE, t)
    assert t % tm == 0
    nt = t // tm
    kw = MLA_KEY_WIDTH
    cos, sin, cos2, sin2 = _rope_tables(pos)
    out_shape = (
        jax.ShapeDtypeStruct((nb, t, MLA_KV_RANK), F32),
        jax.ShapeDtypeStruct((nb, t, MLA_ROPE), F32),
        jax.ShapeDtypeStruct((nb, t, kw), BF16),
        jax.ShapeDtypeStruct((nb, nt, MLA_KV_RANK, tm), BF16),
        jax.ShapeDtypeStruct((nb, nt, kw, MLA_HEADS * tm), BF16),
    )
    in_specs = [
        pl.BlockSpec((1, tm, d), lambda i, j: (i, j, 0)),
        _full((1, d)),
        _full(w["w_in1"].shape),
        _full((1, MLA_Q_RANK)),
        _full((1, MLA_KV_RANK)),
        _full(w["w_q_nope"].shape),
        _full(w["w_q_pe"].shape),
        _full(w["w_uk_h"].shape),
        pl.BlockSpec((tm, LANES), lambda i, j: (j, 0)),
        pl.BlockSpec((tm, LANES), lambda i, j: (j, 0)),
        pl.BlockSpec((tm, LANES), lambda i, j: (j, 0)),
        pl.BlockSpec((tm, LANES), lambda i, j: (j, 0)),
    ]
    out_specs = (
        pl.BlockSpec((1, tm, MLA_KV_RANK), lambda i, j: (i, j, 0)),
        pl.BlockSpec((1, tm, MLA_ROPE), lambda i, j: (i, j, 0)),
        pl.BlockSpec((1, tm, kw), lambda i, j: (i, j, 0)),
        pl.BlockSpec((1, 1, MLA_KV_RANK, tm), lambda i, j: (i, j, 0, 0)),
        pl.BlockSpec((1, 1, kw, MLA_HEADS * tm), lambda i, j: (i, j, 0, 0)),
    )
    return pl.pallas_call(
        _mla_proj_kernel, grid=(nb, nt), in_specs=in_specs, out_specs=out_specs, out_shape=out_shape,
        compiler_params=_params("arbitrary", "arbitrary"), name="mla_proj",
    )(x, w["g_mix1"], w["w_in1"], w["mla_q_norm"], w["mla_kv_norm"], w["w_q_nope"], w["w_q_pe"],
      w["w_uk_h"], cos, sin, cos2, sin2)


def _flash_kernel(qt_ref, k_ref, vt_ref, x_ref, wuv_ref, wo_ref, y_ref, acc_s, m_s, l_s, asm_s, *, tq):
    qi = pl.program_id(1)
    tk = tq
    rows = MLA_HEADS * tq
    n_parts = _query_parts(tq)
    pt = tq // n_parts
    m_s[...] = jnp.full_like(m_s, -jnp.inf)
    l_s[...] = jnp.zeros_like(l_s)
    acc_s[...] = jnp.zeros_like(acc_s)

    def block(kb, key_lo, n_keys, first_part, masked):
        start = pl.multiple_of(kb * tk, tk)
        cols = slice(first_part * MLA_HEADS * pt, rows)
        n_cols = rows - cols.start
        s = _dot(k_ref[0, pl.ds(start + key_lo, n_keys), :], qt_ref[0, 0, :, cols])
        if masked:
            key = key_lo + lax.broadcasted_iota(jnp.int32, (n_keys, n_cols), 0)
            col = cols.start + lax.broadcasted_iota(jnp.int32, (n_keys, n_cols), 1)
            part_of_col = lax.shift_right_logical(col, (MLA_HEADS * pt).bit_length() - 1)
            tok = part_of_col * pt + (col & (pt - 1))
            s = jnp.where(key <= tok, s, -jnp.inf)
        m_old = m_s[:, cols]
        m_new = jnp.maximum(m_old, jnp.max(s, axis=0, keepdims=True))
        alpha = jnp.exp2((m_old - m_new) * MLA_SCALE_LOG2E)
        p = jnp.exp2((s - m_new) * MLA_SCALE_LOG2E)
        l_s[:, cols] = alpha * l_s[:, cols] + jnp.sum(p, axis=0, keepdims=True)
        acc_s[:, cols] = alpha * acc_s[:, cols] + _dot(vt_ref[0, kb][:, key_lo:key_lo + n_keys], p.astype(BF16))
        m_s[:, cols] = m_new

    def body(kb, carry):
        block(kb, 0, tk, 0, False)
        return carry

    lax.fori_loop(0, qi, body, 0)
    for part in range(n_parts):
        block(qi, part * pt, pt, part, True)
    o_t = acc_s[...] / l_s[...]
    for h in range(MLA_HEADS):
        for part in range(n_parts):
            c0 = (part * MLA_HEADS + h) * pt
            oh = o_t[:, c0:c0 + pt].astype(BF16)
            asm_s[h * MLA_V:(h + 1) * MLA_V, part * pt:(part + 1) * pt] = _dot(wuv_ref[h], oh).astype(BF16)
    y_ref[0] = x_ref[0] + _dot_tn(asm_s[...], wo_ref[...])


def _mla_attend_prompt(q_t, keys, v_t, x, w):
    b, t, d = x.shape
    nt, tq = v_t.shape[1], v_t.shape[3]
    assert t == nt * tq and tq & (tq - 1) == 0
    rows = MLA_HEADS * tq
    kern = functools.partial(_flash_kernel, tq=tq)
    return pl.pallas_call(
        kern, grid=(b, nt),
        in_specs=[
            pl.BlockSpec((1, 1, MLA_KEY_WIDTH, rows), lambda i, j: (i, j, 0, 0)),
            pl.BlockSpec((1, t, MLA_KEY_WIDTH), lambda i, j: (i, 0, 0)),
            pl.BlockSpec((1, nt, MLA_KV_RANK, tq), lambda i, j: (i, 0, 0, 0)),
            pl.BlockSpec((1, tq, d), lambda i, j: (i, j, 0)),
            _full(w["w_uv_h"].shape),
            _full(w["w_out1"].shape),
        ],
        out_specs=pl.BlockSpec((1, tq, d), lambda i, j: (i, j, 0)),
        out_shape=jax.ShapeDtypeStruct((b, t, d), F32),
        scratch_shapes=[
            pltpu.VMEM((MLA_KV_RANK, rows), F32),
            pltpu.VMEM((1, rows), F32),
            pltpu.VMEM((1, rows), F32),
            pltpu.VMEM((MLA_HEADS * MLA_V, tq), BF16),
        ],
        compiler_params=_params("arbitrary", "arbitrary"), name="mla_flash",
    )(q_t, keys, v_t, x, w["w_uv_h"], w["w_out1"])


def _page_copies(pt_ref, lat_hbm, rope_hbm, lat_buf, rope_buf, sem, seq, slot, start, pages=None):
    def body(p, carry):
        page = pt_ref[seq, p]
        copies = (pltpu.make_async_copy(lat_hbm.at[page], lat_buf.at[slot, p], sem.at[0, slot]),
                  pltpu.make_async_copy(rope_hbm.at[page], rope_buf.at[slot, p], sem.at[1, slot]))
        for c in copies:
            if start:
                c.start(priority=p % 2 if isinstance(p, int) else 0)
            else:
                c.wait()
        return carry

    if pages is None:
        lax.fori_loop(0, lat_buf.shape[1], body, 0)
    else:
        for p in pages:
            body(p, 0)


def _decode_kernel(pt_ref, q_ref, kn_ref, lat_hbm, rope_hbm, o_ref, lat_buf, rope_buf, s_buf, sem, *, group):
    r = MLA_KV_RANK
    n_slots, n_pages, page = lat_buf.shape[0], lat_buf.shape[1], lat_buf.shape[2]
    ahead = n_slots - 1
    b = pl.program_id(0)
    last = pl.num_programs(0) - 1
    slot = lax.rem(b, n_slots)
    args = (pt_ref, lat_hbm, rope_hbm, lat_buf, rope_buf, sem)

    @pl.when(b == 0)
    def _():
        for s0 in range(ahead):
            _page_copies(*args, jnp.minimum(s0, last), s0, True)

    _page_copies(*args, b, slot, False, range(n_pages))
    nxt = jnp.minimum(b + ahead, last)
    nxt_slot = lax.rem(b + ahead, n_slots)

    q = q_ref[0]
    ql = q[:, 0:r]
    qp = q[:, r:r + MLA_ROPE]
    kn = kn_ref[0].astype(F32)
    s_new = jnp.sum(q.astype(F32) * kn, axis=1, keepdims=True)
    lat = lat_buf.at[slot]
    rope = rope_buf.at[slot]
    gk = group * page
    for g in range(n_pages // group):
        lat_g = lat[g * group:(g + 1) * group].reshape(gk, r).astype(BF16)
        s_pe = [_dot(qp, rope[g * group + i].astype(BF16)) for i in range(group)]
        s_buf[:, g * gk:(g + 1) * gk] = _dot_nt(ql, lat_g) + jnp.concatenate(s_pe, axis=1)
        _page_copies(*args, nxt, nxt_slot, True, range(g * group, (g + 1) * group))
    s = s_buf[...]
    m = jnp.maximum(jnp.max(s, axis=1, keepdims=True), s_new)
    p_all = jnp.exp((s - m) * MLA_SCALE)
    p_new = jnp.exp((s_new - m) * MLA_SCALE)
    denom = jnp.sum(p_all, axis=1, keepdims=True) + p_new
    acc = p_new * kn[:, 0:r]
    for g in range(n_pages // group):
        lat_g = lat[g * group:(g + 1) * group].reshape(gk, r).astype(BF16)
        acc = acc + _dot(p_all[:, g * gk:(g + 1) * gk].astype(BF16), lat_g)
    o_ref[0] = acc / denom

    @pl.when(b == last)
    def _():
        for j in range(1, n_slots):
            _page_copies(*args, last, lax.rem(last + j, n_slots), False)


def _mla_attend_sample(q, key_new, cache_latent, cache_rope_t, page_table):
    b = q.shape[0]
    n_pages = page_table.shape[1]
    page = cache_latent.shape[1]
    kw = MLA_KEY_WIDTH
    grid_spec = pltpu.PrefetchScalarGridSpec(
        num_scalar_prefetch=1, grid=(b,),
        in_specs=[
            pl.BlockSpec((1, MLA_HEADS, kw), lambda i, pt: (i, 0, 0)),
            pl.BlockSpec((1, 1, kw), lambda i, pt: (i, 0, 0)),
            pl.BlockSpec(memory_space=pl.ANY),
            pl.BlockSpec(memory_space=pl.ANY),
        ],
        out_specs=pl.BlockSpec((1, MLA_HEADS, MLA_KV_RANK), lambda i, pt: (i, 0, 0)),
        scratch_shapes=[
            pltpu.VMEM((DECODE_SLOTS, n_pages, page, MLA_KV_RANK), F32),
            pltpu.VMEM((DECODE_SLOTS, n_pages, MLA_ROPE, page), F32),
            pltpu.VMEM((MLA_HEADS, n_pages * page), F32),
            pltpu.SemaphoreType.DMA((2, DECODE_SLOTS)),
        ],
    )
    kern = functools.partial(_decode_kernel, group=math.gcd(n_pages, 8))
    return pl.pallas_call(
        kern, grid_spec=grid_spec, out_shape=jax.ShapeDtypeStruct((b, MLA_HEADS, MLA_KV_RANK), F32),
        compiler_params=_params("arbitrary"), name="mla_decode",
    )(page_table, q, key_new, cache_latent, cache_rope_t)


def _attn_out_kernel(o_ref, x_ref, wuv_ref, wo_ref, y_ref, asm_s):
    for h in range(MLA_HEADS):
        oh = o_ref[:, h * MLA_KV_RANK:(h + 1) * MLA_KV_RANK].astype(BF16)
        asm_s[:, h * MLA_V:(h + 1) * MLA_V] = _dot(oh, wuv_ref[h]).astype(BF16)
    y_ref[...] = x_ref[...] + _dot(asm_s[...], wo_ref[...])


def _attn_out_sample(o, x, w):
    m, d = x.shape
    tm = min(128, m)
    assert m % tm == 0
    ow = MLA_HEADS * MLA_KV_RANK
    return pl.pallas_call(
        _attn_out_kernel, grid=(m // tm,),
        in_specs=[pl.BlockSpec((tm, ow), lambda i: (i, 0)), pl.BlockSpec((tm, d), lambda i: (i, 0)),
                  _full(w["w_uv_t"].shape), _full(w["w_out1"].shape)],
        out_specs=pl.BlockSpec((tm, d), lambda i: (i, 0)),
        out_shape=jax.ShapeDtypeStruct((m, d), F32),
        scratch_shapes=[pltpu.VMEM((tm, MLA_HEADS * MLA_V), BF16)],
        compiler_params=_params("arbitrary"), name="attn_out",
    )(o.reshape(m, ow), x, w["w_uv_t"], w["w_out1"])


def _rope_tables(pos):
    half = MLA_ROPE // 2
    inv = ROPE_THETA ** (-jnp.arange(half, dtype=F32) * 2.0 / MLA_ROPE)
    ang = pos.astype(F32)[:, None] * inv[None, :]
    cos = jnp.cos(ang)
    sin = jnp.sin(ang)
    pad = jnp.zeros((pos.shape[0], LANES - MLA_ROPE), F32)
    return (jnp.concatenate([cos, cos, pad], axis=1), jnp.concatenate([-sin, sin, pad], axis=1),
            jnp.concatenate([cos, cos, cos, cos], axis=1), jnp.concatenate([-sin, sin, -sin, sin], axis=1))


def _pad_lanes(a, width):
    return jnp.pad(a, ((0, 0), (0, width - a.shape[1])))


def _prepare_weights(norm_mix, norm_ffn, norm_final, w_in0, mlstm_b_i, mlstm_b_f, mlstm_norm, pool_w,
                     pool_scale, w_out0, w_in1, mla_q_norm, mla_kv_norm, w_q_up, w_uk, w_uv, w_out1,
                     w_up, w_down):
    d = w_in0.shape[0]
    half = MLA_ROPE // 2
    gate_i = w_in0[:, Z_MAIN:Z_MAIN + ML_HEADS]
    gate_f = w_in0[:, Z_MAIN + ML_HEADS:Z_MAIN + 2 * ML_HEADS]
    rope_k = w_in1[:, MLA_Q_RANK + MLA_KV_RANK:]
    rope_k_sw = jnp.concatenate([rope_k[:, half:], rope_k[:, :half]], axis=1)
    q_up = w_q_up.reshape(MLA_Q_RANK, MLA_HEADS, MLA_NOPE + MLA_ROPE)
    q_pe = q_up[:, :, MLA_NOPE:]
    q_pe_sw = jnp.concatenate([q_pe[:, :, half:], q_pe[:, :, :half]], axis=2)
    col8 = lambda v: jnp.pad(v.astype(F32), (0, 8 - ML_HEADS)).reshape(8, 1)
    return {
        "g_mix0": norm_mix[0].reshape(1, d), "g_mix1": norm_mix[1].reshape(1, d),
        "g_ffn0": norm_ffn[0].reshape(1, d), "g_ffn1": norm_ffn[1].reshape(1, d),
        "g_final": norm_final.reshape(1, d),
        "w_in0_gates": _pad_lanes(w_in0, Z_MAIN + LANES).astype(BF16),
        "w_gate_i_t": jnp.pad(gate_i.T, ((0, 8 - ML_HEADS), (0, 0))).astype(BF16),
        "w_gate_f_t": jnp.pad(gate_f.T, ((0, 8 - ML_HEADS), (0, 0))).astype(BF16),
        "b_i_col": col8(mlstm_b_i), "b_f_col": col8(mlstm_b_f),
        "b_i_row": mlstm_b_i.reshape(1, ML_HEADS), "b_f_row": mlstm_b_f.reshape(1, ML_HEADS),
        "mlstm_norm": mlstm_norm.reshape(1, ML_WIDTH),
        "pool_w": pool_w.astype(BF16), "pool_scale": pool_scale.reshape(1, POOL_WIDTH),
        "w_out0": w_out0.astype(BF16),
        "w_in1": jnp.concatenate(
            [w_in1[:, :MLA_Q_RANK + MLA_KV_RANK], _pad_lanes(rope_k, LANES), _pad_lanes(rope_k_sw, LANES)],
            axis=1).astype(BF16),
        "mla_q_norm": mla_q_norm.reshape(1, MLA_Q_RANK), "mla_kv_norm": mla_kv_norm.reshape(1, MLA_KV_RANK),
        "w_q_nope": q_up[:, :, :MLA_NOPE].reshape(MLA_Q_RANK, MLA_HEADS * MLA_NOPE).astype(BF16),
        "w_q_pe": jnp.concatenate([q_pe.reshape(MLA_Q_RANK, MLA_HEADS * MLA_ROPE),
                                   q_pe_sw.reshape(MLA_Q_RANK, MLA_HEADS * MLA_ROPE)], axis=1).astype(BF16),
        "w_uk_h": jnp.transpose(w_uk, (1, 0, 2)).astype(BF16),
        "w_uv_t": jnp.transpose(w_uv, (1, 0, 2)).astype(BF16),
        "w_uv_h": jnp.transpose(w_uv, (1, 2, 0)).astype(BF16),
        "w_out1": w_out1.astype(BF16),
        "w_up": w_up.astype(BF16), "w_down": w_down.astype(BF16),
    }


def kernel(x_prompt, x_sample, state_mlstm_C, state_mlstm_n, state_mlstm_m, state_pool, cache_latent, cache_rope_k, page_table, norm_mix, norm_ffn, norm_final, w_in0, mlstm_b_i, mlstm_b_f, mlstm_norm, pool_w, pool_scale, w_out0, w_in1, mla_q_norm, mla_kv_norm, w_q_up, w_uk, w_uv, w_out1, w_up, w_down):
    w = _prepare_weights(norm_mix, norm_ffn, norm_final, w_in0, mlstm_b_i, mlstm_b_f, mlstm_norm, pool_w,
                         pool_scale, w_out0, w_in1, mla_q_norm, mla_kv_norm, w_q_up, w_uk, w_uv, w_out1,
                         w_up, w_down)
    bp, t, d = x_prompt.shape
    bs, ts, _ = x_sample.shape
    assert ts == 1
    past_len = page_table.shape[1] * cache_latent.shape[1]

    x1, c_p, n_p, m_p, buf_p = _mixer0_prompt(x_prompt, w, 0)
    x2 = _ffn(x1.reshape(bp * t, d), w["g_ffn0"], w["w_up"], w["w_down"], 0, w["g_final"], False)
    x2 = x2.reshape(bp, t, d)
    ckv_p, kpe_p, keys_p, vt_p, qt_p = _mla_proj(x2, w, jnp.arange(t))
    x3 = _mla_attend_prompt(qt_p, keys_p, vt_p, x2, w)
    y_p = _ffn(x3.reshape(bp * t, d), w["g_ffn1"], w["w_up"], w["w_down"], 1, w["g_final"], True)
    y_p = y_p.reshape(bp, t, d)

    xs = x_sample.reshape(bs, d)
    xs1, c_s, n_s, m_s, buf_s = _mixer0_sample(xs, state_mlstm_C, state_mlstm_n, state_mlstm_m, state_pool,
                                               w, past_len)
    xs2 = _ffn(xs1, w["g_ffn0"], w["w_up"], w["w_down"], 0, w["g_final"], False)
    ckv_s, kpe_s, keys_s, _, qt_s = _mla_proj(xs2.reshape(1, bs, d), w, jnp.full((bs,), past_len))
    assert qt_s.shape == (1, 1, MLA_KEY_WIDTH, MLA_HEADS * bs)
    q_s = jnp.transpose(qt_s.reshape(MLA_KEY_WIDTH, MLA_HEADS, bs), (2, 1, 0))
    rope_t = jnp.transpose(cache_rope_k, (0, 2, 1))
    o_s = _mla_attend_sample(q_s, keys_s.reshape(bs, 1, MLA_KEY_WIDTH), cache_latent, rope_t, page_table)
    xs3 = _attn_out_sample(o_s, xs2, w)
    y_s = _ffn(xs3, w["g_ffn1"], w["w_up"], w["w_down"], 1, w["g_final"], True)

    return (y_p, y_s.reshape(bs, 1, d), c_p, n_p, m_p, buf_p, ckv_p, kpe_p,
            c_s, n_s, m_s, buf_s, ckv_s.reshape(bs, 1, MLA_KV_RANK), kpe_s.reshape(bs, 1, MLA_ROPE))
```

```python
import functools
import math

import jax
import jax.numpy as jnp
from jax import lax
from jax.experimental import pallas as pl
from jax.experimental.pallas import tpu as pltpu

F32 = jnp.float32
BF16 = jnp.bfloat16

EPS = 1e-6
ML_HEADS = 4
ML_HEAD_DIM = 128
ML_WIDTH = ML_HEADS * ML_HEAD_DIM
POOL_WINDOWS = (2, 4, 8, 16)
POOL_GROUP = 128
POOL_WIDTH = len(POOL_WINDOWS) * POOL_GROUP
POOL_BUF = max(POOL_WINDOWS) - 1
POOL_HIST = POOL_BUF + 1
POOL_KEEP = 2 * POOL_HIST
MLA_HEADS = 8
MLA_NOPE = 128
MLA_ROPE = 64
MLA_V = 128
MLA_Q_RANK = 512
MLA_KV_RANK = 256
MLA_SCALE = (MLA_NOPE + MLA_ROPE) ** -0.5
MLA_SCALE_LOG2E = MLA_SCALE * math.log2(math.e)
ROPE_THETA = 10000.0
LANES = 128
MLA_KEY_WIDTH = MLA_KV_RANK + LANES
Z_MAIN = 4 * ML_WIDTH + POOL_WIDTH
MLA_TOKEN_TILE = 512
DECODE_SLOTS = 3

VMEM_LIMIT_BYTES = 48 * 1024 * 1024
MIXER0_VMEM_LIMIT_BYTES = 56 * 1024 * 1024

_NT = (((1,), (1,)), ((), ()))
_TN = (((0,), (0,)), ((), ()))


def _dot(a, b):
    return jnp.dot(a, b, preferred_element_type=F32)


def _dot_nt(a, b):
    return lax.dot_general(a, b, _NT, preferred_element_type=F32)


def _dot_tn(a, b):
    return lax.dot_general(a, b, _TN, preferred_element_type=F32)


def _rms(x, g):
    return x * lax.rsqrt(jnp.mean(x * x, axis=-1, keepdims=True) + EPS) * g


def _log_sigmoid(x):
    return jnp.minimum(x, 0.0) - jnp.log1p(jnp.exp(-jnp.abs(x)))


def _head_norm(h, g):
    hc = h - jnp.mean(h, axis=-1, keepdims=True)
    return hc * lax.rsqrt(jnp.mean(hc * hc, axis=-1, keepdims=True) + EPS) * g


def _cumsum_lanes(x):
    n = x.shape[-1]
    lane = lax.broadcasted_iota(jnp.int32, x.shape, x.ndim - 1)
    s = 1
    while s < n:
        x = x + jnp.where(lane >= s, pltpu.roll(x, s, axis=x.ndim - 1), 0.0)
        s *= 2
    return x


def _params(*semantics, vmem_limit_bytes=VMEM_LIMIT_BYTES):
    return pltpu.CompilerParams(dimension_semantics=semantics, vmem_limit_bytes=vmem_limit_bytes)


def _full(shape):
    return pl.BlockSpec(shape, lambda *_: (0,) * len(shape))


def _in_proj(x, w_in, z_out, qkv_out, gi_out, gf_out):
    g_ref, win_ref, wgi_ref, wgf_ref, bi_ref, bf_ref = w_in
    hn = _rms(x(), g_ref[...]).astype(BF16)
    z = _dot(hn, win_ref[...])
    z_out[...] = z
    qkv_out[:, 0:ML_WIDTH] = z[:, 0:ML_WIDTH].astype(BF16)
    qkv_out[:, ML_WIDTH:2 * ML_WIDTH] = (z[:, ML_WIDTH:2 * ML_WIDTH] * (ML_HEAD_DIM ** -0.5)).astype(BF16)
    qkv_out[:, 2 * ML_WIDTH:3 * ML_WIDTH] = z[:, 2 * ML_WIDTH:3 * ML_WIDTH].astype(BF16)
    gi_out[...] = _dot_nt(wgi_ref[...], hn) + bi_ref[...]
    gf_out[...] = _log_sigmoid(_dot_nt(wgf_ref[...], hn) + bf_ref[...])


def _mixer0_tile(x, cur, nxt, w_in, gn_ref, pw_ref, ps_ref, wout_ref, asm_s, ext_s, lvl_s, c_s, n_s, m_s,
                 *, chunk, pos):
    z_s, qkv_s, gi_s, gf_s = cur
    tt = z_s.shape[0]
    dh = ML_HEAD_DIM
    units = [(c, h) for c in range(tt // chunk) for h in range(ML_HEADS)]
    rows_of = lambda c: slice(c * chunk, (c + 1) * chunk)
    head = lambda part, h: slice(part * ML_WIDTH + h * dh, part * ML_WIDTH + (h + 1) * dh)
    row = lax.broadcasted_iota(jnp.int32, (chunk, chunk), 0)
    col = lax.broadcasted_iota(jnp.int32, (chunk, chunk), 1)
    causal = row >= col
    diag = row == col

    gates = []
    m_prev8 = m_s[:, 0:1]
    for c in range(tt // chunk):
        bcum = _cumsum_lanes(gf_s[:, rows_of(c)])
        a = gi_s[:, rows_of(c)] - bcum
        m_last8 = jnp.maximum(m_prev8, jnp.max(a, axis=1, keepdims=True))
        gates.append((a, bcum, m_prev8, m_last8, jnp.exp(m_prev8 - m_last8)))
        m_prev8 = bcum[:, chunk - 1:chunk] + m_last8
    m_s[...] = jnp.broadcast_to(m_prev8, m_s.shape)

    scores = {}
    for c, h in units:
        scores[c, h] = _dot_nt(qkv_s[rows_of(c), head(0, h)], qkv_s[rows_of(c), head(1, h)])

    _in_proj(nxt[0], w_in, *nxt[1])

    parts = {}
    for c, h in units:
        a, bcum, m_prev8, m_last8, _ = gates[c]
        a_row = a[h:h + 1, :]
        m_prev = m_prev8[h:h + 1, :]
        a_mat = jnp.where(causal, a_row, -jnp.inf)
        m_col = jnp.maximum(jnp.max(a_mat, axis=1, keepdims=True), m_prev)
        b_col = jnp.sum(jnp.where(diag, bcum[h:h + 1, :], 0.0), axis=1, keepdims=True)
        a_col = jnp.sum(jnp.where(diag, a_row, 0.0), axis=1, keepdims=True)
        s = scores[c, h] * jnp.exp(a_mat - m_col)
        vb = qkv_s[rows_of(c), head(2, h)]
        kw = z_s[rows_of(c), head(1, h)] * (dh ** -0.5) * jnp.exp(a_col - m_last8[h:h + 1, :])
        parts[c, h] = (_dot(s.astype(BF16), vb), jnp.sum(s, axis=1, keepdims=True),
                       jnp.exp(m_prev - m_col), jnp.exp(-(b_col + m_col)),
                       _dot_tn(vb, kw.astype(BF16)), jnp.sum(kw, axis=0, keepdims=True))

    for c, h in units:
        num_intra, den_intra, w_inter, e_neg_m, c_add, n_add = parts[c, h]
        w_old = gates[c][4][h:h + 1, :]
        q = z_s[rows_of(c), head(0, h)]
        c_mat = c_s[h]
        n_row = n_s[h:h + 1, :]
        num = num_intra + w_inter * _dot_nt(qkv_s[rows_of(c), head(0, h)], c_mat.astype(BF16))
        den = den_intra + w_inter * jnp.sum(q * n_row, axis=1, keepdims=True)
        hh = num / jnp.maximum(jnp.abs(den), e_neg_m)
        c_s[h] = w_old * c_mat + c_add
        n_s[h:h + 1, :] = w_old * n_row + n_add
        o_gate = jax.nn.sigmoid(z_s[rows_of(c), head(3, h)])
        hm = _head_norm(hh, gn_ref[:, h * dh:(h + 1) * dh]) * o_gate
        asm_s[rows_of(c), h * dh:(h + 1) * dh] = hm.astype(BF16)

    keep = POOL_KEEP
    end = keep + tt
    ext_s[keep:end, :] = z_s[:, 4 * ML_WIDTH:Z_MAIN]
    for g, win in enumerate(POOL_WINDOWS):
        lanes = slice(g * POOL_GROUP, (g + 1) * POOL_GROUP)
        src, src_lanes = ext_s, lanes
        width, lo, level = 1, 0, 0
        while 2 * width < win:
            lo += 8
            dst = lvl_s.at[level % 2]
            dst[lo:end, :] = src[lo:end, src_lanes] + src[lo - width:end - width, src_lanes]
            src, src_lanes = dst, slice(0, POOL_GROUP)
            width, level = 2 * width, level + 1
        wsum = src[keep:end, src_lanes] + src[keep - width:end - width, src_lanes]
        u = ext_s[keep:end, lanes]
        cnt = jnp.minimum(win, pos + 1).astype(F32)
        pooled = wsum / cnt - u
        og = _dot(pooled.astype(BF16), pw_ref[g]) * ps_ref[:, lanes]
        asm_s[:, ML_WIDTH + g * POOL_GROUP:ML_WIDTH + (g + 1) * POOL_GROUP] = og.astype(BF16)
    ext_s[0:keep, :] = ext_s[tt:end, :]

    return x() + _dot(asm_s[...], wout_ref[...])


def _mixer0_kernel(x_ref, xn_ref, g_ref, win_ref, wgi_ref, wgf_ref, bi_ref, bf_ref, gn_ref, pw_ref, ps_ref,
                   wout_ref, y_ref, c_out, n_out, m_out, buf_out,
                   za_s, qa_s, gia_s, gfa_s, zb_s, qb_s, gib_s, gfb_s, asm_s, ext_s, lvl_s, c_s, n_s, m_s,
                   *, tt, chunk, pos0):
    b = pl.program_id(0)
    i = pl.program_id(1)
    w_in = (g_ref, win_ref, wgi_ref, wgf_ref, bi_ref, bf_ref)
    set_a = (za_s, qa_s, gia_s, gfa_s)
    set_b = (zb_s, qb_s, gib_s, gfb_s)

    @pl.when((b == 0) & (i == 0))
    def _():
        _in_proj(lambda: x_ref[0, 0:tt, :], w_in, *set_a)

    @pl.when(i == 0)
    def _():
        c_s[...] = jnp.zeros_like(c_s)
        n_s[...] = jnp.zeros_like(n_s)
        m_s[...] = jnp.zeros_like(m_s)
        ext_s[0:POOL_KEEP, :] = jnp.zeros((POOL_KEEP, POOL_WIDTH), F32)

    tile = functools.partial(_mixer0_tile, w_in=w_in, gn_ref=gn_ref, pw_ref=pw_ref, ps_ref=ps_ref,
                             wout_ref=wout_ref, asm_s=asm_s, ext_s=ext_s, lvl_s=lvl_s, c_s=c_s, n_s=n_s,
                             m_s=m_s, chunk=chunk)
    pos = pos0 + 2 * i * tt + lax.broadcasted_iota(jnp.int32, (tt, 1), 0)
    x0 = lambda: x_ref[0, 0:tt, :]
    x1 = lambda: x_ref[0, tt:2 * tt, :]
    y_ref[0, 0:tt, :] = tile(x0, set_a, (x1, set_b), pos=pos)
    y_ref[0, tt:2 * tt, :] = tile(x1, set_b, (lambda: xn_ref[0], set_a), pos=pos + tt)

    @pl.when(i == pl.num_programs(1) - 1)
    def _():
        c_out[0] = c_s[...]
        n_out[0] = n_s[0:ML_HEADS, :]
        m_out[0] = m_s[...]
        buf_out[0] = ext_s[POOL_KEEP - POOL_HIST:POOL_KEEP, :]


def _mixer0_prompt(x, w, pos0):
    b, t, d = x.shape
    tt = min(512, t // 2)
    chunk = min(256, tt)
    assert t % (2 * tt) == 0 and tt % chunk == 0 and tt >= POOL_KEEP
    n_pairs = t // (2 * tt)
    kern = functools.partial(_mixer0_kernel, tt=tt, chunk=chunk, pos0=pos0)

    def next_tile(i, j):
        nxt = jnp.minimum(i * n_pairs + j + 1, b * n_pairs - 1)
        return (nxt // n_pairs, 2 * (nxt % n_pairs), 0)

    out_shape = (
        jax.ShapeDtypeStruct((b, t, d), F32),
        jax.ShapeDtypeStruct((b, ML_HEADS, ML_HEAD_DIM, ML_HEAD_DIM), F32),
        jax.ShapeDtypeStruct((b, ML_HEADS, ML_HEAD_DIM), F32),
        jax.ShapeDtypeStruct((b, 8, LANES), F32),
        jax.ShapeDtypeStruct((b, POOL_HIST, POOL_WIDTH), F32),
    )
    in_specs = [
        pl.BlockSpec((1, 2 * tt, d), lambda i, j: (i, j, 0)),
        pl.BlockSpec((1, tt, d), next_tile),
        _full((1, d)),
        pl.BlockSpec((d, Z_MAIN), lambda i, j: (0, 0), pipeline_mode=pl.Buffered(1)),
        _full((8, d)),
        _full((8, d)),
        _full((8, 1)),
        _full((8, 1)),
        _full((1, ML_WIDTH)),
        _full((len(POOL_WINDOWS), POOL_GROUP, POOL_GROUP)),
        _full((1, POOL_WIDTH)),
        pl.BlockSpec((ML_WIDTH + POOL_WIDTH, d), lambda i, j: (0, 0), pipeline_mode=pl.Buffered(1)),
    ]
    out_specs = (
        pl.BlockSpec((1, 2 * tt, d), lambda i, j: (i, j, 0)),
        pl.BlockSpec((1, ML_HEADS, ML_HEAD_DIM, ML_HEAD_DIM), lambda i, j: (i, 0, 0, 0)),
        pl.BlockSpec((1, ML_HEADS, ML_HEAD_DIM), lambda i, j: (i, 0, 0)),
        pl.BlockSpec((1, 8, LANES), lambda i, j: (i, 0, 0)),
        pl.BlockSpec((1, POOL_HIST, POOL_WIDTH), lambda i, j: (i, 0, 0)),
    )
    proj_set = [pltpu.VMEM((tt, Z_MAIN), F32), pltpu.VMEM((tt, 3 * ML_WIDTH), BF16),
                pltpu.VMEM((8, tt), F32), pltpu.VMEM((8, tt), F32)]
    scratch = proj_set + proj_set + [
        pltpu.VMEM((tt, ML_WIDTH + POOL_WIDTH), BF16),
        pltpu.VMEM((tt + POOL_KEEP, POOL_WIDTH), F32),
        pltpu.VMEM((2, tt + POOL_KEEP, POOL_GROUP), F32),
        pltpu.VMEM((ML_HEADS, ML_HEAD_DIM, ML_HEAD_DIM), F32),
        pltpu.VMEM((8, ML_HEAD_DIM), F32),
        pltpu.VMEM((8, LANES), F32),
    ]
    y, c, n, m, buf = pl.pallas_call(
        kern, grid=(b, n_pairs), in_specs=in_specs, out_specs=out_specs, out_shape=out_shape,
        scratch_shapes=scratch, name="mixer0_prompt",
        compiler_params=_params("arbitrary", "arbitrary", vmem_limit_bytes=MIXER0_VMEM_LIMIT_BYTES),
    )(x, x, w["g_mix0"], w["w_in0_gates"], w["w_gate_i_t"], w["w_gate_f_t"], w["b_i_col"], w["b_f_col"],
      w["mlstm_norm"], w["pool_w"], w["pool_scale"], w["w_out0"])
    return y, c, n, m[:, :ML_HEADS, 0], buf[:, 1:, :]


def _norm_matmul_kernel(x_ref, g_ref, w_ref, o_ref):
    o_ref[...] = _dot(_rms(x_ref[...], g_ref[...]).astype(BF16), w_ref[...])


def _norm_matmul(x, g, w_bf16):
    m, d = x.shape
    n = w_bf16.shape[1]
    tm = min(128, m)
    assert m % tm == 0
    return pl.pallas_call(
        _norm_matmul_kernel, grid=(m // tm,),
        in_specs=[pl.BlockSpec((tm, d), lambda i: (i, 0)), _full((1, d)), _full((d, n))],
        out_specs=pl.BlockSpec((tm, n), lambda i: (i, 0)),
        out_shape=jax.ShapeDtypeStruct((m, n), F32),
        compiler_params=_params("arbitrary"), name="norm_matmul",
    )(x, g, w_bf16)


def _mixer0_step_kernel(z_ref, x_ref, c0_ref, n0_ref, m0_ref, buf_ref, bi_ref, bf_ref, gn_ref, pw_ref,
                        ps_ref, wout_ref, y_ref, c_ref, n_ref, m_ref, bufo_ref, cq_s, asm_s, *, bb, cnts):
    dh = ML_HEAD_DIM
    gate_i = Z_MAIN
    gate_f = Z_MAIN + ML_HEADS
    ig = z_ref[:, gate_i:gate_i + ML_HEADS] + bi_ref[...]
    lf = _log_sigmoid(z_ref[:, gate_f:gate_f + ML_HEADS] + bf_ref[...])
    m0 = m0_ref[...]
    m_t = jnp.maximum(lf + m0, ig)
    w_i = jnp.exp(ig - m_t)
    w_f = jnp.exp(lf + m0 - m_t)
    e_m = jnp.exp(-m_t)
    m_ref[...] = m_t

    row8 = lax.broadcasted_iota(jnp.int32, (8, dh), 0) == 0
    for h in range(ML_HEADS):
        q = z_ref[:, h * dh:(h + 1) * dh]
        k = z_ref[:, ML_WIDTH + h * dh:ML_WIDTH + (h + 1) * dh] * (dh ** -0.5)
        v = z_ref[:, 2 * ML_WIDTH + h * dh:2 * ML_WIDTH + (h + 1) * dh]
        wi = w_i[:, h:h + 1]
        wf = w_f[:, h:h + 1]
        kw = k * wi
        for b in range(bb):
            c_mat = c0_ref[b, h]
            q8 = jnp.broadcast_to(q[b:b + 1, :], (8, dh))
            cq_s[b:b + 1, :] = _dot_nt(q8, c_mat)[0:1, :]
            v8 = jnp.where(row8, v[b:b + 1, :], 0.0)
            k8 = jnp.where(row8, kw[b:b + 1, :], 0.0)
            c_ref[b, h] = wf[b:b + 1, :] * c_mat + _dot_tn(v8, k8)
        n_old = n0_ref[:, h, :]
        s = jnp.sum(q * k, axis=1, keepdims=True) * wi
        num = s * v + wf * cq_s[...]
        den = s + wf * jnp.sum(n_old * q, axis=1, keepdims=True)
        hh = num / jnp.maximum(jnp.abs(den), e_m[:, h:h + 1])
        n_ref[:, h, :] = wf * n_old + kw
        o_gate = jax.nn.sigmoid(z_ref[:, 3 * ML_WIDTH + h * dh:3 * ML_WIDTH + (h + 1) * dh])
        hm = _head_norm(hh, gn_ref[:, h * dh:(h + 1) * dh]) * o_gate
        asm_s[:, h * dh:(h + 1) * dh] = hm

    u_all = z_ref[:, 4 * ML_WIDTH:Z_MAIN]
    for g, win in enumerate(POOL_WINDOWS):
        u = u_all[:, g * POOL_GROUP:(g + 1) * POOL_GROUP]
        wsum = u
        for j in range(1, win):
            off = (POOL_BUF - j) * POOL_WIDTH + g * POOL_GROUP
            wsum = wsum + buf_ref[:, off:off + POOL_GROUP]
        pooled = wsum / cnts[g] - u
        og = _dot(pooled.astype(BF16), pw_ref[g]) * ps_ref[:, g * POOL_GROUP:(g + 1) * POOL_GROUP]
        asm_s[:, ML_WIDTH + g * POOL_GROUP:ML_WIDTH + (g + 1) * POOL_GROUP] = og
    keep = (POOL_BUF - 1) * POOL_WIDTH
    bufo_ref[:, 0:keep] = buf_ref[:, POOL_WIDTH:POOL_WIDTH + keep]
    bufo_ref[:, keep:keep + POOL_WIDTH] = u_all

    y_ref[...] = x_ref[...] + _dot(asm_s[...].astype(BF16), wout_ref[...])


def _mixer0_sample(x, c0, n0, m0, buf0, w, pos0):
    b, d = x.shape
    z = _norm_matmul(x, w["g_mix0"], w["w_in0_gates"])
    zw = z.shape[1]
    bb = 16 if b % 16 == 0 else 8
    assert b % bb == 0
    cnts = tuple(float(min(win, pos0 + 1)) for win in POOL_WINDOWS)
    bufw = POOL_BUF * POOL_WIDTH
    kern = functools.partial(_mixer0_step_kernel, bb=bb, cnts=cnts)
    hd = (ML_HEADS, ML_HEAD_DIM)
    in_specs = [
        pl.BlockSpec((bb, zw), lambda i: (i, 0)),
        pl.BlockSpec((bb, d), lambda i: (i, 0)),
        pl.BlockSpec((bb,) + hd + (ML_HEAD_DIM,), lambda i: (i, 0, 0, 0)),
        pl.BlockSpec((bb,) + hd, lambda i: (i, 0, 0)),
        pl.BlockSpec((bb, ML_HEADS), lambda i: (i, 0)),
        pl.BlockSpec((bb, bufw), lambda i: (i, 0)),
        _full((1, ML_HEADS)),
        _full((1, ML_HEADS)),
        _full((1, ML_WIDTH)),
        _full((len(POOL_WINDOWS), POOL_GROUP, POOL_GROUP)),
        _full((1, POOL_WIDTH)),
        _full((ML_WIDTH + POOL_WIDTH, d)),
    ]
    out_specs = (
        pl.BlockSpec((bb, d), lambda i: (i, 0)),
        pl.BlockSpec((bb,) + hd + (ML_HEAD_DIM,), lambda i: (i, 0, 0, 0)),
        pl.BlockSpec((bb,) + hd, lambda i: (i, 0, 0)),
        pl.BlockSpec((bb, ML_HEADS), lambda i: (i, 0)),
        pl.BlockSpec((bb, bufw), lambda i: (i, 0)),
    )
    out_shape = (
        jax.ShapeDtypeStruct((b, d), F32),
        jax.ShapeDtypeStruct((b,) + hd + (ML_HEAD_DIM,), F32),
        jax.ShapeDtypeStruct((b,) + hd, F32),
        jax.ShapeDtypeStruct((b, ML_HEADS), F32),
        jax.ShapeDtypeStruct((b, bufw), F32),
    )
    scratch = [pltpu.VMEM((bb, ML_HEAD_DIM), F32), pltpu.VMEM((bb, ML_WIDTH + POOL_WIDTH), F32)]
    y, c, n, m, buf = pl.pallas_call(
        kern, grid=(b // bb,), in_specs=in_specs, out_specs=out_specs, out_shape=out_shape,
        scratch_shapes=scratch, compiler_params=_params("arbitrary"), name="mixer0_sample",
    )(z, x, c0, n0, m0, buf0.reshape(b, bufw), w["b_i_row"], w["b_f_row"], w["mlstm_norm"],
      w["pool_w"], w["pool_scale"], w["w_out0"])
    return y, c, n, m, buf.reshape(b, POOL_BUF, POOL_WIDTH)


def _ffn_kernel(x_ref, g_ref, wu_ref, wd_ref, gf_ref, y_ref, hn_s, acc_s, *, ts, tf, final_norm, f_steps):
    n_sub = x_ref.shape[0] // ts
    j = pl.program_id(1)

    def norm():
        for s in range(n_sub):
            hn_s[s] = _rms(x_ref[s * ts:(s + 1) * ts, :], g_ref[...]).astype(BF16)

    def mlp(s, assign_first):
        for f in range(wu_ref.shape[1] // tf):
            a = jnp.maximum(_dot(hn_s[s], wu_ref[:, f * tf:(f + 1) * tf]), 0.0)
            part = _dot((a * a).astype(BF16), wd_ref[f * tf:(f + 1) * tf, :])
            if f == 0 and assign_first:
                acc_s[s] = part
            else:
                acc_s[s] += part

    def finish(s):
        rows = slice(s * ts, (s + 1) * ts)
        y = x_ref[rows, :] + acc_s[s]
        if final_norm:
            y = _rms(y, gf_ref[...])
        y_ref[rows, :] = y

    if f_steps == 1:
        norm()
        for s in range(n_sub):
            mlp(s, True)
            finish(s)
    else:
        @pl.when(j == 0)
        def _():
            norm()
            acc_s[...] = jnp.zeros_like(acc_s)

        for s in range(n_sub):
            mlp(s, False)

        @pl.when(j == f_steps - 1)
        def _():
            for s in range(n_sub):
                finish(s)


def _ffn(x, g, w_up, w_down, layer, g_final, final_norm):
    m, d = x.shape
    dff = w_up.shape[2]
    n_sub = 2 if m % 16 == 0 else 1
    ts = min(512, m // n_sub)
    tf = min(1024, dff)
    tm = n_sub * ts
    assert m % tm == 0 and dff % tf == 0 and ts % 8 == 0
    f_steps = dff // tf if m == tm else 1
    fw = dff // f_steps
    kern = functools.partial(_ffn_kernel, ts=ts, tf=tf, final_norm=final_norm, f_steps=f_steps)
    mode = pl.Buffered(1) if f_steps == 1 else None
    return pl.pallas_call(
        kern, grid=(m // tm, f_steps),
        in_specs=[
            pl.BlockSpec((tm, d), lambda i, j: (i, 0)),
            _full((1, d)),
            pl.BlockSpec((None, d, fw), lambda i, j: (layer, 0, j), pipeline_mode=mode),
            pl.BlockSpec((None, fw, d), lambda i, j: (layer, j, 0), pipeline_mode=mode),
            _full((1, d)),
        ],
        out_specs=pl.BlockSpec((tm, d), lambda i, j: (i, 0)),
        out_shape=jax.ShapeDtypeStruct((m, d), F32),
        scratch_shapes=[pltpu.VMEM((n_sub, ts, d), BF16), pltpu.VMEM((n_sub, ts, d), F32)],
        compiler_params=_params("arbitrary", "arbitrary"), name="ffn",
    )(x, g, w_up, w_down, g_final)


def _query_parts(tokens):
    return 2 if tokens % (2 * LANES) == 0 else 1


def _mla_proj_kernel(x_ref, g_ref, win_ref, gq_ref, gkv_ref, wqn_ref, wqp_ref, wuk_ref,
                     cos_ref, sin_ref, cos2_ref, sin2_ref, ckv_ref, kpe_ref, key_ref, vt_ref, qt_ref):
    r = MLA_KV_RANK
    tm = x_ref.shape[1]
    hn = _rms(x_ref[0], g_ref[...]).astype(BF16)
    z = _dot(hn, win_ref[...])
    cq = _rms(z[:, 0:MLA_Q_RANK], gq_ref[...]).astype(BF16)
    ckv = _rms(z[:, MLA_Q_RANK:MLA_Q_RANK + r], gkv_ref[...])
    cos = cos_ref[...]
    sin = sin_ref[...]
    k_prod = z[:, MLA_Q_RANK + r:MLA_Q_RANK + r + LANES] * (cos + pltpu.roll(sin, MLA_ROPE, axis=1))
    k_lane = lax.broadcasted_iota(jnp.int32, k_prod.shape, 1)
    kpe = jnp.where(k_lane < MLA_ROPE, k_prod + pltpu.roll(k_prod, MLA_ROPE, axis=1), 0.0)
    ckv_ref[0] = ckv
    kpe_ref[0] = kpe[:, 0:MLA_ROPE]
    key_ref[0, :, 0:r] = ckv.astype(BF16)
    key_ref[0, :, r:r + LANES] = kpe.astype(BF16)
    vt_ref[0, 0] = ckv.T.astype(BF16)
    qn = _dot(cq, wqn_ref[...])
    qpe = _dot(cq, wqp_ref[...])
    n_parts = _query_parts(tm)
    pt = tm // n_parts
    hw = MLA_HEADS * MLA_ROPE
    qt_ref[0, 0, r + MLA_ROPE:r + LANES, :] = jnp.zeros((LANES - MLA_ROPE, MLA_HEADS * tm), BF16)

    def store(h, rows, value):
        for part in range(n_parts):
            cols = slice((part * MLA_HEADS + h) * pt, (part * MLA_HEADS + h + 1) * pt)
            qt_ref[0, 0, rows, cols] = value[:, part * pt:(part + 1) * pt]

    for h in range(MLA_HEADS):
        lanes = slice(h * LANES, (h + 1) * LANES)
        store(h, slice(0, r), _dot_nt(wuk_ref[h], qn[:, lanes].astype(BF16)).astype(BF16))
    for pair in range(MLA_HEADS // 2):
        lanes = slice(pair * LANES, (pair + 1) * LANES)
        rot = qpe[:, lanes] * cos2_ref[...] + qpe[:, hw + pair * LANES:hw + (pair + 1) * LANES] * sin2_ref[...]
        rot_t = rot.T.astype(BF16)
        for i in range(2):
            store(2 * pair + i, slice(r, r + MLA_ROPE), rot_t[i * MLA_ROPE:(i + 1) * MLA_ROPE, :])


def _mla_proj(x, w, pos):
    nb, t, d = x.shape
    tm = min(MLA_TOKEN_TILE, t)
    assert t % tm == 0
    nt = t // tm
    kw = MLA_KEY_WIDTH
    cos, sin, cos2, sin2 = _rope_tables(pos)
    out_shape = (
        jax.ShapeDtypeStruct((nb, t, MLA_KV_RANK), F32),
        jax.ShapeDtypeStruct((nb, t, MLA_ROPE), F32),
        jax.ShapeDtypeStruct((nb, t, kw), BF16),
        jax.ShapeDtypeStruct((nb, nt, MLA_KV_RANK, tm), BF16),
        jax.ShapeDtypeStruct((nb, nt, kw, MLA_HEADS * tm), BF16),
    )
    in_specs = [
        pl.BlockSpec((1, tm, d), lambda i, j: (i, j, 0)),
        _full((1, d)),
        _full(w["w_in1"].shape),
        _full((1, MLA_Q_RANK)),
        _full((1, MLA_KV_RANK)),
        _full(w["w_q_nope"].shape),
        _full(w["w_q_pe"].shape),
        _full(w["w_uk_h"].shape),
        pl.BlockSpec((tm, LANES), lambda i, j: (j, 0)),
        pl.BlockSpec((tm, LANES), lambda i, j: (j, 0)),
        pl.BlockSpec((tm, LANES), lambda i, j: (j, 0)),
        pl.BlockSpec((tm, LANES), lambda i, j: (j, 0)),
    ]
    out_specs = (
        pl.BlockSpec((1, tm, MLA_KV_RANK), lambda i, j: (i, j, 0)),
        pl.BlockSpec((1, tm, MLA_ROPE), lambda i, j: (i, j, 0)),
        pl.BlockSpec((1, tm, kw), lambda i, j: (i, j, 0)),
        pl.BlockSpec((1, 1, MLA_KV_RANK, tm), lambda i, j: (i, j, 0, 0)),
        pl.BlockSpec((1, 1, kw, MLA_HEADS * tm), lambda i, j: (i, j, 0, 0)),
    )
    return pl.pallas_call(
        _mla_proj_kernel, grid=(nb, nt), in_specs=in_specs, out_specs=out_specs, out_shape=out_shape,
        compiler_params=_params("arbitrary", "arbitrary"), name="mla_proj",
    )(x, w["g_mix1"], w["w_in1"], w["mla_q_norm"], w["mla_kv_norm"], w["w_q_nope"], w["w_q_pe"],
      w["w_uk_h"], cos, sin, cos2, sin2)


def _flash_kernel(qt_ref, k_ref, vt_ref, x_ref, wuv_ref, wo_ref, y_ref, acc_s, m_s, l_s, asm_s, *, tq):
    qi = pl.program_id(1)
    tk = tq
    rows = MLA_HEADS * tq
    n_parts = _query_parts(tq)
    pt = tq // n_parts
    m_s[...] = jnp.full_like(m_s, -jnp.inf)
    l_s[...] = jnp.zeros_like(l_s)
    acc_s[...] = jnp.zeros_like(acc_s)

    def block(kb, key_lo, n_keys, first_part, masked):
        start = pl.multiple_of(kb * tk, tk)
        cols = slice(first_part * MLA_HEADS * pt, rows)
        n_cols = rows - cols.start
        s = _dot(k_ref[0, pl.ds(start + key_lo, n_keys), :], qt_ref[0, 0, :, cols])
        if masked:
            key = key_lo + lax.broadcasted_iota(jnp.int32, (n_keys, n_cols), 0)
            col = cols.start + lax.broadcasted_iota(jnp.int32, (n_keys, n_cols), 1)
            part_of_col = lax.shift_right_logical(col, (MLA_HEADS * pt).bit_length() - 1)
            tok = part_of_col * pt + (col & (pt - 1))
            s = jnp.where(key <= tok, s, -jnp.inf)
        m_old = m_s[:, cols]
        m_new = jnp.maximum(m_old, jnp.max(s, axis=0, keepdims=True))
        alpha = jnp.exp2((m_old - m_new) * MLA_SCALE_LOG2E)
        p = jnp.exp2((s - m_new) * MLA_SCALE_LOG2E)
        l_s[:, cols] = alpha * l_s[:, cols] + jnp.sum(p, axis=0, keepdims=True)
        acc_s[:, cols] = alpha * acc_s[:, cols] + _dot(vt_ref[0, kb][:, key_lo:key_lo + n_keys], p.astype(BF16))
        m_s[:, cols] = m_new

    def body(kb, carry):
        block(kb, 0, tk, 0, False)
        return carry

    lax.fori_loop(0, qi, body, 0)
    for part in range(n_parts):
        block(qi, part * pt, pt, part, True)
    o_t = acc_s[...] / l_s[...]
    for h in range(MLA_HEADS):
        for part in range(n_parts):
            c0 = (part * MLA_HEADS + h) * pt
            oh = o_t[:, c0:c0 + pt].astype(BF16)
            asm_s[h * MLA_V:(h + 1) * MLA_V, part * pt:(part + 1) * pt] = _dot(wuv_ref[h], oh).astype(BF16)
    y_ref[0] = x_ref[0] + _dot_tn(asm_s[...], wo_ref[...])


def _mla_attend_prompt(q_t, keys, v_t, x, w):
    b, t, d = x.shape
    nt, tq = v_t.shape[1], v_t.shape[3]
    assert t == nt * tq and tq & (tq - 1) == 0
    rows = MLA_HEADS * tq
    kern = functools.partial(_flash_kernel, tq=tq)
    return pl.pallas_call(
        kern, grid=(b, nt),
        in_specs=[
            pl.BlockSpec((1, 1, MLA_KEY_WIDTH, rows), lambda i, j: (i, j, 0, 0)),
            pl.BlockSpec((1, t, MLA_KEY_WIDTH), lambda i, j: (i, 0, 0)),
            pl.BlockSpec((1, nt, MLA_KV_RANK, tq), lambda i, j: (i, 0, 0, 0)),
            pl.BlockSpec((1, tq, d), lambda i, j: (i, j, 0)),
            _full(w["w_uv_h"].shape),
            _full(w["w_out1"].shape),
        ],
        out_specs=pl.BlockSpec((1, tq, d), lambda i, j: (i, j, 0)),
        out_shape=jax.ShapeDtypeStruct((b, t, d), F32),
        scratch_shapes=[
            pltpu.VMEM((MLA_KV_RANK, rows), F32),
            pltpu.VMEM((1, rows), F32),
            pltpu.VMEM((1, rows), F32),
            pltpu.VMEM((MLA_HEADS * MLA_V, tq), BF16),
        ],
        compiler_params=_params("arbitrary", "arbitrary"), name="mla_flash",
    )(q_t, keys, v_t, x, w["w_uv_h"], w["w_out1"])


def _page_copies(pt_ref, lat_hbm, rope_hbm, lat_buf, rope_buf, sem, seq, slot, start, pages=None):
    def body(p, carry):
        page = pt_ref[seq, p]
        copies = (pltpu.make_async_copy(lat_hbm.at[page], lat_buf.at[slot, p], sem.at[0, slot]),
                  pltpu.make_async_copy(rope_hbm.at[page], rope_buf.at[slot, p], sem.at[1, slot]))
        for c in copies:
            if start:
                c.start(priority=p % 2 if isinstance(p, int) else 0)
            else:
                c.wait()
        return carry

    if pages is None:
        lax.fori_loop(0, lat_buf.shape[1], body, 0)
    else:
        for p in pages:
            body(p, 0)


def _decode_kernel(pt_ref, q_ref, kn_ref, lat_hbm, rope_hbm, o_ref, lat_buf, rope_buf, s_buf, sem, *, group):
    r = MLA_KV_RANK
    n_slots, n_pages, page = lat_buf.shape[0], lat_buf.shape[1], lat_buf.shape[2]
    ahead = n_slots - 1
    b = pl.program_id(0)
    last = pl.num_programs(0) - 1
    slot = lax.rem(b, n_slots)
    args = (pt_ref, lat_hbm, rope_hbm, lat_buf, rope_buf, sem)

    @pl.when(b == 0)
    def _():
        for s0 in range(ahead):
            _page_copies(*args, jnp.minimum(s0, last), s0, True)

    _page_copies(*args, b, slot, False, range(n_pages))
    nxt = jnp.minimum(b + ahead, last)
    nxt_slot = lax.rem(b + ahead, n_slots)

    q = q_ref[0]
    ql = q[:, 0:r]
    qp = q[:, r:r + MLA_ROPE]
    kn = kn_ref[0].astype(F32)
    s_new = jnp.sum(q.astype(F32) * kn, axis=1, keepdims=True)
    lat = lat_buf.at[slot]
    rope = rope_buf.at[slot]
    gk = group * page
    for g in range(n_pages // group):
        lat_g = lat[g * group:(g + 1) * group].reshape(gk, r).astype(BF16)
        s_pe = [_dot(qp, rope[g * group + i].astype(BF16)) for i in range(group)]
        s_buf[:, g * gk:(g + 1) * gk] = _dot_nt(ql, lat_g) + jnp.concatenate(s_pe, axis=1)
        _page_copies(*args, nxt, nxt_slot, True, range(g * group, (g + 1) * group))
    s = s_buf[...]
    m = jnp.maximum(jnp.max(s, axis=1, keepdims=True), s_new)
    p_all = jnp.exp((s - m) * MLA_SCALE)
    p_new = jnp.exp((s_new - m) * MLA_SCALE)
    denom = jnp.sum(p_all, axis=1, keepdims=True) + p_new
    acc = p_new * kn[:, 0:r]
    for g in range(n_pages // group):
        lat_g = lat[g * group:(g + 1) * group].reshape(gk, r).astype(BF16)
        acc = acc + _dot(p_all[:, g * gk:(g + 1) * gk].astype(BF16), lat_g)
    o_ref[0] = acc / denom

    @pl.when(b == last)
    def _():
        for j in range(1, n_slots):
            _page_copies(*args, last, lax.rem(last + j, n_slots), False)


def _mla_attend_sample(q, key_new, cache_latent, cache_rope_t, page_table):
    b = q.shape[0]
    n_pages = page_table.shape[1]
    page = cache_latent.shape[1]
    kw = MLA_KEY_WIDTH
    grid_spec = pltpu.PrefetchScalarGridSpec(
        num_scalar_prefetch=1, grid=(b,),
        in_specs=[
            pl.BlockSpec((1, MLA_HEADS, kw), lambda i, pt: (i, 0, 0)),
            pl.BlockSpec((1, 1, kw), lambda i, pt: (i, 0, 0)),
            pl.BlockSpec(memory_space=pl.ANY),
            pl.BlockSpec(memory_space=pl.ANY),
        ],
        out_specs=pl.BlockSpec((1, MLA_HEADS, MLA_KV_RANK), lambda i, pt: (i, 0, 0)),
        scratch_shapes=[
            pltpu.VMEM((DECODE_SLOTS, n_pages, page, MLA_KV_RANK), F32),
            pltpu.VMEM((DECODE_SLOTS, n_pages, MLA_ROPE, page), F32),
            pltpu.VMEM((MLA_HEADS, n_pages * page), F32),
            pltpu.SemaphoreType.DMA((2, DECODE_SLOTS)),
        ],
    )
    kern = functools.partial(_decode_kernel, group=math.gcd(n_pages, 8))
    return pl.pallas_call(
        kern, grid_spec=grid_spec, out_shape=jax.ShapeDtypeStruct((b, MLA_HEADS, MLA_KV_RANK), F32),
        compiler_params=_params("arbitrary"), name="mla_decode",
    )(page_table, q, key_new, cache_latent, cache_rope_t)


def _attn_out_kernel(o_ref, x_ref, wuv_ref, wo_ref, y_ref, asm_s):
    for h in range(MLA_HEADS):
        oh = o_ref[:, h * MLA_KV_RANK:(h + 1) * MLA_KV_RANK].astype(BF16)
        asm_s[:, h * MLA_V:(h + 1) * MLA_V] = _dot(oh, wuv_ref[h]).astype(BF16)
    y_ref[...] = x_ref[...] + _dot(asm_s[...], wo_ref[...])


def _attn_out_sample(o, x, w):
    m, d = x.shape
    tm = min(128, m)
    assert m % tm == 0
    ow = MLA_HEADS * MLA_KV_RANK
    return pl.pallas_call(
        _attn_out_kernel, grid=(m // tm,),
        in_specs=[pl.BlockSpec((tm, ow), lambda i: (i, 0)), pl.BlockSpec((tm, d), lambda i: (i, 0)),
                  _full(w["w_uv_t"].shape), _full(w["w_out1"].shape)],
        out_specs=pl.BlockSpec((tm, d), lambda i: (i, 0)),
        out_shape=jax.ShapeDtypeStruct((m, d), F32),
        scratch_shapes=[pltpu.VMEM((tm, MLA_HEADS * MLA_V), BF16)],
        compiler_params=_params("arbitrary"), name="attn_out",
    )(o.reshape(m, ow), x, w["w_uv_t"], w["w_out1"])


def _rope_tables(pos):
    half = MLA_ROPE // 2
    inv = ROPE_THETA ** (-jnp.arange(half, dtype=F32) * 2.0 / MLA_ROPE)
    ang = pos.astype(F32)[:, None] * inv[None, :]
    cos = jnp.cos(ang)
    sin = jnp.sin(ang)
    pad = jnp.zeros((pos.shape[0], LANES - MLA_ROPE), F32)
    return (jnp.concatenate([cos, cos, pad], axis=1), jnp.concatenate([-sin, sin, pad], axis=1),
            jnp.concatenate([cos, cos, cos, cos], axis=1), jnp.concatenate([-sin, sin, -sin, sin], axis=1))


def _pad_lanes(a, width):
    return jnp.pad(a, ((0, 0), (0, width - a.shape[1])))


def _prepare_weights(norm_mix, norm_ffn, norm_final, w_in0, mlstm_b_i, mlstm_b_f, mlstm_norm, pool_w,
                     pool_scale, w_out0, w_in1, mla_q_norm, mla_kv_norm, w_q_up, w_uk, w_uv, w_out1,
                     w_up, w_down):
    d = w_in0.shape[0]
    half = MLA_ROPE // 2
    gate_i = w_in0[:, Z_MAIN:Z_MAIN + ML_HEADS]
    gate_f = w_in0[:, Z_MAIN + ML_HEADS:Z_MAIN + 2 * ML_HEADS]
    rope_k = w_in1[:, MLA_Q_RANK + MLA_KV_RANK:]
    rope_k_sw = jnp.concatenate([rope_k[:, half:], rope_k[:, :half]], axis=1)
    q_up = w_q_up.reshape(MLA_Q_RANK, MLA_HEADS, MLA_NOPE + MLA_ROPE)
    q_pe = q_up[:, :, MLA_NOPE:]
    q_pe_sw = jnp.concatenate([q_pe[:, :, half:], q_pe[:, :, :half]], axis=2)
    col8 = lambda v: jnp.pad(v.astype(F32), (0, 8 - ML_HEADS)).reshape(8, 1)
    return {
        "g_mix0": norm_mix[0].reshape(1, d), "g_mix1": norm_mix[1].reshape(1, d),
        "g_ffn0": norm_ffn[0].reshape(1, d), "g_ffn1": norm_ffn[1].reshape(1, d),
        "g_final": norm_final.reshape(1, d),
        "w_in0_gates": _pad_lanes(w_in0, Z_MAIN + LANES).astype(BF16),
        "w_gate_i_t": jnp.pad(gate_i.T, ((0, 8 - ML_HEADS), (0, 0))).astype(BF16),
        "w_gate_f_t": jnp.pad(gate_f.T, ((0, 8 - ML_HEADS), (0, 0))).astype(BF16),
        "b_i_col": col8(mlstm_b_i), "b_f_col": col8(mlstm_b_f),
        "b_i_row": mlstm_b_i.reshape(1, ML_HEADS), "b_f_row": mlstm_b_f.reshape(1, ML_HEADS),
        "mlstm_norm": mlstm_norm.reshape(1, ML_WIDTH),
        "pool_w": pool_w.astype(BF16), "pool_scale": pool_scale.reshape(1, POOL_WIDTH),
        "w_out0": w_out0.astype(BF16),
        "w_in1": jnp.concatenate(
            [w_in1[:, :MLA_Q_RANK + MLA_KV_RANK], rope_k, rope_k_sw],
            axis=1).astype(BF16),
        "mla_q_norm": mla_q_norm.reshape(1, MLA_Q_RANK), "mla_kv_norm": mla_kv_norm.reshape(1, MLA_KV_RANK),
        "w_q_nope": q_up[:, :, :MLA_NOPE].reshape(MLA_Q_RANK, MLA_HEADS * MLA_NOPE).astype(BF16),
        "w_q_pe": jnp.concatenate([q_pe.reshape(MLA_Q_RANK, MLA_HEADS * MLA_ROPE),
                                   q_pe_sw.reshape(MLA_Q_RANK, MLA_HEADS * MLA_ROPE)], axis=1).astype(BF16),
        "w_uk_h": jnp.transpose(w_uk, (1, 0, 2)).astype(BF16),
        "w_uv_t": jnp.transpose(w_uv, (1, 0, 2)).astype(BF16),
        "w_uv_h": jnp.transpose(w_uv, (1, 2, 0)).astype(BF16),
        "w_out1": w_out1.astype(BF16),
        "w_up": w_up.astype(BF16), "w_down": w_down.astype(BF16),
    }


def kernel(x_prompt, x_sample, state_mlstm_C, state_mlstm_n, state_mlstm_m, state_pool, cache_latent, cache_rope_k, page_table, norm_mix, norm_ffn, norm_final, w_in0, mlstm_b_i, mlstm_b_f, mlstm_norm, pool_w, pool_scale, w_out0, w_in1, mla_q_norm, mla_kv_norm, w_q_up, w_uk, w_uv, w_out1, w_up, w_down):
    w = _prepare_weights(norm_mix, norm_ffn, norm_final, w_in0, mlstm_b_i, mlstm_b_f, mlstm_norm, pool_w,
                         pool_scale, w_out0, w_in1, mla_q_norm, mla_kv_norm, w_q_up, w_uk, w_uv, w_out1,
                         w_up, w_down)
    bp, t, d = x_prompt.shape
    bs, ts, _ = x_sample.shape
    assert ts == 1
    past_len = page_table.shape[1] * cache_latent.shape[1]

    x1, c_p, n_p, m_p, buf_p = _mixer0_prompt(x_prompt, w, 0)
    x2 = _ffn(x1.reshape(bp * t, d), w["g_ffn0"], w["w_up"], w["w_down"], 0, w["g_final"], False)
    x2 = x2.reshape(bp, t, d)
    ckv_p, kpe_p, keys_p, vt_p, qt_p = _mla_proj(x2, w, jnp.arange(t))
    x3 = _mla_attend_prompt(qt_p, keys_p, vt_p, x2, w)
    y_p = _ffn(x3.reshape(bp * t, d), w["g_ffn1"], w["w_up"], w["w_down"], 1, w["g_final"], True)
    y_p = y_p.reshape(bp, t, d)

    xs = x_sample.reshape(bs, d)
    xs1, c_s, n_s, m_s, buf_s = _mixer0_sample(xs, state_mlstm_C, state_mlstm_n, state_mlstm_m, state_pool,
                                               w, past_len)
    xs2 = _ffn(xs1, w["g_ffn0"], w["w_up"], w["w_down"], 0, w["g_final"], False)
    ckv_s, kpe_s, keys_s, _, qt_s = _mla_proj(xs2.reshape(1, bs, d), w, jnp.full((bs,), past_len))
    assert qt_s.shape == (1, 1, MLA_KEY_WIDTH, MLA_HEADS * bs)
    q_s = jnp.transpose(qt_s.reshape(MLA_KEY_WIDTH, MLA_HEADS, bs), (2, 1, 0))
    rope_t = jnp.transpose(cache_rope_k, (0, 2, 1))
    o_s = _mla_attend_sample(q_s, keys_s.reshape(bs, 1, MLA_KEY_WIDTH), cache_latent, rope_t, page_table)
    xs3 = _attn_out_sample(o_s, xs2, w)
    y_s = _ffn(xs3, w["g_ffn1"], w["w_up"], w["w_down"], 1, w["g_final"], True)

    return (y_p, y_s.reshape(bs, 1, d), c_p, n_p, m_p, buf_p, ckv_p, kpe_p,
            c_s, n_s, m_s, buf_s, ckv_s.reshape(bs, 1, MLA_KV_RANK), kpe_s.reshape(bs, 1, MLA_ROPE))
```
